```python
import math
import jax, jax.numpy as jnp
from jax import lax
import numpy as np

D_MODEL = 1024
BATCH = 4
SEQ = 8192
DEPTH = 1

ATTN_HEADS = 8
ATTN_HEAD_DIM = 64
ATTN_WIDTH = ATTN_HEADS * ATTN_HEAD_DIM
MOBA_BLOCK = 256
MOBA_TOPK = 3
MOBA_Q_BLOCK = 64
REL_BUCKETS = 32
REL_MAX_DISTANCE = 128
DN_HEADS = 4
DN_HEAD_K = 128
DN_HEAD_V = 128
DN_QK_WIDTH = DN_HEADS * DN_HEAD_K
DN_V_WIDTH = DN_HEADS * DN_HEAD_V
DN_CONV_WIDTH = 4
DN_CONV_CH = 2 * DN_QK_WIDTH + DN_V_WIDTH
DN_CHUNK = 64
D_MIX = ATTN_WIDTH + DN_V_WIDTH
IN_COLS = 4 * ATTN_WIDTH + DN_CONV_CH + DN_V_WIDTH + 2 * DN_HEADS
EPS = 1e-6

kernel_name = "hymba_moba_gated_deltanet_layer"


def rms_norm(x, w):
    xf = x.astype(jnp.float32)
    y = xf * lax.rsqrt(jnp.mean(xf * xf, axis=-1, keepdims=True) + EPS)
    return (y * w.astype(jnp.float32)).astype(x.dtype)


def l2_norm(x):
    xf = x.astype(jnp.float32)
    return xf * lax.rsqrt(jnp.sum(xf * xf, axis=-1, keepdims=True) + EPS)


def t5_bucket(dist):
    n = jnp.maximum(dist, 0)
    max_exact = REL_BUCKETS // 2
    nf = jnp.maximum(n, 1).astype(jnp.float32)
    large = max_exact + (jnp.log(nf / max_exact) / math.log(REL_MAX_DISTANCE / max_exact)
                         * (REL_BUCKETS - max_exact)).astype(jnp.int32)
    large = jnp.minimum(large, REL_BUCKETS - 1)
    return jnp.where(n < max_exact, n, large)


def causal_depthwise_conv(u, w):
    k_w, ch = w.shape
    return lax.conv_general_dilated(u, w[:, None, :], window_strides=(1,), padding=[(k_w - 1, 0)],
                                    dimension_numbers=('NWC', 'WIO', 'NWC'), feature_group_count=ch)


def moba_attention(q, k, v, rel_bias):
    B, H, S, D = q.shape
    nb = -(-S // MOBA_BLOCK)
    s_pad = nb * MOBA_BLOCK
    padw = ((0, 0), (0, 0), (0, s_pad - S), (0, 0))
    q, k, v = jnp.pad(q, padw), jnp.pad(k, padw), jnp.pad(v, padw)
    k_blocks = k.reshape(B, H, nb, MOBA_BLOCK, D)
    v_blocks = v.reshape(B, H, nb, MOBA_BLOCK, D)
    n_gate = max(nb, MOBA_TOPK)
    k_mean = jnp.mean(k_blocks.astype(jnp.float32), axis=3)
    k_mean = jnp.pad(k_mean, ((0, 0), (0, 0), (0, n_gate - nb), (0, 0)))
    scale = D ** -0.5
    b_ix = jnp.arange(B)[:, None, None, None]
    h_ix = jnp.arange(H)[None, :, None, None]
    blk_ar = jnp.arange(MOBA_BLOCK)
    nqb = s_pad // MOBA_Q_BLOCK
    q_sweep = jnp.moveaxis(q.reshape(B, H, nqb, MOBA_Q_BLOCK, D), 2, 0)

    def one_query_block(args):
        qb, q_blk = args
        q_pos = qb * MOBA_Q_BLOCK + jnp.arange(MOBA_Q_BLOCK)
        own = (qb * MOBA_Q_BLOCK) // MOBA_BLOCK
        gate = jnp.einsum('bhqd,bhnd->bhqn', q_blk.astype(jnp.float32), k_mean)
        gate = jnp.where(jnp.arange(n_gate) < own, gate, -jnp.inf)
        _, idx = lax.top_k(gate, MOBA_TOPK)
        valid = idx < own
        idx = jnp.minimum(idx, nb - 1)
        k_sel = k_blocks[b_ix, h_ix, idx]
        v_sel = v_blocks[b_ix, h_ix, idx]
        s_sel = jnp.einsum('bhqd,bhqsjd->bhqsj', q_blk, k_sel).astype(jnp.float32) * scale
        dist_sel = q_pos[None, None, :, None, None] - (idx[..., None] * MOBA_BLOCK + blk_ar)
        s_sel = s_sel + rel_bias[t5_bucket(dist_sel), h_ix[..., None]]
        s_sel = jnp.where(valid[..., None], s_sel, -jnp.inf)
        k_own = lax.dynamic_index_in_dim(k_blocks, own, axis=2, keepdims=False)
        v_own = lax.dynamic_index_in_dim(v_blocks, own, axis=2, keepdims=False)
        dist_own = q_pos[:, None] - (own * MOBA_BLOCK + blk_ar)[None, :]
        s_own = (jnp.einsum('bhqd,bhjd->bhqj', q_blk, k_own).astype(jnp.float32) * scale
                 + jnp.moveaxis(rel_bias[t5_bucket(dist_own)], -1, 0))
        s_own = jnp.where(dist_own >= 0, s_own, -jnp.inf)
        logits = jnp.concatenate([s_own, s_sel.reshape(B, H, MOBA_Q_BLOCK, MOBA_TOPK * MOBA_BLOCK)], axis=-1)
        p = jax.nn.softmax(logits, axis=-1).astype(v.dtype)
        p_own = p[..., :MOBA_BLOCK]
        p_sel = p[..., MOBA_BLOCK:].reshape(B, H, MOBA_Q_BLOCK, MOBA_TOPK, MOBA_BLOCK)
        return (jnp.einsum('bhqj,bhjd->bhqd', p_own, v_own)
                + jnp.einsum('bhqsj,bhqsjd->bhqd', p_sel, v_sel))

    out = lax.map(one_query_block, (jnp.arange(nqb), q_sweep))
    out = jnp.moveaxis(out, 0, 2).reshape(B, H, s_pad, D)
    return out[:, :, :S]


def gated_delta_rule_chunked(q, k, v, g, beta):
    B, H, S, DK = q.shape
    DV = v.shape[-1]
    C = DN_CHUNK
    n = S // C
    q = (q * DK ** -0.5).reshape(B, H, n, C, DK)
    k = k.reshape(B, H, n, C, DK)
    v = v.reshape(B, H, n, C, DV)
    beta = beta.reshape(B, H, n, C)
    g = jnp.cumsum(g.reshape(B, H, n, C), axis=-1)
    incl = jnp.tril(jnp.ones((C, C), dtype=bool))
    strict = jnp.tril(jnp.ones((C, C), dtype=bool), k=-1)
    diff = g[..., :, None] - g[..., None, :]
    decay = jnp.where(incl, jnp.exp(jnp.where(incl, diff, 0.0)), 0.0)
    kb = k * beta[..., None]
    lmat = jnp.where(strict, jnp.einsum('bhnid,bhnjd->bhnij', kb, k) * decay, 0.0)
    amat = lmat + jnp.eye(C, dtype=jnp.float32)
    u = lax.linalg.triangular_solve(amat, v * beta[..., None], left_side=True, lower=True, unit_diagonal=True)
    w = lax.linalg.triangular_solve(amat, kb * jnp.exp(g)[..., None], left_side=True, lower=True, unit_diagonal=True)
    a_intra = jnp.einsum('bhnid,bhnjd->bhnij', q, k) * decay
    q_dec = q * jnp.exp(g)[..., None]
    k_dec = k * jnp.exp(g[..., -1:] - g)[..., None]
    g_last = jnp.exp(g[..., -1])

    def step(state, xs):
        q_i, a_i, w_i, u_i, k_i, gl_i = xs
        v_new = u_i - jnp.einsum('bhcd,bhde->bhce', w_i, state)
        o_i = jnp.einsum('bhcd,bhde->bhce', q_i, state) + jnp.einsum('bhcj,bhje->bhce', a_i, v_new)
        state = state * gl_i[..., None, None] + jnp.einsum('bhcd,bhce->bhde', k_i, v_new)
        return state, o_i

    xs = (jnp.moveaxis(q_dec, 2, 0), jnp.moveaxis(a_intra, 2, 0), jnp.moveaxis(w, 2, 0),
          jnp.moveaxis(u, 2, 0), jnp.moveaxis(k_dec, 2, 0), jnp.moveaxis(g_last, 2, 0))
    _, o = lax.scan(step, jnp.zeros((B, H, DK, DV), jnp.float32), xs)
    return jnp.moveaxis(o, 0, 2).reshape(B, H, S, DV)


def setup_inputs(seed: int = 0) -> dict:
    key = jax.random.key(seed)
    ks = jax.random.split(key, 12)
    f32 = jnp.float32
    x = jax.random.normal(ks[0], (BATCH, SEQ, D_MODEL), f32)
    rel_bias = 0.5 * jax.random.normal(ks[1], (REL_BUCKETS, ATTN_HEADS), f32)
    norm_w = 1.0 + 0.02 * jax.random.normal(ks[2], (DEPTH, D_MODEL), f32)
    w_in = jax.random.normal(ks[3], (DEPTH, D_MODEL, IN_COLS), f32) * D_MODEL ** -0.5
    q_norm_w = 1.0 + 0.02 * jax.random.normal(ks[4], (DEPTH, ATTN_HEAD_DIM), f32)
    k_norm_w = 1.0 + 0.02 * jax.random.normal(ks[5], (DEPTH, ATTN_HEAD_DIM), f32)
    conv_w = jax.random.normal(ks[6], (DEPTH, DN_CONV_WIDTH, DN_CONV_CH), f32) * DN_CONV_WIDTH ** -0.5
    a_log = jnp.log(jax.random.uniform(ks[7], (DEPTH, DN_HEADS), f32, minval=1.0, maxval=16.0))
    dt = jnp.exp(jax.random.uniform(ks[8], (DEPTH, DN_HEADS), f32,
                                    minval=math.log(1e-3), maxval=math.log(1e-1)))
    dt_bias = dt + jnp.log(-jnp.expm1(-dt))
    dn_norm_w = 1.0 + 0.02 * jax.random.normal(ks[9], (DEPTH, DN_HEAD_V), f32)
    w_out = jax.random.normal(ks[10], (DEPTH, D_MIX, D_MODEL), f32) * D_MIX ** -0.5
    return {"x": x, "rel_bias": rel_bias, "norm_w": norm_w, "w_in": w_in,
            "q_norm_w": q_norm_w, "k_norm_w": k_norm_w, "conv_w": conv_w,
            "a_log": a_log, "dt_bias": dt_bias, "dn_norm_w": dn_norm_w, "w_out": w_out}


def reference(x, rel_bias, norm_w, w_in, q_norm_w, k_norm_w, conv_w, a_log, dt_bias, dn_norm_w, w_out):
    B, S, _ = x.shape
    splits = [ATTN_WIDTH, 2 * ATTN_WIDTH, 3 * ATTN_WIDTH, 4 * ATTN_WIDTH,
              4 * ATTN_WIDTH + DN_CONV_CH, 4 * ATTN_WIDTH + DN_CONV_CH + DN_V_WIDTH,
              4 * ATTN_WIDTH + DN_CONV_CH + DN_V_WIDTH + DN_HEADS]
    for layer in range(DEPTH):
        h = rms_norm(x, norm_w[layer])
        proj = jnp.einsum('bsd,dc->bsc', h, w_in[layer])
        q_a, k_a, v_a, z_a, qkv_dn, z_dn, b_dn, a_dn = jnp.split(proj, splits, axis=-1)

        heads_a = lambda t: t.reshape(B, S, ATTN_HEADS, ATTN_HEAD_DIM).transpose(0, 2, 1, 3)
        qa = rms_norm(heads_a(q_a), q_norm_w[layer])
        ka = rms_norm(heads_a(k_a), k_norm_w[layer])
        va = heads_a(v_a)
        o_a = moba_attention(qa, ka, va, rel_bias)
        o_a = o_a.transpose(0, 2, 1, 3).reshape(B, S, ATTN_WIDTH)
        y_a = o_a * jax.nn.silu(z_a)

        qkv_dn = jax.nn.silu(causal_depthwise_conv(qkv_dn, conv_w[layer]))
        q_d, k_d, v_d = jnp.split(qkv_dn, [DN_QK_WIDTH, 2 * DN_QK_WIDTH], axis=-1)
        q_d = l2_norm(q_d.reshape(B, S, DN_HEADS, DN_HEAD_K)).transpose(0, 2, 1, 3)
        k_d = l2_norm(k_d.reshape(B, S, DN_HEADS, DN_HEAD_K)).transpose(0, 2, 1, 3)
        v_d = v_d.reshape(B, S, DN_HEADS, DN_HEAD_V).transpose(0, 2, 1, 3).astype(jnp.float32)
        beta = jax.nn.sigmoid(b_dn.astype(jnp.float32)).transpose(0, 2, 1)
        g = (-jnp.exp(a_log[layer].astype(jnp.float32))
             * jax.nn.softplus(a_dn.astype(jnp.float32) + dt_bias[layer].astype(jnp.float32))).transpose(0, 2, 1)
        o_d = gated_delta_rule_chunked(q_d, k_d, v_d, g, beta)
        o_d = o_d.transpose(0, 2, 1, 3).astype(x.dtype)
        o_d = rms_norm(o_d, dn_norm_w[layer]).reshape(B, S, DN_V_WIDTH)
        y_d = o_d * jax.nn.silu(z_dn)

        y = jnp.concatenate([y_a, y_d], axis=-1)
        x = x + jnp.einsum('bsc,cd->bsd', y, w_out[layer])
    return x
```

```python
import functools
import math

import jax
import jax.numpy as jnp
from jax import lax
from jax.experimental import pallas as pl
from jax.experimental.pallas import tpu as pltpu

F32 = jnp.float32
BF16 = jnp.bfloat16
HI = lax.Precision.HIGHEST

D_MODEL = 1024
ATTN_HEADS = 8
ATTN_HEAD_DIM = 64
ATTN_WIDTH = ATTN_HEADS * ATTN_HEAD_DIM
MOBA_BLOCK = 256
MOBA_TOPK = 3
REL_BUCKETS = 32
REL_MAX_DISTANCE = 128
DN_HEADS = 4
DN_HEAD = 128
DN_WIDTH = DN_HEADS * DN_HEAD
DN_CONV_WIDTH = 4
DN_CHUNK = 64
MAIN_COLS = 4 * ATTN_WIDTH + 3 * DN_WIDTH + DN_WIDTH
GATE_PAD = 128
EPS = 1e-6
LOG2E = math.log2(math.e)
NEG_BIG = -32768.0
VMEM_LIMIT = 56 * 1024 * 1024


def _nt(a, b, precision=None):
    return lax.dot_general(a, b, (((1,), (1,)), ((), ())), preferred_element_type=F32, precision=precision)


def _nn(a, b, precision=None):
    return lax.dot_general(a, b, (((1,), (0,)), ((), ())), preferred_element_type=F32, precision=precision)


def _silu(x):
    return x * (1.0 / (1.0 + jnp.exp(-x)))


def _inproj_kernel(x_ref, nw_ref, w_ref, wg_ref, qnw_ref, knw_ref,
                   q_ref, k_ref, v_ref, za_ref, qkv_ref, zd_ref, g_ref):
    x = x_ref[...]
    ms = jnp.mean(x * x, axis=-1, keepdims=True)
    h = (x * lax.rsqrt(ms + EPS) * nw_ref[...]).astype(BF16)

    r_i = lax.broadcasted_iota(jnp.int32, (ATTN_WIDTH, 128), 0)
    c_i = lax.broadcasted_iota(jnp.int32, (ATTN_WIDTH, 128), 1)
    ind = jnp.where(r_i // ATTN_HEAD_DIM == c_i, 1.0, 0.0).astype(BF16)
    r_e = lax.broadcasted_iota(jnp.int32, (128, ATTN_WIDTH), 0)
    c_e = lax.broadcasted_iota(jnp.int32, (128, ATTN_WIDTH), 1)
    expand = jnp.where(c_e // ATTN_HEAD_DIM == r_e, 1.0, 0.0).astype(BF16)

    def split3(a):
        a0 = a.astype(BF16)
        r1 = a - a0.astype(F32)
        a1 = r1.astype(BF16)
        a2 = (r1 - a1.astype(F32)).astype(BF16)
        return a0, a1, a2

    def head_rms(t, w):
        s0, s1, s2 = split3(t * t)
        ss = _nn(s0, ind) + _nn(s1, ind) + _nn(s2, ind)
        r = lax.rsqrt(ss * (1.0 / ATTN_HEAD_DIM) + EPS)
        r0, r1, r2 = split3(r)
        rf = _nn(r0, expand) + _nn(r1, expand) + _nn(r2, expand)
        return t * rf * w

    def proj(c0, width):
        return _nn(h, w_ref[:, c0:c0 + width])

    q_ref[...] = head_rms(proj(0, ATTN_WIDTH), qnw_ref[...]).astype(q_ref.dtype)
    k_ref[...] = head_rms(proj(ATTN_WIDTH, ATTN_WIDTH), knw_ref[...]).astype(k_ref.dtype)
    v_ref[...] = proj(2 * ATTN_WIDTH, ATTN_WIDTH).astype(v_ref.dtype)
    za_ref[...] = proj(3 * ATTN_WIDTH, ATTN_WIDTH).astype(za_ref.dtype)
    for c in range(3):
        qkv_ref[:, c * DN_WIDTH:(c + 1) * DN_WIDTH] = proj(4 * ATTN_WIDTH + c * DN_WIDTH, DN_WIDTH).astype(qkv_ref.dtype)
    zd_ref[...] = proj(4 * ATTN_WIDTH + 3 * DN_WIDTH, DN_WIDTH).astype(zd_ref.dtype)
    g_ref[...] = _nn(h, wg_ref[...])


def _inproj(x2, norm_w, w_main, w_gate, qnw, knw, tm):
    n = x2.shape[0]
    row = lambda i: (i, 0)
    fixed = lambda i: (0, 0)
    outs = [
        jax.ShapeDtypeStruct((n, ATTN_WIDTH), F32),
        jax.ShapeDtypeStruct((n, ATTN_WIDTH), F32),
        jax.ShapeDtypeStruct((n, ATTN_WIDTH), BF16),
        jax.ShapeDtypeStruct((n, ATTN_WIDTH), F32),
        jax.ShapeDtypeStruct((n, 3 * DN_WIDTH), F32),
        jax.ShapeDtypeStruct((n, DN_WIDTH), F32),
        jax.ShapeDtypeStruct((n, GATE_PAD), F32),
    ]
    return pl.pallas_call(
        _inproj_kernel,
        grid=(n // tm,),
        in_specs=[
            pl.BlockSpec((tm, D_MODEL), row),
            pl.BlockSpec((1, D_MODEL), fixed),
            pl.BlockSpec((D_MODEL, MAIN_COLS), fixed),
            pl.BlockSpec((D_MODEL, GATE_PAD), fixed),
            pl.BlockSpec((1, ATTN_WIDTH), fixed),
            pl.BlockSpec((1, ATTN_WIDTH), fixed),
        ],
        out_specs=[pl.BlockSpec((tm, o.shape[1]), row) for o in outs],
        out_shape=outs,
        compiler_params=pltpu.CompilerParams(dimension_semantics=("arbitrary",), vmem_limit_bytes=VMEM_LIMIT),
        name="inproj",
    )(x2, norm_w, w_main, w_gate, qnw, knw)


def _t5_bias_tile(relb_ref, head, offset):
    r = lax.broadcasted_iota(jnp.int32, (MOBA_BLOCK, MOBA_BLOCK), 0)
    c = lax.broadcasted_iota(jnp.int32, (MOBA_BLOCK, MOBA_BLOCK), 1)
    dist = r - c + offset
    n = jnp.maximum(dist, 0)
    max_exact = REL_BUCKETS // 2
    nf = jnp.maximum(n, 1).astype(F32)
    large = max_exact + (jnp.log(nf / max_exact) / math.log(REL_MAX_DISTANCE / max_exact)
                         * (REL_BUCKETS - max_exact)).astype(jnp.int32)
    large = jnp.minimum(large, REL_BUCKETS - 1)
    bucket = jnp.where(n < max_exact, n, large)
    far = relb_ref[REL_BUCKETS - 1, head]
    bias = jnp.zeros((MOBA_BLOCK, MOBA_BLOCK), F32)
    for t in range(REL_BUCKETS):
        bias = jnp.where(bucket == t, relb_ref[t, head] - far, bias)
    return jnp.where(dist >= 0, bias * LOG2E, NEG_BIG)


def _moba_kernel(relb_ref, q_ref, k_ref, v_ref, z_ref, o_ref,
                 kaug0_ref, kaug1_ref, kmean0_ref, kmean1_ref, bias_ref, *, nb):
    hp = pl.program_id(1)
    i = pl.program_id(2)
    lane = lax.broadcasted_iota(jnp.int32, (MOBA_BLOCK, 128), 1)
    lo_half = lane < ATTN_HEAD_DIM

    @pl.when(i == 0)
    def _prepare():
        kmean0_ref[...] = jnp.zeros_like(kmean0_ref)
        kmean1_ref[...] = jnp.zeros_like(kmean1_ref)

        def prep(j, carry):
            kb = k_ref[0, pl.ds(pl.multiple_of(j * MOBA_BLOCK, MOBA_BLOCK), MOBA_BLOCK), :]
            km = jnp.mean(kb, axis=0, keepdims=True)
            kmean0_ref[pl.ds(ATTN_HEAD_DIM + j, 1), :] = km
            kmean1_ref[pl.ds(j, 1), :] = km
            hot0 = jnp.where(lane == ATTN_HEAD_DIM + j, 1.0, 0.0)
            hot1 = jnp.where(lane == j, 1.0, 0.0)
            rows = pl.ds(pl.multiple_of(j * MOBA_BLOCK, MOBA_BLOCK), MOBA_BLOCK)
            kaug0_ref[rows, :] = jnp.where(lo_half, kb, hot0).astype(BF16)
            kaug1_ref[rows, :] = jnp.where(lo_half, hot1, kb).astype(BF16)
            return carry

        lax.fori_loop(0, nb, prep, 0)
        for hh in range(2):
            bias_ref[hh, 0] = _t5_bias_tile(relb_ref, 2 * hp + hh, 0)
            bias_ref[hh, 1] = _t5_bias_tile(relb_ref, 2 * hp + hh, MOBA_BLOCK)

    q = q_ref[0]
    qs = q * (ATTN_HEAD_DIM ** -0.5 * LOG2E)

    def select(qz, kmean_ref, lane_off):
        gate = _nt(qz, kmean_ref[...], precision=HI)
        blk = lane - lane_off
        g = jnp.where((blk >= 0) & (blk < i), gate, -jnp.inf)
        sel = blk == i
        for _ in range(MOBA_TOPK):
            mx = jnp.max(g, axis=-1, keepdims=True)
            first = jnp.min(jnp.where(g == mx, lane, 128), axis=-1, keepdims=True)
            hit = (lane == first) & (mx > -jnp.inf)
            sel = sel | hit
            g = jnp.where(hit, -jnp.inf, g)
        return jnp.where(sel, 0.0, NEG_BIG)

    mb0 = select(jnp.where(lo_half, q, 0.0), kmean0_ref, ATTN_HEAD_DIM)
    mb1 = select(jnp.where(lo_half, 0.0, q), kmean1_ref, 0)
    qa0 = jnp.where(lo_half, qs, mb0).astype(BF16)
    qa1 = jnp.where(lo_half, mb1, qs).astype(BF16)

    def scores(j):
        rows = pl.ds(pl.multiple_of(j * MOBA_BLOCK, MOBA_BLOCK), MOBA_BLOCK)
        return _nt(qa0, kaug0_ref[rows, :]), _nt(qa1, kaug1_ref[rows, :]), v_ref[0, rows, :]

    def update(state, s, vb):
        m, l, acc = state
        m_new = jnp.maximum(m, jnp.max(s, axis=-1, keepdims=True))
        alpha = jnp.exp2(m - m_new)
        p = jnp.exp2(s - m_new)
        l = alpha * l + jnp.sum(p, axis=-1, keepdims=True)
        acc = alpha * acc + _nn(p.astype(BF16), vb)
        return m_new, l, acc

    def init(s, vb):
        m = jnp.max(s, axis=-1, keepdims=True)
        p = jnp.exp2(s - m)
        return m, jnp.sum(p, axis=-1, keepdims=True), _nn(p.astype(BF16), vb)

    s0, s1, vb = scores(i)
    st0 = init(s0 + bias_ref[0, 0], vb)
    st1 = init(s1 + bias_ref[1, 0], vb)

    def prev_block(sts):
        a0, a1 = sts
        p0, p1, vp = scores(i - 1)
        return update(a0, p0 + bias_ref[0, 1], vp), update(a1, p1 + bias_ref[1, 1], vp)

    st0, st1 = lax.cond(i >= 1, prev_block, lambda sts: sts, (st0, st1))

    def far_block(j, sts):
        a0, a1 = sts
        f0, f1, vf = scores(j)
        return update(a0, f0, vf), update(a1, f1, vf)

    st0, st1 = lax.fori_loop(0, jnp.maximum(i - 1, 0), far_block, (st0, st1))

    o = jnp.where(lo_half, st0[2] / st0[1], st1[2] / st1[1])
    o_ref[0] = (o * _silu(z_ref[0])).astype(o_ref.dtype)


def _moba(rel_bias, q, k, v, z):
    b, s, _ = q.shape
    nb = s // MOBA_BLOCK
    assert s % MOBA_BLOCK == 0 and nb <= ATTN_HEAD_DIM
    pairs = ATTN_HEADS // 2
    blk = lambda bi, hp, i: (bi, i, hp)
    whole = lambda bi, hp, i: (bi, 0, hp)
    return pl.pallas_call(
        functools.partial(_moba_kernel, nb=nb),
        grid=(b, pairs, nb),
        in_specs=[
            pl.BlockSpec(memory_space=pltpu.SMEM),
            pl.BlockSpec((1, MOBA_BLOCK, 128), blk),
            pl.BlockSpec((1, s, 128), whole),
            pl.BlockSpec((1, s, 128), whole),
            pl.BlockSpec((1, MOBA_BLOCK, 128), blk),
        ],
        out_specs=pl.BlockSpec((1, MOBA_BLOCK, 128), blk),
        out_shape=jax.ShapeDtypeStruct((b, s, ATTN_WIDTH), BF16),
        scratch_shapes=[
            pltpu.VMEM((s, 128), BF16),
            pltpu.VMEM((s, 128), BF16),
            pltpu.VMEM((128, 128), F32),
            pltpu.VMEM((128, 128), F32),
            pltpu.VMEM((2, 2, MOBA_BLOCK, MOBA_BLOCK), F32),
        ],
        compiler_params=pltpu.CompilerParams(
            dimension_semantics=("arbitrary", "arbitrary", "arbitrary"), vmem_limit_bytes=VMEM_LIMIT),
        name="moba",
    )(rel_bias, q, k, v, z)


def _deltanet_kernel(alog_ref, dtb_ref, q_ref, k_ref, v_ref, z_ref, g_ref, cwq_ref, cwk_ref, cwv_ref, nw_ref,
                     o_ref, state_ref, pad_ref, *, tb):
    h = pl.program_id(1)
    t = pl.program_id(2)
    C = DN_CHUNK

    @pl.when(t == 0)
    def _reset():
        state_ref[...] = jnp.zeros_like(state_ref)
        pad_ref[:, 0:8, :] = jnp.zeros((3, 8, DN_HEAD), F32)

    def conv_silu(idx, x_ref, cw_ref):
        cur = x_ref[0]
        pad_ref[idx, 8:8 + tb, :] = cur
        acc = jnp.zeros((tb, DN_HEAD), F32)
        for w in range(DN_CONV_WIDTH):
            acc = acc + pad_ref[idx, pl.ds(8 - (DN_CONV_WIDTH - 1) + w, tb), :] * cw_ref[w:w + 1, :]
        pad_ref[idx, 0:8, :] = cur[tb - 8:tb, :]
        return _silu(acc)

    def l2n(x):
        return x * lax.rsqrt(jnp.sum(x * x, axis=-1, keepdims=True) + EPS)

    q_all = l2n(conv_silu(0, q_ref, cwq_ref)) * (DN_HEAD ** -0.5)
    k_all = l2n(conv_silu(1, k_ref, cwk_ref))
    v_all = conv_silu(2, v_ref, cwv_ref)

    gates = g_ref[0]
    lane = lax.broadcasted_iota(jnp.int32, (tb, GATE_PAD), 1)
    b_raw = jnp.sum(jnp.where(lane == h, gates, 0.0), axis=-1, keepdims=True)
    a_raw = jnp.sum(jnp.where(lane == DN_HEADS + h, gates, 0.0), axis=-1, keepdims=True)
    beta_all = 1.0 / (1.0 + jnp.exp(-b_raw))
    xs = a_raw + dtb_ref[h]
    softplus = jnp.maximum(xs, 0.0) + jnp.log(1.0 + jnp.exp(-jnp.abs(xs)))
    g_all = -jnp.exp(alog_ref[h]) * softplus

    ri = lax.broadcasted_iota(jnp.int32, (C, C), 0)
    ci = lax.broadcasted_iota(jnp.int32, (C, C), 1)
    incl = ri >= ci
    strict = ri > ci
    tri = jnp.where(incl, 1.0, 0.0)
    eye = jnp.where(ri == ci, 1.0, 0.0)
    lane_c = lax.broadcasted_iota(jnp.int32, (C, DN_HEAD), 1)

    for c in range(tb // C):
        rows = slice(c * C, (c + 1) * C)
        q, k, v = q_all[rows], k_all[rows], v_all[rows]
        beta = beta_all[rows]
        gc = _nn(tri, jnp.broadcast_to(g_all[rows], (C, DN_HEAD)), precision=HI)
        gcol = gc[:, 0:1]
        glast = gc[C - 1:C, 0:1]
        u_m = jnp.where(lane_c == 0, gc, jnp.where(lane_c == 1, 1.0, 0.0))
        w_m = jnp.where(lane_c == 0, 1.0, jnp.where(lane_c == 1, -gc, 0.0))
        diff = _nt(u_m, w_m, precision=HI)
        decay = jnp.where(incl, jnp.exp(jnp.where(incl, diff, 0.0)), 0.0)
        kb = k * beta
        lmat = jnp.where(strict, _nt(kb, k, precision=HI) * decay, 0.0)
        x = -lmat
        tinv = eye + x
        pw = x
        for _ in range(int(math.log2(C)) - 1):
            pw = _nn(pw, pw, precision=HI)
            tinv = tinv + _nn(tinv, pw, precision=HI)
        eg = jnp.exp(gcol)
        u = _nn(tinv, v * beta, precision=HI)
        w = _nn(tinv, kb * eg, precision=HI)
        a_intra = jnp.where(incl, _nt(q, k, precision=HI) * decay, 0.0)
        q_dec = q * eg
        k_dec = k * jnp.exp(glast - gcol)
        state = state_ref[...]
        v_new = u - _nn(w, state, precision=HI)
        o = _nn(q_dec, state, precision=HI) + _nn(a_intra, v_new, precision=HI)
        state_ref[...] = state * jnp.exp(glast) + lax.dot_general(
            k_dec, v_new, (((0,), (0,)), ((), ())), preferred_element_type=F32, precision=HI)
        on = o * lax.rsqrt(jnp.mean(o * o, axis=-1, keepdims=True) + EPS) * nw_ref[...]
        o_ref[0, rows, :] = (on * _silu(z_ref[0, rows, :])).astype(o_ref.dtype)


def _deltanet(a_log, dt_bias, qkv, z, gates, conv_w, dn_norm_w, tb):
    b, s, _ = qkv.shape
    assert s % tb == 0 and tb % DN_CHUNK == 0
    col = lambda off: (lambda bi, h, t: (bi, t, off + h))
    cw = lambda off: (lambda bi, h, t: (0, off + h))
    return pl.pallas_call(
        functools.partial(_deltanet_kernel, tb=tb),
        grid=(b, DN_HEADS, s // tb),
        in_specs=[
            pl.BlockSpec(memory_space=pltpu.SMEM),
            pl.BlockSpec(memory_space=pltpu.SMEM),
            pl.BlockSpec((1, tb, DN_HEAD), col(0)),
            pl.BlockSpec((1, tb, DN_HEAD), col(DN_HEADS)),
            pl.BlockSpec((1, tb, DN_HEAD), col(2 * DN_HEADS)),
            pl.BlockSpec((1, tb, DN_HEAD), col(0)),
            pl.BlockSpec((1, tb, GATE_PAD), lambda bi, h, t: (bi, t, 0)),
            pl.BlockSpec((DN_CONV_WIDTH, DN_HEAD), cw(0)),
            pl.BlockSpec((DN_CONV_WIDTH, DN_HEAD), cw(DN_HEADS)),
            pl.BlockSpec((DN_CONV_WIDTH, DN_HEAD), cw(2 * DN_HEADS)),
            pl.BlockSpec((1, DN_HEAD), lambda bi, h, t: (0, 0)),
        ],
        out_specs=pl.BlockSpec((1, tb, DN_HEAD), col(0)),
        out_shape=jax.ShapeDtypeStruct((b, s, DN_WIDTH), BF16),
        scratch_shapes=[
            pltpu.VMEM((DN_HEAD, DN_HEAD), F32),
            pltpu.VMEM((3, tb + 8, DN_HEAD), F32),
        ],
        compiler_params=pltpu.CompilerParams(
            dimension_semantics=("arbitrary", "arbitrary", "arbitrary"), vmem_limit_bytes=VMEM_LIMIT),
        name="deltanet",
    )(a_log, dt_bias, qkv, qkv, qkv, z, gates, conv_w, conv_w, conv_w, dn_norm_w)


def _outproj_kernel(x_ref, ya_ref, yd_ref, wa_ref, wd_ref, o_ref):
    o_ref[...] = x_ref[...] + _nn(ya_ref[...], wa_ref[...]) + _nn(yd_ref[...], wd_ref[...])


def _outproj(x2, ya, yd, wa, wd, tm):
    n = x2.shape[0]
    row = lambda i: (i, 0)
    fixed = lambda i: (0, 0)
    return pl.pallas_call(
        _outproj_kernel,
        grid=(n // tm,),
        in_specs=[
            pl.BlockSpec((tm, D_MODEL), row),
            pl.BlockSpec((tm, ATTN_WIDTH), row),
            pl.BlockSpec((tm, DN_WIDTH), row),
            pl.BlockSpec((ATTN_WIDTH, D_MODEL), fixed),
            pl.BlockSpec((DN_WIDTH, D_MODEL), fixed),
        ],
        out_specs=pl.BlockSpec((tm, D_MODEL), row),
        out_shape=jax.ShapeDtypeStruct((n, D_MODEL), F32),
        compiler_params=pltpu.CompilerParams(dimension_semantics=("arbitrary",), vmem_limit_bytes=VMEM_LIMIT),
        name="outproj",
    )(x2, ya, yd, wa, wd)


def _row_tile(n):
    for tm in (512, 256, 128, 64, 32, 16, 8):
        if n % tm == 0:
            return tm
    raise ValueError(f"row count {n} is not a multiple of 8")


def kernel(x, rel_bias, norm_w, w_in, q_norm_w, k_norm_w, conv_w, a_log, dt_bias, dn_norm_w, w_out):
    b, s, d = x.shape
    assert d == D_MODEL and norm_w.shape[0] == 1, "single-layer kernel"
    n = b * s
    x2 = x.reshape(n, d)
    tm = _row_tile(n)

    w_main = w_in[0][:, :MAIN_COLS].astype(BF16)
    w_gate = jnp.pad(w_in[0][:, MAIN_COLS:], ((0, 0), (0, GATE_PAD - 2 * DN_HEADS))).astype(BF16)
    qnw = jnp.tile(q_norm_w[0], ATTN_HEADS)[None, :]
    knw = jnp.tile(k_norm_w[0], ATTN_HEADS)[None, :]

    q, k, v, za, qkv, zd, gates = _inproj(x2, norm_w, w_main, w_gate, qnw, knw, tm)

    r3 = lambda t: t.reshape(b, s, t.shape[-1])
    ya = _moba(rel_bias, r3(q), r3(k), r3(v), r3(za))
    tb = 256 if s % 256 == 0 else DN_CHUNK
    yd = _deltanet(a_log[0], dt_bias[0], r3(qkv), r3(zd), r3(gates), conv_w[0], dn_norm_w, tb)

    w_o = w_out[0].astype(BF16)
    out = _outproj(x2, ya.reshape(n, ATTN_WIDTH), yd.reshape(n, DN_WIDTH), w_o[:ATTN_WIDTH], w_o[ATTN_WIDTH:], tm)
    return out.reshape(b, s, d)
```

```python
import functools
import math

import jax
import jax.numpy as jnp
from jax import lax
from jax.experimental import pallas as pl
from jax.experimental.pallas import tpu as pltpu

F32 = jnp.float32
BF16 = jnp.bfloat16
HI = lax.Precision.HIGHEST

D_MODEL = 1024
ATTN_HEADS = 8
ATTN_HEAD_DIM = 64
ATTN_WIDTH = ATTN_HEADS * ATTN_HEAD_DIM
MOBA_BLOCK = 256
MOBA_TOPK = 3
REL_BUCKETS = 32
REL_MAX_DISTANCE = 128
DN_HEADS = 4
DN_HEAD = 128
DN_WIDTH = DN_HEADS * DN_HEAD
DN_CONV_WIDTH = 4
DN_CHUNK = 64
MAIN_COLS = 4 * ATTN_WIDTH + 3 * DN_WIDTH + DN_WIDTH
GATE_PAD = 128
EPS = 1e-6
LOG2E = math.log2(math.e)
NEG_BIG = -32768.0
VMEM_LIMIT = 56 * 1024 * 1024


def _nt(a, b, precision=None):
    return lax.dot_general(a, b, (((1,), (1,)), ((), ())), preferred_element_type=F32, precision=precision)


def _nn(a, b, precision=None):
    return lax.dot_general(a, b, (((1,), (0,)), ((), ())), preferred_element_type=F32, precision=precision)


def _silu(x):
    return x * (1.0 / (1.0 + jnp.exp(-x)))


def _inproj_kernel(x_ref, nw_ref, w_ref, wg_ref, qnw_ref, knw_ref,
                   q_ref, k_ref, v_ref, za_ref, qkv_ref, zd_ref, g_ref):
    x = x_ref[...]
    ms = jnp.mean(x * x, axis=-1, keepdims=True)
    h = (x * lax.rsqrt(ms + EPS) * nw_ref[...]).astype(BF16)

    r_i = lax.broadcasted_iota(jnp.int32, (ATTN_WIDTH, 128), 0)
    c_i = lax.broadcasted_iota(jnp.int32, (ATTN_WIDTH, 128), 1)
    ind = jnp.where(r_i // ATTN_HEAD_DIM == c_i, 1.0, 0.0).astype(BF16)
    r_e = lax.broadcasted_iota(jnp.int32, (128, ATTN_WIDTH), 0)
    c_e = lax.broadcasted_iota(jnp.int32, (128, ATTN_WIDTH), 1)
    expand = jnp.where(c_e // ATTN_HEAD_DIM == r_e, 1.0, 0.0).astype(BF16)

    def split3(a):
        a0 = a.astype(BF16)
        r1 = a - a0.astype(F32)
        a1 = r1.astype(BF16)
        a2 = (r1 - a1.astype(F32)).astype(BF16)
        return a0, a1, a2

    def head_rms(t, w):
        s0, s1, s2 = split3(t * t)
        ss = _nn(s0, ind) + _nn(s1, ind) + _nn(s2, ind)
        r = lax.rsqrt(ss * (1.0 / ATTN_HEAD_DIM) + EPS)
        r0, r1, r2 = split3(r)
        rf = _nn(r0, expand) + _nn(r1, expand) + _nn(r2, expand)
        return t * rf * w

    def proj(c0, width):
        return _nn(h, w_ref[:, c0:c0 + width])

    q_ref[...] = head_rms(proj(0, ATTN_WIDTH), qnw_ref[...]).astype(q_ref.dtype)
    k_ref[...] = head_rms(proj(ATTN_WIDTH, ATTN_WIDTH), knw_ref[...]).astype(k_ref.dtype)
    v_ref[...] = proj(2 * ATTN_WIDTH, ATTN_WIDTH).astype(v_ref.dtype)
    za_ref[...] = proj(3 * ATTN_WIDTH, ATTN_WIDTH).astype(za_ref.dtype)
    for c in range(3):
        qkv_ref[:, c * DN_WIDTH:(c + 1) * DN_WIDTH] = proj(4 * ATTN_WIDTH + c * DN_WIDTH, DN_WIDTH).astype(qkv_ref.dtype)
    zd_ref[...] = proj(4 * ATTN_WIDTH + 3 * DN_WIDTH, DN_WIDTH).astype(zd_ref.dtype)
    g_ref[...] = _nn(h, wg_ref[...])


def _inproj(x2, norm_w, w_main, w_gate, qnw, knw, tm):
    n = x2.shape[0]
    row = lambda i: (i, 0)
    fixed = lambda i: (0, 0)
    outs = [
        jax.ShapeDtypeStruct((n, ATTN_WIDTH), F32),
        jax.ShapeDtypeStruct((n, ATTN_WIDTH), F32),
        jax.ShapeDtypeStruct((n, ATTN_WIDTH), BF16),
        jax.ShapeDtypeStruct((n, ATTN_WIDTH), F32),
        jax.ShapeDtypeStruct((n, 3 * DN_WIDTH), F32),
        jax.ShapeDtypeStruct((n, DN_WIDTH), F32),
        jax.ShapeDtypeStruct((n, GATE_PAD), F32),
    ]
    return pl.pallas_call(
        _inproj_kernel,
        grid=(n // tm,),
        in_specs=[
            pl.BlockSpec((tm, D_MODEL), row),
            pl.BlockSpec((1, D_MODEL), fixed),
            pl.BlockSpec((D_MODEL, MAIN_COLS), fixed),
            pl.BlockSpec((D_MODEL, GATE_PAD), fixed),
            pl.BlockSpec((1, ATTN_WIDTH), fixed),
            pl.BlockSpec((1, ATTN_WIDTH), fixed),
        ],
        out_specs=[pl.BlockSpec((tm, o.shape[1]), row) for o in outs],
        out_shape=outs,
        compiler_params=pltpu.CompilerParams(dimension_semantics=("arbitrary",), vmem_limit_bytes=VMEM_LIMIT),
        name="inproj",
    )(x2, norm_w, w_main, w_gate, qnw, knw)


def _t5_bias_tile_t(relb_ref, head, offset):
    c = lax.broadcasted_iota(jnp.int32, (MOBA_BLOCK, MOBA_BLOCK), 0)
    r = lax.broadcasted_iota(jnp.int32, (MOBA_BLOCK, MOBA_BLOCK), 1)
    dist = r - c + offset
    n = jnp.maximum(dist, 0)
    max_exact = REL_BUCKETS // 2
    nf = jnp.maximum(n, 1).astype(F32)
    large = max_exact + (jnp.log(nf / max_exact) / math.log(REL_MAX_DISTANCE / max_exact)
                         * (REL_BUCKETS - max_exact)).astype(jnp.int32)
    large = jnp.minimum(large, REL_BUCKETS - 1)
    bucket = jnp.where(n < max_exact, n, large)
    far = relb_ref[REL_BUCKETS - 1, head]
    bias = jnp.zeros((MOBA_BLOCK, MOBA_BLOCK), F32)
    for t in range(REL_BUCKETS):
        bias = jnp.where(bucket == t, relb_ref[t, head] - far, bias)
    return jnp.where(dist >= 0, bias * LOG2E, NEG_BIG)


def _moba_kernel(relb_ref, q_ref, k_ref, v_ref, z_ref, o_ref,
                 kaug_ref, vt_ref, kmean_ref, bias_ref, acc_ref, qa_ref, sa_ref, sb_ref, *, nb):
    hp = pl.program_id(1)
    i = pl.program_id(2)
    HD = ATTN_HEAD_DIM
    lane = lax.broadcasted_iota(jnp.int32, (MOBA_BLOCK, 128), 1)
    lo_half = lane < HD
    er = lax.broadcasted_iota(jnp.int32, (128, 128), 0)
    ec = lax.broadcasted_iota(jnp.int32, (128, 128), 1)
    eye = jnp.where(er == ec, 1.0, 0.0).astype(BF16)

    @pl.when(i == 0)
    def _prepare():
        kmean_ref[...] = jnp.zeros_like(kmean_ref)
        lane1 = lax.broadcasted_iota(jnp.int32, (1, 128), 1)

        def prep(j, carry):
            rows = pl.ds(pl.multiple_of(j * MOBA_BLOCK, MOBA_BLOCK), MOBA_BLOCK)
            kb = k_ref[0, rows, :]
            km = jnp.mean(kb, axis=0, keepdims=True)
            kmean_ref[0, pl.ds(j, 1), :] = jnp.where(lane1 < HD, km, 0.0)
            kmean_ref[1, pl.ds(j, 1), :] = jnp.where(lane1 < HD, 0.0, km)
            kaug_ref[0, j] = jnp.where(lo_half, kb, jnp.where(lane == HD + j, 1.0, 0.0)).astype(BF16)
            kaug_ref[1, j] = jnp.where(lo_half, jnp.where(lane == j, 1.0, 0.0), kb).astype(BF16)
            vt_ref[j] = _nt(eye, v_ref[0, rows, :]).astype(BF16)
            return carry

        lax.fori_loop(0, nb, prep, 0)
        kaug_ref[0, nb] = jnp.where(lane == 2 * HD - 1, 1.0, 0.0).astype(BF16)
        kaug_ref[1, nb] = jnp.where(lane == HD - 1, 1.0, 0.0).astype(BF16)
        vt_ref[nb] = jnp.zeros((128, MOBA_BLOCK), BF16)
        for hh in range(2):
            bias_ref[hh, 0] = _t5_bias_tile_t(relb_ref, 2 * hp + hh, 0)
            bias_ref[hh, 1] = _t5_bias_tile_t(relb_ref, 2 * hp + hh, MOBA_BLOCK)

    q = q_ref[0]
    qt = _nt(eye, (q * (HD ** -0.5 * LOG2E)).astype(BF16))
    blk = lax.broadcasted_iota(jnp.int32, (HD, MOBA_BLOCK), 0)

    def select(hh):
        gate = _nt(kmean_ref[hh], q, precision=HI)
        g = jnp.where(blk < i, gate, -jnp.inf)
        sel = blk == i
        for _ in range(MOBA_TOPK):
            mx = jnp.max(g, axis=0, keepdims=True)
            first = jnp.min(jnp.where(g == mx, blk, HD), axis=0, keepdims=True)
            hit = (blk == first) & (mx > -jnp.inf)
            sel = sel | hit
            g = jnp.where(hit, -jnp.inf, g)
        return jnp.where(sel, 0.0, NEG_BIG)

    qa_ref[0] = jnp.concatenate([qt[0:HD], select(0)], axis=0).astype(BF16)
    qa_ref[1] = jnp.concatenate([select(1), qt[HD:2 * HD]], axis=0).astype(BF16)

    def issue(j, s_ref):
        for hh in range(2):
            s_ref[hh] = _nn(kaug_ref[hh, j], qa_ref[hh])

    def consume(j, s_ref, st, bias_idx=None, first=False):
        out = []
        for hh in range(2):
            m, l = st[2 * hh], st[2 * hh + 1]
            s = s_ref[hh]
            if bias_idx is not None:
                s = s + bias_ref[hh, bias_idx]
            m_cur = jnp.max(s, axis=0, keepdims=True)
            m_new = m_cur if first else jnp.maximum(m, m_cur)
            p = jnp.exp2(s - m_new)
            ps = jnp.sum(p, axis=0, keepdims=True)
            pv = _nn(vt_ref[j, pl.ds(hh * HD, HD), :], p.astype(BF16))
            rows = pl.ds(hh * HD, HD)
            if first:
                acc_ref[rows, :] = pv
                out += [m_new, ps]
            else:
                alpha = jnp.exp2(m - m_new)
                acc_ref[rows, :] = alpha * acc_ref[rows, :] + pv
                out += [m_new, alpha * l + ps]
        return tuple(out)

    nfar = jnp.maximum(i - 1, 0)
    pick = lambda j, ok: jnp.where(ok, j, nb)
    issue(i, sa_ref)
    issue(pick(i - 1, i >= 1), sb_ref)
    st = consume(i, sa_ref, (None,) * 4, bias_idx=0, first=True)
    issue(pick(0, nfar >= 1), sa_ref)
    st = consume(pick(i - 1, i >= 1), sb_ref, st, bias_idx=1)

    def far_pair(t, st):
        a = 2 * t
        issue(pick(a + 1, a + 1 < nfar), sb_ref)
        st = consume(a, sa_ref, st)
        issue(pick(a + 2, a + 2 < nfar), sa_ref)
        return consume(pick(a + 1, a + 1 < nfar), sb_ref, st)

    m0, l0, m1, l1 = lax.fori_loop(0, (nfar + 1) // 2, far_pair, st)

    half = lax.broadcasted_iota(jnp.int32, (2 * HD, MOBA_BLOCK), 0) < HD
    ot = acc_ref[...] * jnp.where(half, 1.0 / l0, 1.0 / l1)
    o_ref[0] = (ot.T * _silu(z_ref[0])).astype(o_ref.dtype)


def _moba(rel_bias, q, k, v, z):
    b, s, _ = q.shape
    nb = s // MOBA_BLOCK
    assert s % MOBA_BLOCK == 0 and nb < ATTN_HEAD_DIM
    pairs = ATTN_HEADS // 2
    blk = lambda bi, hp, i: (bi, i, hp)
    whole = lambda bi, hp, i: (bi, 0, hp)
    return pl.pallas_call(
        functools.partial(_moba_kernel, nb=nb),
        grid=(b, pairs, nb),
        in_specs=[
            pl.BlockSpec(memory_space=pltpu.SMEM),
            pl.BlockSpec((1, MOBA_BLOCK, 128), blk),
            pl.BlockSpec((1, s, 128), whole),
            pl.BlockSpec((1, s, 128), whole),
            pl.BlockSpec((1, MOBA_BLOCK, 128), blk),
        ],
        out_specs=pl.BlockSpec((1, MOBA_BLOCK, 128), blk),
        out_shape=jax.ShapeDtypeStruct((b, s, ATTN_WIDTH), BF16),
        scratch_shapes=[
            pltpu.VMEM((2, nb + 1, MOBA_BLOCK, 128), BF16),
            pltpu.VMEM((nb + 1, 128, MOBA_BLOCK), BF16),
            pltpu.VMEM((2, ATTN_HEAD_DIM, 128), F32),
            pltpu.VMEM((2, 2, MOBA_BLOCK, MOBA_BLOCK), F32),
            pltpu.VMEM((128, MOBA_BLOCK), F32),
            pltpu.VMEM((2, 128, MOBA_BLOCK), BF16),
            pltpu.VMEM((2, MOBA_BLOCK, MOBA_BLOCK), F32),
            pltpu.VMEM((2, MOBA_BLOCK, MOBA_BLOCK), F32),
        ],
        compiler_params=pltpu.CompilerParams(
            dimension_semantics=("arbitrary", "arbitrary", "arbitrary"), vmem_limit_bytes=VMEM_LIMIT),
        name="moba",
    )(rel_bias, q, k, v, z)


def _deltanet_kernel(alog_ref, dtb_ref, q_ref, k_ref, v_ref, z_ref, g_ref, cwq_ref, cwk_ref, cwv_ref, nw_ref,
                     o_ref, state_ref, pad_ref, *, tb):
    h = pl.program_id(1)
    t = pl.program_id(2)
    C = DN_CHUNK

    @pl.when(t == 0)
    def _reset():
        state_ref[...] = jnp.zeros_like(state_ref)
        pad_ref[:, 0:8, :] = jnp.zeros((3, 8, DN_HEAD), F32)

    def conv_silu(idx, x_ref, cw_ref):
        cur = x_ref[0]
        pad_ref[idx, 8:8 + tb, :] = cur
        acc = jnp.zeros((tb, DN_HEAD), F32)
        for w in range(DN_CONV_WIDTH):
            acc = acc + pad_ref[idx, pl.ds(8 - (DN_CONV_WIDTH - 1) + w, tb), :] * cw_ref[w:w + 1, :]
        pad_ref[idx, 0:8, :] = cur[tb - 8:tb, :]
        return _silu(acc)

    def l2n(x):
        return x * lax.rsqrt(jnp.sum(x * x, axis=-1, keepdims=True) + EPS)

    q_all = l2n(conv_silu(0, q_ref, cwq_ref)) * (DN_HEAD ** -0.5)
    k_all = l2n(conv_silu(1, k_ref, cwk_ref))
    v_all = conv_silu(2, v_ref, cwv_ref)

    gates = g_ref[0]
    lane = lax.broadcasted_iota(jnp.int32, (tb, GATE_PAD), 1)
    b_raw = jnp.sum(jnp.where(lane == h, gates, 0.0), axis=-1, keepdims=True)
    a_raw = jnp.sum(jnp.where(lane == DN_HEADS + h, gates, 0.0), axis=-1, keepdims=True)
    beta_all = 1.0 / (1.0 + jnp.exp(-b_raw))
    xs = a_raw + dtb_ref[h]
    softplus = jnp.maximum(xs, 0.0) + jnp.log(1.0 + jnp.exp(-jnp.abs(xs)))
    g_all = -jnp.exp(alog_ref[h]) * softplus

    ri = lax.broadcasted_iota(jnp.int32, (C, C), 0)
    ci = lax.broadcasted_iota(jnp.int32, (C, C), 1)
    incl = ri >= ci
    strict = ri > ci
    tri = jnp.where(incl, 1.0, 0.0)
    eye = jnp.where(ri == ci, 1.0, 0.0)
    lane_c = lax.broadcasted_iota(jnp.int32, (C, DN_HEAD), 1)

    for c in range(tb // C):
        rows = slice(c * C, (c + 1) * C)
        q, k, v = q_all[rows], k_all[rows], v_all[rows]
        beta = beta_all[rows]
        gc = _nn(tri, jnp.broadcast_to(g_all[rows], (C, DN_HEAD)), precision=HI)
        gcol = gc[:, 0:1]
        glast = gc[C - 1:C, 0:1]
        u_m = jnp.where(lane_c == 0, gc, jnp.where(lane_c == 1, 1.0, 0.0))
        w_m = jnp.where(lane_c == 0, 1.0, jnp.where(lane_c == 1, -gc, 0.0))
        diff = _nt(u_m, w_m, precision=HI)
        decay = jnp.where(incl, jnp.exp(jnp.where(incl, diff, 0.0)), 0.0)
        kb = k * beta
        lmat = jnp.where(strict, _nt(kb, k, precision=HI) * decay, 0.0)
        x = -lmat
        tinv = eye + x
        pw = x
        for _ in range(int(math.log2(C)) - 1):
            pw = _nn(pw, pw, precision=HI)
            tinv = tinv + _nn(tinv, pw, precision=HI)
        eg = jnp.exp(gcol)
        u = _nn(tinv, v * beta, precision=HI)
        w = _nn(tinv, kb * eg, precision=HI)
        a_intra = jnp.where(incl, _nt(q, k, precision=HI) * decay, 0.0)
        q_dec = q * eg
        k_dec = k * jnp.exp(glast - gcol)
        state = state_ref[...]
        v_new = u - _nn(w, state, precision=HI)
        o = _nn(q_dec, state, precision=HI) + _nn(a_intra, v_new, precision=HI)
        state_ref[...] = state * jnp.exp(glast) + lax.dot_general(
            k_dec, v_new, (((0,), (0,)), ((), ())), preferred_element_type=F32, precision=HI)
        on = o * lax.rsqrt(jnp.mean(o * o, axis=-1, keepdims=True) + EPS) * nw_ref[...]
        o_ref[0, rows, :] = (on * _silu(z_ref[0, rows, :])).astype(o_ref.dtype)


def _deltanet(a_log, dt_bias, qkv, z, gates, conv_w, dn_norm_w, tb):
    b, s, _ = qkv.shape
    assert s % tb == 0 and tb % DN_CHUNK == 0
    col = lambda off: (lambda bi, h, t: (bi, t, off + h))
    cw = lambda off: (lambda bi, h, t: (0, off + h))
    return pl.pallas_call(
        functools.partial(_deltanet_kernel, tb=tb),
        grid=(b, DN_HEADS, s // tb),
        in_specs=[
            pl.BlockSpec(memory_space=pltpu.SMEM),
            pl.BlockSpec(memory_space=pltpu.SMEM),
            pl.BlockSpec((1, tb, DN_HEAD), col(0)),
            pl.BlockSpec((1, tb, DN_HEAD), col(DN_HEADS)),
            pl.BlockSpec((1, tb, DN_HEAD), col(2 * DN_HEADS)),
            pl.BlockSpec((1, tb, DN_HEAD), col(0)),
            pl.BlockSpec((1, tb, GATE_PAD), lambda bi, h, t: (bi, t, 0)),
            pl.BlockSpec((DN_CONV_WIDTH, DN_HEAD), cw(0)),
            pl.BlockSpec((DN_CONV_WIDTH, DN_HEAD), cw(DN_HEADS)),
            pl.BlockSpec((DN_CONV_WIDTH, DN_HEAD), cw(2 * DN_HEADS)),
            pl.BlockSpec((1, DN_HEAD), lambda bi, h, t: (0, 0)),
        ],
        out_specs=pl.BlockSpec((1, tb, DN_HEAD), col(0)),
        out_shape=jax.ShapeDtypeStruct((b, s, DN_WIDTH), BF16),
        scratch_shapes=[
            pltpu.VMEM((DN_HEAD, DN_HEAD), F32),
            pltpu.VMEM((3, tb + 8, DN_HEAD), F32),
        ],
        compiler_params=pltpu.CompilerParams(
            dimension_semantics=("arbitrary", "arbitrary", "arbitrary"), vmem_limit_bytes=VMEM_LIMIT),
        name="deltanet",
    )(a_log, dt_bias, qkv, qkv, qkv, z, gates, conv_w, conv_w, conv_w, dn_norm_w)


def _outproj_kernel(x_ref, ya_ref, yd_ref, wa_ref, wd_ref, o_ref):
    o_ref[...] = x_ref[...] + _nn(ya_ref[...], wa_ref[...]) + _nn(yd_ref[...], wd_ref[...])


def _outproj(x2, ya, yd, wa, wd, tm):
    n = x2.shape[0]
    row = lambda i: (i, 0)
    fixed = lambda i: (0, 0)
    return pl.pallas_call(
        _outproj_kernel,
        grid=(n // tm,),
        in_specs=[
            pl.BlockSpec((tm, D_MODEL), row),
            pl.BlockSpec((tm, ATTN_WIDTH), row),
            pl.BlockSpec((tm, DN_WIDTH), row),
            pl.BlockSpec((ATTN_WIDTH, D_MODEL), fixed),
            pl.BlockSpec((DN_WIDTH, D_MODEL), fixed),
        ],
        out_specs=pl.BlockSpec((tm, D_MODEL), row),
        out_shape=jax.ShapeDtypeStruct((n, D_MODEL), F32),
        compiler_params=pltpu.CompilerParams(dimension_semantics=("arbitrary",), vmem_limit_bytes=VMEM_LIMIT),
        name="outproj",
    )(x2, ya, yd, wa, wd)


def _row_tile(n):
    for tm in (512, 256, 128, 64, 32, 16, 8):
        if n % tm == 0:
            return tm
    raise ValueError(f"row count {n} is not a multiple of 8")


def kernel(x, rel_bias, norm_w, w_in, q_norm_w, k_norm_w, conv_w, a_log, dt_bias, dn_norm_w, w_out):
    b, s, d = x.shape
    assert d == D_MODEL and norm_w.shape[0] == 1, "single-layer kernel"
    n = b * s
    x2 = x.reshape(n, d)
    tm = _row_tile(n)

    w_main = w_in[0][:, :MAIN_COLS].astype(BF16)
    w_gate = jnp.pad(w_in[0][:, MAIN_COLS:], ((0, 0), (0, GATE_PAD - 2 * DN_HEADS))).astype(BF16)
    qnw = jnp.tile(q_norm_w[0], ATTN_HEADS)[None, :]
    knw = jnp.tile(k_norm_w[0], ATTN_HEADS)[None, :]

    q, k, v, za, qkv, zd, gates = _inproj(x2, norm_w, w_main, w_gate, qnw, knw, tm)

    r3 = lambda t: t.reshape(b, s, t.shape[-1])
    ya = _moba(rel_bias, r3(q), r3(k), r3(v), r3(za))
    tb = 256 if s % 256 == 0 else DN_CHUNK
    yd = _deltanet(a_log[0], dt_bias[0], r3(qkv), r3(zd), r3(gates), conv_w[0], dn_norm_w, tb)

    w_o = w_out[0].astype(BF16)
    out = _outproj(x2, ya.reshape(n, ATTN_WIDTH), yd.reshape(n, DN_WIDTH), w_o[:ATTN_WIDTH], w_o[ATTN_WIDTH:], tm)
    return out.reshape(b, s, d)
```

```python
import functools
import math

import jax
import jax.numpy as jnp
from jax import lax
from jax.experimental import pallas as pl
from jax.experimental.pallas import tpu as pltpu

F32 = jnp.float32
BF16 = jnp.bfloat16
HI = lax.Precision.HIGHEST

D_MODEL = 1024
ATTN_HEADS = 8
ATTN_HEAD_DIM = 64
ATTN_WIDTH = ATTN_HEADS * ATTN_HEAD_DIM
MOBA_BLOCK = 256
MOBA_TOPK = 3
REL_BUCKETS = 32
REL_MAX_DISTANCE = 128
DN_HEADS = 4
DN_HEAD = 128
DN_WIDTH = DN_HEADS * DN_HEAD
DN_CONV_WIDTH = 4
DN_CHUNK = 64
MAIN_COLS = 4 * ATTN_WIDTH + 3 * DN_WIDTH + DN_WIDTH
GATE_PAD = 128
EPS = 1e-6
LOG2E = math.log2(math.e)
NEG_BIG = -32768.0
VMEM_LIMIT = 56 * 1024 * 1024


def _nt(a, b, precision=None):
    return lax.dot_general(a, b, (((1,), (1,)), ((), ())), preferred_element_type=F32, precision=precision)


def _nn(a, b, precision=None):
    return lax.dot_general(a, b, (((1,), (0,)), ((), ())), preferred_element_type=F32, precision=precision)


def _silu(x):
    return x * (1.0 / (1.0 + jnp.exp(-x)))


def _inproj_kernel(x_ref, nw_ref, w_ref, wg_ref, qnw_ref, knw_ref,
                   q_ref, k_ref, v_ref, za_ref, qkv_ref, zd_ref, g_ref):
    x = x_ref[...]
    ms = jnp.mean(x * x, axis=-1, keepdims=True)
    h = (x * lax.rsqrt(ms + EPS) * nw_ref[...]).astype(BF16)

    r_i = lax.broadcasted_iota(jnp.int32, (ATTN_WIDTH, 128), 0)
    c_i = lax.broadcasted_iota(jnp.int32, (ATTN_WIDTH, 128), 1)
    ind = jnp.where(r_i // ATTN_HEAD_DIM == c_i, 1.0, 0.0).astype(BF16)
    r_e = lax.broadcasted_iota(jnp.int32, (128, ATTN_WIDTH), 0)
    c_e = lax.broadcasted_iota(jnp.int32, (128, ATTN_WIDTH), 1)
    expand = jnp.where(c_e // ATTN_HEAD_DIM == r_e, 1.0, 0.0).astype(BF16)

    def split3(a):
        a0 = a.astype(BF16)
        r1 = a - a0.astype(F32)
        a1 = r1.astype(BF16)
        a2 = (r1 - a1.astype(F32)).astype(BF16)
        return a0, a1, a2

    def head_rms(t, w):
        s0, s1, s2 = split3(t * t)
        ss = _nn(s0, ind) + _nn(s1, ind) + _nn(s2, ind)
        r = lax.rsqrt(ss * (1.0 / ATTN_HEAD_DIM) + EPS)
        r0, r1, r2 = split3(r)
        rf = _nn(r0, expand) + _nn(r1, expand) + _nn(r2, expand)
        return t * rf * w

    def proj(c0, width):
        return _nn(h, w_ref[:, c0:c0 + width])

    q_ref[...] = head_rms(proj(0, ATTN_WIDTH), qnw_ref[...]).astype(q_ref.dtype)
    k_ref[...] = head_rms(proj(ATTN_WIDTH, ATTN_WIDTH), knw_ref[...]).astype(k_ref.dtype)
    v_ref[...] = proj(2 * ATTN_WIDTH, ATTN_WIDTH).astype(v_ref.dtype)
    za_ref[...] = proj(3 * ATTN_WIDTH, ATTN_WIDTH).astype(za_ref.dtype)
    for c in range(3):
        qkv_ref[:, c * DN_WIDTH:(c + 1) * DN_WIDTH] = proj(4 * ATTN_WIDTH + c * DN_WIDTH, DN_WIDTH).astype(qkv_ref.dtype)
    zd_ref[...] = proj(4 * ATTN_WIDTH + 3 * DN_WIDTH, DN_WIDTH).astype(zd_ref.dtype)
    g_ref[...] = _nn(h, wg_ref[...])


def _inproj(x2, norm_w, w_main, w_gate, qnw, knw, tm):
    n = x2.shape[0]
    row = lambda i: (i, 0)
    fixed = lambda i: (0, 0)
    outs = [
        jax.ShapeDtypeStruct((n, ATTN_WIDTH), F32),
        jax.ShapeDtypeStruct((n, ATTN_WIDTH), F32),
        jax.ShapeDtypeStruct((n, ATTN_WIDTH), BF16),
        jax.ShapeDtypeStruct((n, ATTN_WIDTH), F32),
        jax.ShapeDtypeStruct((n, 3 * DN_WIDTH), F32),
        jax.ShapeDtypeStruct((n, DN_WIDTH), F32),
        jax.ShapeDtypeStruct((n, GATE_PAD), F32),
    ]
    return pl.pallas_call(
        _inproj_kernel,
        grid=(n // tm,),
        in_specs=[
            pl.BlockSpec((tm, D_MODEL), row),
            pl.BlockSpec((1, D_MODEL), fixed),
            pl.BlockSpec((D_MODEL, MAIN_COLS), fixed),
            pl.BlockSpec((D_MODEL, GATE_PAD), fixed),
            pl.BlockSpec((1, ATTN_WIDTH), fixed),
            pl.BlockSpec((1, ATTN_WIDTH), fixed),
        ],
        out_specs=[pl.BlockSpec((tm, o.shape[1]), row) for o in outs],
        out_shape=outs,
        compiler_params=pltpu.CompilerParams(dimension_semantics=("arbitrary",), vmem_limit_bytes=VMEM_LIMIT),
        name="inproj",
    )(x2, norm_w, w_main, w_gate, qnw, knw)


def _t5_bias_tile_t(relb_ref, head, offset):
    c = lax.broadcasted_iota(jnp.int32, (MOBA_BLOCK, MOBA_BLOCK), 0)
    r = lax.broadcasted_iota(jnp.int32, (MOBA_BLOCK, MOBA_BLOCK), 1)
    dist = r - c + offset
    n = jnp.maximum(dist, 0)
    max_exact = REL_BUCKETS // 2
    nf = jnp.maximum(n, 1).astype(F32)
    large = max_exact + (jnp.log(nf / max_exact) / math.log(REL_MAX_DISTANCE / max_exact)
                         * (REL_BUCKETS - max_exact)).astype(jnp.int32)
    large = jnp.minimum(large, REL_BUCKETS - 1)
    bucket = jnp.where(n < max_exact, n, large)
    far = relb_ref[REL_BUCKETS - 1, head]
    bias = jnp.zeros((MOBA_BLOCK, MOBA_BLOCK), F32)
    for t in range(REL_BUCKETS):
        bias = jnp.where(bucket == t, relb_ref[t, head] - far, bias)
    return jnp.where(dist >= 0, bias * LOG2E, NEG_BIG)


def _moba_kernel(relb_ref, q_ref, k_ref, v_ref, z_ref, o_ref,
                 kaug_ref, vt_ref, kmean_ref, bias_ref, acc_ref, qa_ref, sa_ref, sb_ref, *, nb):
    hp = pl.program_id(1)
    i = pl.program_id(2)
    HD = ATTN_HEAD_DIM
    lane = lax.broadcasted_iota(jnp.int32, (MOBA_BLOCK, 128), 1)
    lo_half = lane < HD
    er = lax.broadcasted_iota(jnp.int32, (128, 128), 0)
    ec = lax.broadcasted_iota(jnp.int32, (128, 128), 1)
    eye = jnp.where(er == ec, 1.0, 0.0).astype(BF16)

    @pl.when(i == 0)
    def _prepare():
        kmean_ref[...] = jnp.zeros_like(kmean_ref)
        lane1 = lax.broadcasted_iota(jnp.int32, (1, 128), 1)

        def prep(j, carry):
            rows = pl.ds(pl.multiple_of(j * MOBA_BLOCK, MOBA_BLOCK), MOBA_BLOCK)
            kb = k_ref[0, rows, :]
            km = jnp.mean(kb, axis=0, keepdims=True)
            kmean_ref[0, pl.ds(j, 1), :] = jnp.where(lane1 < HD, km, 0.0)
            kmean_ref[1, pl.ds(j, 1), :] = jnp.where(lane1 < HD, 0.0, km)
            kaug_ref[0, j] = jnp.where(lo_half, kb, jnp.where(lane == HD + j, 1.0, 0.0)).astype(BF16)
            kaug_ref[1, j] = jnp.where(lo_half, jnp.where(lane == j, 1.0, 0.0), kb).astype(BF16)
            vt_ref[j] = _nt(eye, v_ref[0, rows, :]).astype(BF16)
            return carry

        lax.fori_loop(0, nb, prep, 0)
        kaug_ref[0, nb] = jnp.where(lane == 2 * HD - 1, 1.0, 0.0).astype(BF16)
        kaug_ref[1, nb] = jnp.where(lane == HD - 1, 1.0, 0.0).astype(BF16)
        vt_ref[nb] = jnp.zeros((128, MOBA_BLOCK), BF16)
        for hh in range(2):
            bias_ref[hh, 0] = _t5_bias_tile_t(relb_ref, 2 * hp + hh, 0)
            bias_ref[hh, 1] = _t5_bias_tile_t(relb_ref, 2 * hp + hh, MOBA_BLOCK)

    q = q_ref[0]
    qt = _nt(eye, (q * (HD ** -0.5 * LOG2E)).astype(BF16))
    blk = lax.broadcasted_iota(jnp.int32, (HD, MOBA_BLOCK), 0)

    def select(hh):
        gate = _nt(kmean_ref[hh], q, precision=HI)
        g = jnp.where(blk < i, gate, -jnp.inf)
        sel = blk == i
        for _ in range(MOBA_TOPK):
            mx = jnp.max(g, axis=0, keepdims=True)
            first = jnp.min(jnp.where(g == mx, blk, HD), axis=0, keepdims=True)
            hit = (blk == first) & (mx > -jnp.inf)
            sel = sel | hit
            g = jnp.where(hit, -jnp.inf, g)
        return jnp.where(sel, 0.0, NEG_BIG)

    qa_ref[0] = jnp.concatenate([qt[0:HD], select(0)], axis=0).astype(BF16)
    qa_ref[1] = jnp.concatenate([select(1), qt[HD:2 * HD]], axis=0).astype(BF16)

    def issue(j, s_ref):
        for hh in range(2):
            s_ref[hh] = _nn(kaug_ref[hh, j], qa_ref[hh])

    def consume(j, s_ref, st, bias_idx=None, first=False):
        out = []
        for hh in range(2):
            m, l = st[2 * hh], st[2 * hh + 1]
            s = s_ref[hh]
            if bias_idx is not None:
                s = s + bias_ref[hh, bias_idx]
            m_cur = jnp.max(s, axis=0, keepdims=True)
            m_new = m_cur if first else jnp.maximum(m, m_cur)
            p = jnp.exp2(s - m_new)
            ps = jnp.sum(p, axis=0, keepdims=True)
            pv = _nn(vt_ref[j, pl.ds(hh * HD, HD), :], p.astype(BF16))
            rows = pl.ds(hh * HD, HD)
            if first:
                acc_ref[rows, :] = pv
                out += [m_new, ps]
            else:
                alpha = jnp.exp2(m - m_new)
                acc_ref[rows, :] = alpha * acc_ref[rows, :] + pv
                out += [m_new, alpha * l + ps]
        return tuple(out)

    nfar = jnp.maximum(i - 1, 0)
    pick = lambda j, ok: jnp.where(ok, j, nb)
    issue(i, sa_ref)
    issue(pick(i - 1, i >= 1), sb_ref)
    st = consume(i, sa_ref, (None,) * 4, bias_idx=0, first=True)
    issue(pick(0, nfar >= 1), sa_ref)
    st = consume(pick(i - 1, i >= 1), sb_ref, st, bias_idx=1)

    def far_pair(t, st):
        a = 2 * t
        issue(pick(a + 1, a + 1 < nfar), sb_ref)
        st = consume(a, sa_ref, st)
        issue(pick(a + 2, a + 2 < nfar), sa_ref)
        return consume(pick(a + 1, a + 1 < nfar), sb_ref, st)

    m0, l0, m1, l1 = lax.fori_loop(0, (nfar + 1) // 2, far_pair, st)

    half = lax.broadcasted_iota(jnp.int32, (2 * HD, MOBA_BLOCK), 0) < HD
    ot = acc_ref[...] * jnp.where(half, 1.0 / l0, 1.0 / l1)
    o_ref[0] = (ot.T * _silu(z_ref[0])).astype(o_ref.dtype)


def _moba(rel_bias, q, k, v, z):
    b, s, _ = q.shape
    nb = s // MOBA_BLOCK
    assert s % MOBA_BLOCK == 0 and nb < ATTN_HEAD_DIM
    pairs = ATTN_HEADS // 2
    blk = lambda bi, hp, i: (bi, i, hp)
    whole = lambda bi, hp, i: (bi, 0, hp)
    return pl.pallas_call(
        functools.partial(_moba_kernel, nb=nb),
        grid=(b, pairs, nb),
        in_specs=[
            pl.BlockSpec(memory_space=pltpu.SMEM),
            pl.BlockSpec((1, MOBA_BLOCK, 128), blk),
            pl.BlockSpec((1, s, 128), whole),
            pl.BlockSpec((1, s, 128), whole),
            pl.BlockSpec((1, MOBA_BLOCK, 128), blk),
        ],
        out_specs=pl.BlockSpec((1, MOBA_BLOCK, 128), blk),
        out_shape=jax.ShapeDtypeStruct((b, s, ATTN_WIDTH), BF16),
        scratch_shapes=[
            pltpu.VMEM((2, nb + 1, MOBA_BLOCK, 128), BF16),
            pltpu.VMEM((nb + 1, 128, MOBA_BLOCK), BF16),
            pltpu.VMEM((2, ATTN_HEAD_DIM, 128), F32),
            pltpu.VMEM((2, 2, MOBA_BLOCK, MOBA_BLOCK), F32),
            pltpu.VMEM((128, MOBA_BLOCK), F32),
            pltpu.VMEM((2, 128, MOBA_BLOCK), BF16),
            pltpu.VMEM((2, MOBA_BLOCK, MOBA_BLOCK), F32),
            pltpu.VMEM((2, MOBA_BLOCK, MOBA_BLOCK), F32),
        ],
        compiler_params=pltpu.CompilerParams(
            dimension_semantics=("arbitrary", "arbitrary", "arbitrary"), vmem_limit_bytes=VMEM_LIMIT),
        name="moba",
    )(rel_bias, q, k, v, z)


DN_ROWS = DN_HEADS * DN_CHUNK
DN_PREC = dict(kk="bf16", qk="bf16", inv="bf16", uw="bf16", state="bf16", intra="bf16")


def _split2(a):
    a0 = a.astype(BF16)
    return a0, (a - a0.astype(F32)).astype(BF16)


def _split3(a):
    a0 = a.astype(BF16)
    r1 = a - a0.astype(F32)
    a1 = r1.astype(BF16)
    return a0, a1, (r1 - a1.astype(F32)).astype(BF16)


def _dot(a, b, prec, dims=(((1,), (0,)), ((), ()))):
    f = lambda x, y: lax.dot_general(x, y, dims, preferred_element_type=F32)
    if prec == "bf16":
        return f(a.astype(BF16), b.astype(BF16))
    assert prec == "x3"
    a0, a1 = _split2(a)
    b0, b1 = _split2(b)
    return f(a0, b0) + (f(a0, b1) + f(a1, b0))


_NT = (((1,), (1,)), ((), ()))
_TN = (((0,), (0,)), ((), ()))


def _dot_sel(sel, b):
    return sum(_nn(sel, p) for p in _split3(b))


def _deltanet_kernel(alog_ref, dtb_ref, x_ref, z_ref, g_ref, cw_ref, nw_ref, o_ref,
                     state_ref, pad_ref, act_ref, rep_ref, *, tb):
    t = pl.program_id(1)
    C, R, W, H, D = DN_CHUNK, DN_ROWS, DN_WIDTH, DN_HEADS, DN_HEAD
    hist = DN_CONV_WIDTH - 1

    @pl.when(t == 0)
    def _reset():
        state_ref[...] = jnp.zeros_like(state_ref)
        pad_ref[0:8, :] = jnp.zeros((8, 3 * W), F32)

    pad_ref[8:8 + tb, :] = x_ref[0]
    for gcol in range(3 * H):
        cols = slice(gcol * D, (gcol + 1) * D)
        acc = pad_ref[pl.ds(8 - hist, tb), cols] * cw_ref[0:1, cols]
        for w in range(1, DN_CONV_WIDTH):
            acc = acc + pad_ref[pl.ds(8 - hist + w, tb), cols] * cw_ref[w:w + 1, cols]
        a = _silu(acc)
        if gcol < 2 * H:
            a = a * lax.rsqrt(jnp.sum(a * a, axis=-1, keepdims=True) + EPS)
            if gcol < H:
                a = a * (D ** -0.5)
        act_ref[:, cols] = a
    pad_ref[0:8, :] = x_ref[0, tb - 8:tb, :]

    lane1 = lax.broadcasted_iota(jnp.int32, (1, GATE_PAD), 1)
    alog_row = jnp.zeros((1, GATE_PAD), F32)
    dtb_row = jnp.zeros((1, GATE_PAD), F32)
    for h in range(H):
        alog_row = jnp.where(lane1 == H + h, alog_ref[h], alog_row)
        dtb_row = jnp.where(lane1 == H + h, dtb_ref[h], dtb_row)
    gates = g_ref[0]
    xs = gates + dtb_row
    softplus = jnp.maximum(xs, 0.0) + jnp.log(1.0 + jnp.exp(-jnp.abs(xs)))
    gval = jnp.where(lane1 < H, 1.0 / (1.0 + jnp.exp(-gates)), -jnp.exp(alog_row) * softplus)
    er = lax.broadcasted_iota(jnp.int32, (GATE_PAD, 2 * H * D), 0)
    ec = lax.broadcasted_iota(jnp.int32, (GATE_PAD, 2 * H * D), 1)
    spread = jnp.where(er == ec // D, 1.0, 0.0).astype(BF16)
    rep_ref[...] = sum(_nn(p, spread) for p in _split3(gval))

    r = lax.broadcasted_iota(jnp.int32, (R, R), 0)
    c = lax.broadcasted_iota(jnp.int32, (R, R), 1)
    same = (r // C) == (c // C)
    incl = same & (r >= c)
    strict = same & (r > c)
    eye = jnp.where(r == c, 1.0, 0.0)
    cum = jnp.where(incl, 1.0, 0.0).astype(BF16)
    lane_r = lax.broadcasted_iota(jnp.int32, (R, D), 1)
    P = DN_PREC

    for ci in range(tb // C):
        rows = slice(ci * C, (ci + 1) * C)
        stack = lambda ref, c0: jnp.concatenate([ref[rows, c0 + D * h:c0 + D * (h + 1)] for h in range(H)], axis=0)
        qc, kc, vc = stack(act_ref, 0), stack(act_ref, W), stack(act_ref, 2 * W)
        beta = stack(rep_ref, 0)
        gc = _dot_sel(cum, stack(rep_ref, H * D))
        p0, p1, p2 = (p.astype(F32) for p in _split3(gc))
        u_m = jnp.where(lane_r == 0, p0, jnp.where(lane_r == 1, p1, jnp.where(lane_r == 2, p2,
                        jnp.where(lane_r < 6, 1.0, 0.0))))
        w_m = jnp.where(lane_r < 3, 1.0, jnp.where(lane_r == 3, -p0, jnp.where(lane_r == 4, -p1,
                        jnp.where(lane_r == 5, -p2, 0.0))))
        diff = _nt(u_m.astype(BF16), w_m.astype(BF16))
        decay = jnp.where(incl, jnp.exp(jnp.where(incl, diff, 0.0)), 0.0)
        kb = kc * beta
        x = -jnp.where(strict, _dot(kb, kc, P["kk"], _NT) * decay, 0.0)
        x2 = _dot(x, x, P["inv"])
        x4 = _dot(x2, x2, P["inv"])
        x8 = _dot(x4, x4, P["inv"])
        x16 = _dot(x8, x8, P["inv"])
        x32 = _dot(x16, x16, P["inv"])
        f1, f4, f16 = eye + x, eye + x4, eye + x16
        m1 = f1 + _dot(f1, x2, P["inv"])
        m2 = f4 + _dot(f4, x8, P["inv"])
        m3 = f16 + _dot(f16, x32, P["inv"])
        tinv = _dot(_dot(m1, m2, P["inv"]), m3, P["inv"])
        eg = jnp.exp(gc)
        uw = _dot(tinv, jnp.concatenate([vc * beta, kb * eg], axis=1), P["uw"])
        a_intra = jnp.where(incl, _dot(qc, kc, P["qk"], _NT) * decay, 0.0)
        glast = [gc[h * C + C - 1:h * C + C, :] for h in range(H)]
        q_dec = qc * eg
        k_dec = kc * jnp.exp(jnp.concatenate([jnp.broadcast_to(gl, (C, D)) for gl in glast], axis=0) - gc)
        v_new, o_inter = [], []
        for h in range(H):
            hs = slice(h * C, (h + 1) * C)
            state = state_ref[h]
            wq = _dot(jnp.concatenate([uw[hs, D:2 * D], q_dec[hs]], axis=0), state, P["state"])
            vn = uw[hs, 0:D] - wq[0:C]
            state_ref[h] = state * jnp.exp(glast[h]) + _dot(k_dec[hs], vn, P["state"], _TN)
            v_new.append(vn)
            o_inter.append(wq[C:2 * C])
        o = jnp.concatenate(o_inter, axis=0) + _dot(a_intra, jnp.concatenate(v_new, axis=0), P["intra"])
        on = o * lax.rsqrt(jnp.mean(o * o, axis=-1, keepdims=True) + EPS) * nw_ref[...]
        for h in range(H):
            cols = slice(h * D, (h + 1) * D)
            o_ref[0, rows, cols] = (on[h * C:(h + 1) * C] * _silu(z_ref[0, rows, cols])).astype(o_ref.dtype)


def _deltanet(a_log, dt_bias, qkv, z, gates, conv_w, dn_norm_w, tb):
    b, s, _ = qkv.shape
    assert s % tb == 0 and tb % DN_CHUNK == 0
    blk = lambda bi, t: (bi, t, 0)
    fixed = lambda bi, t: (0, 0)
    return pl.pallas_call(
        functools.partial(_deltanet_kernel, tb=tb),
        grid=(b, s // tb),
        in_specs=[
            pl.BlockSpec(memory_space=pltpu.SMEM),
            pl.BlockSpec(memory_space=pltpu.SMEM),
            pl.BlockSpec((1, tb, 3 * DN_WIDTH), blk),
            pl.BlockSpec((1, tb, DN_WIDTH), blk),
            pl.BlockSpec((1, tb, GATE_PAD), blk),
            pl.BlockSpec((DN_CONV_WIDTH, 3 * DN_WIDTH), fixed),
            pl.BlockSpec((1, DN_HEAD), fixed),
        ],
        out_specs=pl.BlockSpec((1, tb, DN_WIDTH), blk),
        out_shape=jax.ShapeDtypeStruct((b, s, DN_WIDTH), BF16),
        scratch_shapes=[
            pltpu.VMEM((DN_HEADS, DN_HEAD, DN_HEAD), F32),
            pltpu.VMEM((tb + 8, 3 * DN_WIDTH), F32),
            pltpu.VMEM((tb, 3 * DN_WIDTH), F32),
            pltpu.VMEM((tb, 2 * DN_HEADS * DN_HEAD), F32),
        ],
        compiler_params=pltpu.CompilerParams(
            dimension_semantics=("arbitrary", "arbitrary"), vmem_limit_bytes=VMEM_LIMIT),
        name="deltanet",
    )(a_log, dt_bias, qkv, z, gates, conv_w, dn_norm_w)


def _outproj_kernel(x_ref, ya_ref, yd_ref, wa_ref, wd_ref, o_ref):
    o_ref[...] = x_ref[...] + _nn(ya_ref[...], wa_ref[...]) + _nn(yd_ref[...], wd_ref[...])


def _outproj(x2, ya, yd, wa, wd, tm):
    n = x2.shape[0]
    row = lambda i: (i, 0)
    fixed = lambda i: (0, 0)
    return pl.pallas_call(
        _outproj_kernel,
        grid=(n // tm,),
        in_specs=[
            pl.BlockSpec((tm, D_MODEL), row),
            pl.BlockSpec((tm, ATTN_WIDTH), row),
            pl.BlockSpec((tm, DN_WIDTH), row),
            pl.BlockSpec((ATTN_WIDTH, D_MODEL), fixed),
            pl.BlockSpec((DN_WIDTH, D_MODEL), fixed),
        ],
        out_specs=pl.BlockSpec((tm, D_MODEL), row),
        out_shape=jax.ShapeDtypeStruct((n, D_MODEL), F32),
        compiler_params=pltpu.CompilerParams(dimension_semantics=("arbitrary",), vmem_limit_bytes=VMEM_LIMIT),
        name="outproj",
    )(x2, ya, yd, wa, wd)


def _row_tile(n):
    for tm in (512, 256, 128, 64, 32, 16, 8):
        if n % tm == 0:
            return tm
    raise ValueError(f"row count {n} is not a multiple of 8")


def kernel(x, rel_bias, norm_w, w_in, q_norm_w, k_norm_w, conv_w, a_log, dt_bias, dn_norm_w, w_out):
    b, s, d = x.shape
    assert d == D_MODEL and norm_w.shape[0] == 1, "single-layer kernel"
    n = b * s
    x2 = x.reshape(n, d)
    tm = _row_tile(n)

    w_main = w_in[0][:, :MAIN_COLS].astype(BF16)
    w_gate = jnp.pad(w_in[0][:, MAIN_COLS:], ((0, 0), (0, GATE_PAD - 2 * DN_HEADS))).astype(BF16)
    qnw = jnp.tile(q_norm_w[0], ATTN_HEADS)[None, :]
    knw = jnp.tile(k_norm_w[0], ATTN_HEADS)[None, :]

    q, k, v, za, qkv, zd, gates = _inproj(x2, norm_w, w_main, w_gate, qnw, knw, tm)

    r3 = lambda t: t.reshape(b, s, t.shape[-1])
    ya = _moba(rel_bias, r3(q), r3(k), r3(v), r3(za))
    tb = 256 if s % 256 == 0 else DN_CHUNK
    yd = _deltanet(a_log[0], dt_bias[0], r3(qkv), r3(zd), r3(gates), conv_w[0], dn_norm_w, tb)

    w_o = w_out[0].astype(BF16)
    out = _outproj(x2, ya.reshape(n, ATTN_WIDTH), yd.reshape(n, DN_WIDTH), w_o[:ATTN_WIDTH], w_o[ATTN_WIDTH:], tm)
    return out.reshape(b, s, d)
```

```python
import functools
import math

import jax
import jax.numpy as jnp
from jax import lax
from jax.experimental import pallas as pl
from jax.experimental.pallas import tpu as pltpu

F32 = jnp.float32
BF16 = jnp.bfloat16
HI = lax.Precision.HIGHEST

D_MODEL = 1024
ATTN_HEADS = 8
ATTN_HEAD_DIM = 64
ATTN_WIDTH = ATTN_HEADS * ATTN_HEAD_DIM
MOBA_BLOCK = 256
MOBA_TOPK = 3
REL_BUCKETS = 32
REL_MAX_DISTANCE = 128
DN_HEADS = 4
DN_HEAD = 128
DN_WIDTH = DN_HEADS * DN_HEAD
DN_CONV_WIDTH = 4
DN_CHUNK = 64
MAIN_COLS = 4 * ATTN_WIDTH + 3 * DN_WIDTH + DN_WIDTH
GATE_PAD = 128
EPS = 1e-6
LOG2E = math.log2(math.e)
NEG_BIG = -32768.0
VMEM_LIMIT = 56 * 1024 * 1024


def _nt(a, b, precision=None):
    return lax.dot_general(a, b, (((1,), (1,)), ((), ())), preferred_element_type=F32, precision=precision)


def _nn(a, b, precision=None):
    return lax.dot_general(a, b, (((1,), (0,)), ((), ())), preferred_element_type=F32, precision=precision)


def _silu(x):
    half = 0.5 * x
    return half + half * jnp.tanh(half)


def _inproj_kernel(x_ref, nw_ref, w_ref, wg_ref, qnw_ref, knw_ref,
                   q_ref, k_ref, v_ref, za_ref, qkv_ref, zd_ref, g_ref):
    x = x_ref[...]
    ms = jnp.mean(x * x, axis=-1, keepdims=True)
    h = (x * lax.rsqrt(ms + EPS) * nw_ref[...]).astype(BF16)

    r_i = lax.broadcasted_iota(jnp.int32, (ATTN_WIDTH, 128), 0)
    c_i = lax.broadcasted_iota(jnp.int32, (ATTN_WIDTH, 128), 1)
    ind = jnp.where(r_i // ATTN_HEAD_DIM == c_i, 1.0, 0.0).astype(BF16)
    r_e = lax.broadcasted_iota(jnp.int32, (128, ATTN_WIDTH), 0)
    c_e = lax.broadcasted_iota(jnp.int32, (128, ATTN_WIDTH), 1)
    expand = jnp.where(c_e // ATTN_HEAD_DIM == r_e, 1.0, 0.0).astype(BF16)

    def split3(a):
        a0 = a.astype(BF16)
        r1 = a - a0.astype(F32)
        a1 = r1.astype(BF16)
        a2 = (r1 - a1.astype(F32)).astype(BF16)
        return a0, a1, a2

    def head_rms(t, w):
        s0, s1, s2 = split3(t * t)
        ss = _nn(s0, ind) + _nn(s1, ind) + _nn(s2, ind)
        r = lax.rsqrt(ss * (1.0 / ATTN_HEAD_DIM) + EPS)
        r0, r1, r2 = split3(r)
        rf = _nn(r0, expand) + _nn(r1, expand) + _nn(r2, expand)
        return t * rf * w

    def proj(c0, width):
        return _nn(h, w_ref[:, c0:c0 + width])

    q_ref[...] = head_rms(proj(0, ATTN_WIDTH), qnw_ref[...]).astype(q_ref.dtype)
    k_ref[...] = head_rms(proj(ATTN_WIDTH, ATTN_WIDTH), knw_ref[...]).astype(k_ref.dtype)
    v_ref[...] = proj(2 * ATTN_WIDTH, ATTN_WIDTH).astype(v_ref.dtype)
    za_ref[...] = proj(3 * ATTN_WIDTH, ATTN_WIDTH).astype(za_ref.dtype)
    for c in range(3):
        qkv_ref[:, c * DN_WIDTH:(c + 1) * DN_WIDTH] = proj(4 * ATTN_WIDTH + c * DN_WIDTH, DN_WIDTH).astype(qkv_ref.dtype)
    zd_ref[...] = proj(4 * ATTN_WIDTH + 3 * DN_WIDTH, DN_WIDTH).astype(zd_ref.dtype)
    g_ref[...] = _nn(h, wg_ref[...])


def _inproj(x2, norm_w, w_main, w_gate, qnw, knw, tm):
    n = x2.shape[0]
    row = lambda i: (i, 0)
    fixed = lambda i: (0, 0)
    outs = [
        jax.ShapeDtypeStruct((n, ATTN_WIDTH), F32),
        jax.ShapeDtypeStruct((n, ATTN_WIDTH), F32),
        jax.ShapeDtypeStruct((n, ATTN_WIDTH), BF16),
        jax.ShapeDtypeStruct((n, ATTN_WIDTH), F32),
        jax.ShapeDtypeStruct((n, 3 * DN_WIDTH), F32),
        jax.ShapeDtypeStruct((n, DN_WIDTH), F32),
        jax.ShapeDtypeStruct((n, GATE_PAD), F32),
    ]
    return pl.pallas_call(
        _inproj_kernel,
        grid=(n // tm,),
        in_specs=[
            pl.BlockSpec((tm, D_MODEL), row),
            pl.BlockSpec((1, D_MODEL), fixed),
            pl.BlockSpec((D_MODEL, MAIN_COLS), fixed),
            pl.BlockSpec((D_MODEL, GATE_PAD), fixed),
            pl.BlockSpec((1, ATTN_WIDTH), fixed),
            pl.BlockSpec((1, ATTN_WIDTH), fixed),
        ],
        out_specs=[pl.BlockSpec((tm, o.shape[1]), row) for o in outs],
        out_shape=outs,
        compiler_params=pltpu.CompilerParams(dimension_semantics=("arbitrary",), vmem_limit_bytes=VMEM_LIMIT),
        name="inproj",
    )(x2, norm_w, w_main, w_gate, qnw, knw)


def _t5_bias_tile_t(relb_ref, head, offset):
    c = lax.broadcasted_iota(jnp.int32, (MOBA_BLOCK, MOBA_BLOCK), 0)
    r = lax.broadcasted_iota(jnp.int32, (MOBA_BLOCK, MOBA_BLOCK), 1)
    dist = r - c + offset
    n = jnp.maximum(dist, 0)
    max_exact = REL_BUCKETS // 2
    nf = jnp.maximum(n, 1).astype(F32)
    large = max_exact + (jnp.log(nf / max_exact) / math.log(REL_MAX_DISTANCE / max_exact)
                         * (REL_BUCKETS - max_exact)).astype(jnp.int32)
    large = jnp.minimum(large, REL_BUCKETS - 1)
    bucket = jnp.where(n < max_exact, n, large)
    far = relb_ref[REL_BUCKETS - 1, head]
    bias = jnp.zeros((MOBA_BLOCK, MOBA_BLOCK), F32)
    for t in range(REL_BUCKETS):
        bias = jnp.where(bucket == t, relb_ref[t, head] - far, bias)
    return jnp.where(dist >= 0, bias * LOG2E, NEG_BIG)


def _moba_kernel(relb_ref, q_ref, k_ref, v_ref, z_ref, o_ref,
                 kaug_ref, vt_ref, kmean_ref, bias_ref, acc_ref, qa_ref, sa_ref, sb_ref, *, nb):
    hp = pl.program_id(1)
    i = pl.program_id(2)
    HD = ATTN_HEAD_DIM
    lane = lax.broadcasted_iota(jnp.int32, (MOBA_BLOCK, 128), 1)
    lo_half = lane < HD
    er = lax.broadcasted_iota(jnp.int32, (128, 128), 0)
    ec = lax.broadcasted_iota(jnp.int32, (128, 128), 1)
    eye = jnp.where(er == ec, 1.0, 0.0).astype(BF16)

    @pl.when(i == 0)
    def _prepare():
        kmean_ref[...] = jnp.zeros_like(kmean_ref)
        lane1 = lax.broadcasted_iota(jnp.int32, (1, 128), 1)

        def prep(j, carry):
            rows = pl.ds(pl.multiple_of(j * MOBA_BLOCK, MOBA_BLOCK), MOBA_BLOCK)
            kb = k_ref[0, rows, :]
            km = jnp.mean(kb, axis=0, keepdims=True)
            kmean_ref[0, pl.ds(j, 1), :] = jnp.where(lane1 < HD, km, 0.0)
            kmean_ref[1, pl.ds(j, 1), :] = jnp.where(lane1 < HD, 0.0, km)
            kaug_ref[0, j] = jnp.where(lo_half, kb, jnp.where(lane == HD + j, 1.0, 0.0)).astype(BF16)
            kaug_ref[1, j] = jnp.where(lo_half, jnp.where(lane == j, 1.0, 0.0), kb).astype(BF16)
            vt_ref[j] = _nt(eye, v_ref[0, rows, :]).astype(BF16)
            return carry

        lax.fori_loop(0, nb, prep, 0)
        kaug_ref[0, nb] = jnp.where(lane == 2 * HD - 1, 1.0, 0.0).astype(BF16)
        kaug_ref[1, nb] = jnp.where(lane == HD - 1, 1.0, 0.0).astype(BF16)
        vt_ref[nb] = jnp.zeros((128, MOBA_BLOCK), BF16)
        for hh in range(2):
            bias_ref[hh, 0] = _t5_bias_tile_t(relb_ref, 2 * hp + hh, 0)
            bias_ref[hh, 1] = _t5_bias_tile_t(relb_ref, 2 * hp + hh, MOBA_BLOCK)

    q = q_ref[0]
    qt = _nt(eye, (q * (HD ** -0.5 * LOG2E)).astype(BF16))
    blk = lax.broadcasted_iota(jnp.int32, (HD, MOBA_BLOCK), 0)

    def select(hh):
        gate = _nt(kmean_ref[hh], q, precision=HI)
        g = jnp.where(blk < i, gate, -jnp.inf)
        sel = blk == i
        for _ in range(MOBA_TOPK):
            mx = jnp.max(g, axis=0, keepdims=True)
            first = jnp.min(jnp.where(g == mx, blk, HD), axis=0, keepdims=True)
            hit = (blk == first) & (mx > -jnp.inf)
            sel = sel | hit
            g = jnp.where(hit, -jnp.inf, g)
        return jnp.where(sel, 0.0, NEG_BIG)

    qa_ref[0] = jnp.concatenate([qt[0:HD], select(0)], axis=0).astype(BF16)
    qa_ref[1] = jnp.concatenate([select(1), qt[HD:2 * HD]], axis=0).astype(BF16)

    def issue(j, s_ref):
        for hh in range(2):
            s_ref[hh] = _nn(kaug_ref[hh, j], qa_ref[hh])

    def consume(j, s_ref, st, bias_idx=None, first=False):
        out = []
        for hh in range(2):
            m, l = st[2 * hh], st[2 * hh + 1]
            s = s_ref[hh]
            if bias_idx is not None:
                s = s + bias_ref[hh, bias_idx]
            m_cur = jnp.max(s, axis=0, keepdims=True)
            m_new = m_cur if first else jnp.maximum(m, m_cur)
            p = jnp.exp2(s - m_new)
            ps = jnp.sum(p, axis=0, keepdims=True)
            pv = _nn(vt_ref[j, pl.ds(hh * HD, HD), :], p.astype(BF16))
            rows = pl.ds(hh * HD, HD)
            if first:
                acc_ref[rows, :] = pv
                out += [m_new, ps]
            else:
                alpha = jnp.exp2(m - m_new)
                acc_ref[rows, :] = alpha * acc_ref[rows, :] + pv
                out += [m_new, alpha * l + ps]
        return tuple(out)

    nfar = jnp.maximum(i - 1, 0)
    pick = lambda j, ok: jnp.where(ok, j, nb)
    issue(i, sa_ref)
    issue(pick(i - 1, i >= 1), sb_ref)
    st = consume(i, sa_ref, (None,) * 4, bias_idx=0, first=True)
    issue(pick(0, nfar >= 1), sa_ref)
    st = consume(pick(i - 1, i >= 1), sb_ref, st, bias_idx=1)

    def far_pair(t, st):
        a = 2 * t
        issue(pick(a + 1, a + 1 < nfar), sb_ref)
        st = consume(a, sa_ref, st)
        issue(pick(a + 2, a + 2 < nfar), sa_ref)
        return consume(pick(a + 1, a + 1 < nfar), sb_ref, st)

    m0, l0, m1, l1 = lax.fori_loop(0, (nfar + 1) // 2, far_pair, st)

    half = lax.broadcasted_iota(jnp.int32, (2 * HD, MOBA_BLOCK), 0) < HD
    ot = acc_ref[...] * jnp.where(half, 1.0 / l0, 1.0 / l1)
    o_ref[0] = (ot.T * _silu(z_ref[0])).astype(o_ref.dtype)


def _moba(rel_bias, q, k, v, z):
    b, s, _ = q.shape
    nb = s // MOBA_BLOCK
    assert s % MOBA_BLOCK == 0 and nb < ATTN_HEAD_DIM
    pairs = ATTN_HEADS // 2
    blk = lambda bi, hp, i: (bi, i, hp)
    whole = lambda bi, hp, i: (bi, 0, hp)
    return pl.pallas_call(
        functools.partial(_moba_kernel, nb=nb),
        grid=(b, pairs, nb),
        in_specs=[
            pl.BlockSpec(memory_space=pltpu.SMEM),
            pl.BlockSpec((1, MOBA_BLOCK, 128), blk),
            pl.BlockSpec((1, s, 128), whole),
            pl.BlockSpec((1, s, 128), whole),
            pl.BlockSpec((1, MOBA_BLOCK, 128), blk),
        ],
        out_specs=pl.BlockSpec((1, MOBA_BLOCK, 128), blk),
        out_shape=jax.ShapeDtypeStruct((b, s, ATTN_WIDTH), BF16),
        scratch_shapes=[
            pltpu.VMEM((2, nb + 1, MOBA_BLOCK, 128), BF16),
            pltpu.VMEM((nb + 1, 128, MOBA_BLOCK), BF16),
            pltpu.VMEM((2, ATTN_HEAD_DIM, 128), F32),
            pltpu.VMEM((2, 2, MOBA_BLOCK, MOBA_BLOCK), F32),
            pltpu.VMEM((128, MOBA_BLOCK), F32),
            pltpu.VMEM((2, 128, MOBA_BLOCK), BF16),
            pltpu.VMEM((2, MOBA_BLOCK, MOBA_BLOCK), F32),
            pltpu.VMEM((2, MOBA_BLOCK, MOBA_BLOCK), F32),
        ],
        compiler_params=pltpu.CompilerParams(
            dimension_semantics=("arbitrary", "arbitrary", "arbitrary"), vmem_limit_bytes=VMEM_LIMIT),
        name="moba",
    )(rel_bias, q, k, v, z)


DN_ROWS = DN_HEADS * DN_CHUNK
DN_PREC = dict(kk="bf16", qk="bf16", inv="bf16", uw="bf16", state="bf16", intra="bf16")


def _split2(a):
    a0 = a.astype(BF16)
    return a0, (a - a0.astype(F32)).astype(BF16)


def _split3(a):
    a0 = a.astype(BF16)
    r1 = a - a0.astype(F32)
    a1 = r1.astype(BF16)
    return a0, a1, (r1 - a1.astype(F32)).astype(BF16)


def _dot(a, b, prec, dims=(((1,), (0,)), ((), ()))):
    f = lambda x, y: lax.dot_general(x, y, dims, preferred_element_type=F32)
    if prec == "bf16":
        return f(a.astype(BF16), b.astype(BF16))
    assert prec == "x3"
    a0, a1 = _split2(a)
    b0, b1 = _split2(b)
    return f(a0, b0) + (f(a0, b1) + f(a1, b0))


_NT = (((1,), (1,)), ((), ()))
_TN = (((0,), (0,)), ((), ()))


def _dot_sel(sel, b):
    return sum(_nn(sel, p) for p in _split3(b))


def _deltanet_kernel(alog_ref, dtb_ref, x_ref, z_ref, g_ref, cw_ref, nw_ref, o_ref,
                     state_ref, pad_ref, act_ref, rep_ref, *, tb):
    t = pl.program_id(1)
    C, R, W, H, D = DN_CHUNK, DN_ROWS, DN_WIDTH, DN_HEADS, DN_HEAD
    hist = DN_CONV_WIDTH - 1

    @pl.when(t == 0)
    def _reset():
        state_ref[...] = jnp.zeros_like(state_ref)
        pad_ref[0:8, :] = jnp.zeros((8, 3 * W), F32)

    pad_ref[8:8 + tb, :] = x_ref[0]
    for gcol in range(3 * H):
        cols = slice(gcol * D, (gcol + 1) * D)
        acc = pad_ref[pl.ds(8 - hist, tb), cols] * cw_ref[0:1, cols]
        for w in range(1, DN_CONV_WIDTH):
            acc = acc + pad_ref[pl.ds(8 - hist + w, tb), cols] * cw_ref[w:w + 1, cols]
        a = _silu(acc)
        if gcol < 2 * H:
            a = a * lax.rsqrt(jnp.sum(a * a, axis=-1, keepdims=True) + EPS)
            if gcol < H:
                a = a * (D ** -0.5)
        act_ref[:, cols] = a
    pad_ref[0:8, :] = x_ref[0, tb - 8:tb, :]

    lane1 = lax.broadcasted_iota(jnp.int32, (1, GATE_PAD), 1)
    alog_row = jnp.zeros((1, GATE_PAD), F32)
    dtb_row = jnp.zeros((1, GATE_PAD), F32)
    for h in range(H):
        alog_row = jnp.where(lane1 == H + h, alog_ref[h], alog_row)
        dtb_row = jnp.where(lane1 == H + h, dtb_ref[h], dtb_row)
    gates = g_ref[0]
    xs = gates + dtb_row
    softplus = jnp.maximum(xs, 0.0) + jnp.log(1.0 + jnp.exp(-jnp.abs(xs)))
    gval = jnp.where(lane1 < H, 1.0 / (1.0 + jnp.exp(-gates)), -jnp.exp(alog_row) * softplus)
    er = lax.broadcasted_iota(jnp.int32, (GATE_PAD, 2 * H * D), 0)
    ec = lax.broadcasted_iota(jnp.int32, (GATE_PAD, 2 * H * D), 1)
    spread = jnp.where(er == ec // D, 1.0, 0.0).astype(BF16)
    rep_ref[...] = sum(_nn(p, spread) for p in _split3(gval))

    r = lax.broadcasted_iota(jnp.int32, (R, R), 0)
    c = lax.broadcasted_iota(jnp.int32, (R, R), 1)
    same = (r // C) == (c // C)
    incl = same & (r >= c)
    strict = same & (r > c)
    eye = jnp.where(r == c, 1.0, 0.0)
    cum = jnp.where(incl, 1.0, 0.0).astype(BF16)
    lane_r = lax.broadcasted_iota(jnp.int32, (R, D), 1)
    P = DN_PREC

    chunks = range(tb // C)
    each = lambda f, *seqs: [f(*a) for a in zip(*seqs)]
    rows = [slice(ci * C, (ci + 1) * C) for ci in chunks]
    stack = lambda ref, c0: [jnp.concatenate([ref[rw, c0 + D * h:c0 + D * (h + 1)] for h in range(H)], axis=0)
                             for rw in rows]
    qc, kc, vc = stack(act_ref, 0), stack(act_ref, W), stack(act_ref, 2 * W)
    beta = stack(rep_ref, 0)
    gc = each(lambda g: _dot_sel(cum, g), stack(rep_ref, H * D))

    def pair_diff(g):
        p0, p1, p2 = (p.astype(F32) for p in _split3(g))
        u_m = jnp.where(lane_r == 0, p0, jnp.where(lane_r == 1, p1, jnp.where(lane_r == 2, p2,
                        jnp.where(lane_r < 6, 1.0, 0.0))))
        w_m = jnp.where(lane_r < 3, 1.0, jnp.where(lane_r == 3, -p0, jnp.where(lane_r == 4, -p1,
                        jnp.where(lane_r == 5, -p2, 0.0))))
        return _nt(u_m.astype(BF16), w_m.astype(BF16))

    decay = each(lambda g: jnp.where(incl, jnp.exp(jnp.where(incl, pair_diff(g), 0.0)), 0.0), gc)
    kb = each(lambda k, b: k * b, kc, beta)
    x = each(lambda a, k, d: -jnp.where(strict, _dot(a, k, P["kk"], _NT) * d, 0.0), kb, kc, decay)
    sq = lambda m: each(lambda a: _dot(a, a, P["inv"]), m)
    mul = lambda ma, mb: each(lambda a, b: _dot(a, b, P["inv"]), ma, mb)
    add = lambda ma, mb: each(lambda a, b: a + b, ma, mb)
    plus_eye = lambda m: each(lambda a: eye + a, m)
    x2 = sq(x)
    x4 = sq(x2)
    f1 = plus_eye(x)
    m1 = add(f1, mul(f1, x2))
    x8 = sq(x4)
    f4 = plus_eye(x4)
    m2 = add(f4, mul(f4, x8))
    x16 = sq(x8)
    m12 = mul(m1, m2)
    x32 = sq(x16)
    f16 = plus_eye(x16)
    m3 = add(f16, mul(f16, x32))
    tinv = mul(m12, m3)
    eg = each(jnp.exp, gc)
    uw = each(lambda t, v, b, k, e: _dot(t, jnp.concatenate([v * b, k * e], axis=1), P["uw"]),
              tinv, vc, beta, kb, eg)
    a_intra = each(lambda q, k, d: jnp.where(incl, _dot(q, k, P["qk"], _NT) * d, 0.0), qc, kc, decay)
    glast = [[g[h * C + C - 1:h * C + C, :] for h in range(H)] for g in gc]
    q_dec = each(lambda q, e: q * e, qc, eg)
    k_dec = each(lambda k, g, gl: k * jnp.exp(
        jnp.concatenate([jnp.broadcast_to(t, (C, D)) for t in gl], axis=0) - g), kc, gc, glast)

    hs = [slice(h * C, (h + 1) * C) for h in range(H)]
    for ci in chunks:
        state = [state_ref[h] for h in range(H)]
        wq = [_dot(jnp.concatenate([uw[ci][hs[h], D:2 * D], q_dec[ci][hs[h]]], axis=0), state[h], P["state"])
              for h in range(H)]
        vn = [uw[ci][hs[h], 0:D] - wq[h][0:C] for h in range(H)]
        for h in range(H):
            state_ref[h] = state[h] * jnp.exp(glast[ci][h]) + _dot(k_dec[ci][hs[h]], vn[h], P["state"], _TN)
        o = jnp.concatenate([wq[h][C:2 * C] for h in range(H)], axis=0) + _dot(
            a_intra[ci], jnp.concatenate(vn, axis=0), P["intra"])
        on = o * lax.rsqrt(jnp.mean(o * o, axis=-1, keepdims=True) + EPS) * nw_ref[...]
        for h in range(H):
            cols = slice(h * D, (h + 1) * D)
            o_ref[0, rows[ci], cols] = (on[hs[h]] * _silu(z_ref[0, rows[ci], cols])).astype(o_ref.dtype)


def _deltanet(a_log, dt_bias, qkv, z, gates, conv_w, dn_norm_w, tb):
    b, s, _ = qkv.shape
    assert s % tb == 0 and tb % DN_CHUNK == 0
    blk = lambda bi, t: (bi, t, 0)
    fixed = lambda bi, t: (0, 0)
    return pl.pallas_call(
        functools.partial(_deltanet_kernel, tb=tb),
        grid=(b, s // tb),
        in_specs=[
            pl.BlockSpec(memory_space=pltpu.SMEM),
            pl.BlockSpec(memory_space=pltpu.SMEM),
            pl.BlockSpec((1, tb, 3 * DN_WIDTH), blk),
            pl.BlockSpec((1, tb, DN_WIDTH), blk),
            pl.BlockSpec((1, tb, GATE_PAD), blk),
            pl.BlockSpec((DN_CONV_WIDTH, 3 * DN_WIDTH), fixed),
            pl.BlockSpec((1, DN_HEAD), fixed),
        ],
        out_specs=pl.BlockSpec((1, tb, DN_WIDTH), blk),
        out_shape=jax.ShapeDtypeStruct((b, s, DN_WIDTH), BF16),
        scratch_shapes=[
            pltpu.VMEM((DN_HEADS, DN_HEAD, DN_HEAD), F32),
            pltpu.VMEM((tb + 8, 3 * DN_WIDTH), F32),
            pltpu.VMEM((tb, 3 * DN_WIDTH), F32),
            pltpu.VMEM((tb, 2 * DN_HEADS * DN_HEAD), F32),
        ],
        compiler_params=pltpu.CompilerParams(
            dimension_semantics=("arbitrary", "arbitrary"), vmem_limit_bytes=VMEM_LIMIT),
        name="deltanet",
    )(a_log, dt_bias, qkv, z, gates, conv_w, dn_norm_w)


def _outproj_kernel(x_ref, ya_ref, yd_ref, wa_ref, wd_ref, o_ref):
    o_ref[...] = x_ref[...] + _nn(ya_ref[...], wa_ref[...]) + _nn(yd_ref[...], wd_ref[...])


def _outproj(x2, ya, yd, wa, wd, tm):
    n = x2.shape[0]
    row = lambda i: (i, 0)
    fixed = lambda i: (0, 0)
    return pl.pallas_call(
        _outproj_kernel,
        grid=(n // tm,),
        in_specs=[
            pl.BlockSpec((tm, D_MODEL), row),
            pl.BlockSpec((tm, ATTN_WIDTH), row),
            pl.BlockSpec((tm, DN_WIDTH), row),
            pl.BlockSpec((ATTN_WIDTH, D_MODEL), fixed),
            pl.BlockSpec((DN_WIDTH, D_MODEL), fixed),
        ],
        out_specs=pl.BlockSpec((tm, D_MODEL), row),
        out_shape=jax.ShapeDtypeStruct((n, D_MODEL), F32),
        compiler_params=pltpu.CompilerParams(dimension_semantics=("arbitrary",), vmem_limit_bytes=VMEM_LIMIT),
        name="outproj",
    )(x2, ya, yd, wa, wd)


def _row_tile(n):
    for tm in (512, 256, 128, 64, 32, 16, 8):
        if n % tm == 0:
            return tm
    raise ValueError(f"row count {n} is not a multiple of 8")


def kernel(x, rel_bias, norm_w, w_in, q_norm_w, k_norm_w, conv_w, a_log, dt_bias, dn_norm_w, w_out):
    b, s, d = x.shape
    assert d == D_MODEL and norm_w.shape[0] == 1, "single-layer kernel"
    n = b * s
    x2 = x.reshape(n, d)
    tm = _row_tile(n)

    w_main = w_in[0][:, :MAIN_COLS].astype(BF16)
    w_gate = jnp.pad(w_in[0][:, MAIN_COLS:], ((0, 0), (0, GATE_PAD - 2 * DN_HEADS))).astype(BF16)
    qnw = jnp.tile(q_norm_w[0], ATTN_HEADS)[None, :]
    knw = jnp.tile(k_norm_w[0], ATTN_HEADS)[None, :]

    q, k, v, za, qkv, zd, gates = _inproj(x2, norm_w, w_main, w_gate, qnw, knw, tm)

    r3 = lambda t: t.reshape(b, s, t.shape[-1])
    ya = _moba(rel_bias, r3(q), r3(k), r3(v), r3(za))
    tb = 256 if s % 256 == 0 else DN_CHUNK
    yd = _deltanet(a_log[0], dt_bias[0], r3(qkv), r3(zd), r3(gates), conv_w[0], dn_norm_w, tb)

    w_o = w_out[0].astype(BF16)
    out = _outproj(x2, ya.reshape(n, ATTN_WIDTH), yd.reshape(n, DN_WIDTH), w_o[:ATTN_WIDTH], w_o[ATTN_WIDTH:], tm)
    return out.reshape(b, s, d)
```

```python
import functools
import math

import jax
import jax.numpy as jnp
from jax import lax
from jax.experimental import pallas as pl
from jax.experimental.pallas import tpu as pltpu

F32 = jnp.float32
BF16 = jnp.bfloat16
HI = lax.Precision.HIGHEST

D_MODEL = 1024
ATTN_HEADS = 8
ATTN_HEAD_DIM = 64
ATTN_WIDTH = ATTN_HEADS * ATTN_HEAD_DIM
MOBA_BLOCK = 256
MOBA_TOPK = 3
REL_BUCKETS = 32
REL_MAX_DISTANCE = 128
DN_HEADS = 4
DN_HEAD = 128
DN_WIDTH = DN_HEADS * DN_HEAD
DN_CONV_WIDTH = 4
DN_CHUNK = 64
MAIN_COLS = 4 * ATTN_WIDTH + 3 * DN_WIDTH + DN_WIDTH
GATE_PAD = 128
EPS = 1e-6
LOG2E = math.log2(math.e)
NEG_BIG = -32768.0
VT_ROWS = ATTN_HEAD_DIM + 16
VMEM_LIMIT = 56 * 1024 * 1024


def _nt(a, b, precision=None):
    return lax.dot_general(a, b, (((1,), (1,)), ((), ())), preferred_element_type=F32, precision=precision)


def _nn(a, b, precision=None):
    return lax.dot_general(a, b, (((1,), (0,)), ((), ())), preferred_element_type=F32, precision=precision)


def _silu(x):
    half = 0.5 * x
    return half + half * jnp.tanh(half)


def _inproj_kernel(x_ref, nw_ref, w_ref, wg_ref, qnw_ref, knw_ref,
                   q_ref, k_ref, v_ref, za_ref, qkv_ref, zd_ref, g_ref):
    x = x_ref[...]
    ms = jnp.mean(x * x, axis=-1, keepdims=True)
    h = (x * lax.rsqrt(ms + EPS) * nw_ref[...]).astype(BF16)

    r_i = lax.broadcasted_iota(jnp.int32, (ATTN_WIDTH, 128), 0)
    c_i = lax.broadcasted_iota(jnp.int32, (ATTN_WIDTH, 128), 1)
    ind = jnp.where(r_i // ATTN_HEAD_DIM == c_i, 1.0, 0.0).astype(BF16)
    r_e = lax.broadcasted_iota(jnp.int32, (128, ATTN_WIDTH), 0)
    c_e = lax.broadcasted_iota(jnp.int32, (128, ATTN_WIDTH), 1)
    expand = jnp.where(c_e // ATTN_HEAD_DIM == r_e, 1.0, 0.0).astype(BF16)

    def split3(a):
        a0 = a.astype(BF16)
        r1 = a - a0.astype(F32)
        a1 = r1.astype(BF16)
        a2 = (r1 - a1.astype(F32)).astype(BF16)
        return a0, a1, a2

    def head_rms(t, w):
        s0, s1, s2 = split3(t * t)
        ss = _nn(s0, ind) + _nn(s1, ind) + _nn(s2, ind)
        r = lax.rsqrt(ss * (1.0 / ATTN_HEAD_DIM) + EPS)
        r0, r1, r2 = split3(r)
        rf = _nn(r0, expand) + _nn(r1, expand) + _nn(r2, expand)
        return t * rf * w

    def proj(c0, width):
        return _nn(h, w_ref[:, c0:c0 + width])

    q_ref[...] = head_rms(proj(0, ATTN_WIDTH), qnw_ref[...]).astype(q_ref.dtype)
    k_ref[...] = head_rms(proj(ATTN_WIDTH, ATTN_WIDTH), knw_ref[...]).astype(k_ref.dtype)
    v_ref[...] = proj(2 * ATTN_WIDTH, ATTN_WIDTH).astype(v_ref.dtype)
    za_ref[...] = proj(3 * ATTN_WIDTH, ATTN_WIDTH).astype(za_ref.dtype)
    for c in range(3):
        qkv_ref[:, c * DN_WIDTH:(c + 1) * DN_WIDTH] = proj(4 * ATTN_WIDTH + c * DN_WIDTH, DN_WIDTH).astype(qkv_ref.dtype)
    zd_ref[...] = proj(4 * ATTN_WIDTH + 3 * DN_WIDTH, DN_WIDTH).astype(zd_ref.dtype)
    g_ref[...] = _nn(h, wg_ref[...])


def _inproj(x2, norm_w, w_main, w_gate, qnw, knw, tm):
    n = x2.shape[0]
    row = lambda i: (i, 0)
    fixed = lambda i: (0, 0)
    outs = [
        jax.ShapeDtypeStruct((n, ATTN_WIDTH), F32),
        jax.ShapeDtypeStruct((n, ATTN_WIDTH), F32),
        jax.ShapeDtypeStruct((n, ATTN_WIDTH), BF16),
        jax.ShapeDtypeStruct((n, ATTN_WIDTH), F32),
        jax.ShapeDtypeStruct((n, 3 * DN_WIDTH), F32),
        jax.ShapeDtypeStruct((n, DN_WIDTH), F32),
        jax.ShapeDtypeStruct((n, GATE_PAD), F32),
    ]
    return pl.pallas_call(
        _inproj_kernel,
        grid=(n // tm,),
        in_specs=[
            pl.BlockSpec((tm, D_MODEL), row),
            pl.BlockSpec((1, D_MODEL), fixed),
            pl.BlockSpec((D_MODEL, MAIN_COLS), fixed),
            pl.BlockSpec((D_MODEL, GATE_PAD), fixed),
            pl.BlockSpec((1, ATTN_WIDTH), fixed),
            pl.BlockSpec((1, ATTN_WIDTH), fixed),
        ],
        out_specs=[pl.BlockSpec((tm, o.shape[1]), row) for o in outs],
        out_shape=outs,
        compiler_params=pltpu.CompilerParams(dimension_semantics=("arbitrary",), vmem_limit_bytes=VMEM_LIMIT),
        name="inproj",
    )(x2, norm_w, w_main, w_gate, qnw, knw)


def _t5_bias_tile_t(relb_ref, head, offset):
    c = lax.broadcasted_iota(jnp.int32, (MOBA_BLOCK, MOBA_BLOCK), 0)
    r = lax.broadcasted_iota(jnp.int32, (MOBA_BLOCK, MOBA_BLOCK), 1)
    dist = r - c + offset
    n = jnp.maximum(dist, 0)
    max_exact = REL_BUCKETS // 2
    nf = jnp.maximum(n, 1).astype(F32)
    large = max_exact + (jnp.log(nf / max_exact) / math.log(REL_MAX_DISTANCE / max_exact)
                         * (REL_BUCKETS - max_exact)).astype(jnp.int32)
    large = jnp.minimum(large, REL_BUCKETS - 1)
    bucket = jnp.where(n < max_exact, n, large)
    far = relb_ref[REL_BUCKETS - 1, head]
    bias = jnp.zeros((MOBA_BLOCK, MOBA_BLOCK), F32)
    for t in range(REL_BUCKETS):
        bias = jnp.where(bucket == t, relb_ref[t, head] - far, bias)
    return jnp.where(dist >= 0, bias * LOG2E, NEG_BIG)


def _moba_kernel(relb_ref, q_ref, k_ref, v_ref, z_ref, o_ref,
                 kaug_ref, vt_ref, kmean_ref, bias_ref, acc_ref, qa_ref, s_ref, *, nb):
    hp = pl.program_id(1)
    i = pl.program_id(2)
    HD = ATTN_HEAD_DIM
    lane = lax.broadcasted_iota(jnp.int32, (MOBA_BLOCK, 128), 1)
    lo_half = lane < HD
    er = lax.broadcasted_iota(jnp.int32, (128, 128), 0)
    ec = lax.broadcasted_iota(jnp.int32, (128, 128), 1)
    eye = jnp.where(er == ec, 1.0, 0.0).astype(BF16)

    @pl.when(i == 0)
    def _prepare():
        kmean_ref[...] = jnp.zeros_like(kmean_ref)
        lane1 = lax.broadcasted_iota(jnp.int32, (1, 128), 1)
        ones_rows = jnp.ones((VT_ROWS - HD, MOBA_BLOCK), BF16)

        def prep(j, carry):
            rows = pl.ds(pl.multiple_of(j * MOBA_BLOCK, MOBA_BLOCK), MOBA_BLOCK)
            kb = k_ref[0, rows, :]
            km = jnp.mean(kb, axis=0, keepdims=True)
            kmean_ref[0, pl.ds(j, 1), :] = jnp.where(lane1 < HD, km, 0.0)
            kmean_ref[1, pl.ds(j, 1), :] = jnp.where(lane1 < HD, 0.0, km)
            kaug_ref[0, j] = jnp.where(lo_half, kb, jnp.where(lane == HD + j, 1.0, 0.0)).astype(BF16)
            kaug_ref[1, j] = jnp.where(lo_half, jnp.where(lane == j, 1.0, 0.0), kb).astype(BF16)
            vt = _nt(eye, v_ref[0, rows, :]).astype(BF16)
            for hh in range(2):
                vt_ref[j, hh] = jnp.concatenate([vt[hh * HD:(hh + 1) * HD], ones_rows], axis=0)
            return carry

        lax.fori_loop(0, nb, prep, 0)
        kaug_ref[0, nb] = jnp.where(lane == 2 * HD - 1, 1.0, 0.0).astype(BF16)
        kaug_ref[1, nb] = jnp.where(lane == HD - 1, 1.0, 0.0).astype(BF16)
        vt_ref[nb] = jnp.zeros((2, VT_ROWS, MOBA_BLOCK), BF16)
        for hh in range(2):
            bias_ref[hh, 0] = _t5_bias_tile_t(relb_ref, 2 * hp + hh, 0)
            bias_ref[hh, 1] = _t5_bias_tile_t(relb_ref, 2 * hp + hh, MOBA_BLOCK)

    q = q_ref[0]
    qt = _nt(eye, (q * (HD ** -0.5 * LOG2E)).astype(BF16))
    blk = lax.broadcasted_iota(jnp.int32, (HD, MOBA_BLOCK), 0)

    def select(hh):
        gate = _nt(kmean_ref[hh], q, precision=HI)
        g = jnp.where(blk < i, gate, -jnp.inf)
        sel = blk == i
        for _ in range(MOBA_TOPK):
            mx = jnp.max(g, axis=0, keepdims=True)
            first = jnp.min(jnp.where(g == mx, blk, HD), axis=0, keepdims=True)
            hit = (blk == first) & (mx > -jnp.inf)
            sel = sel | hit
            g = jnp.where(hit, -jnp.inf, g)
        return jnp.where(sel, 0.0, NEG_BIG)

    qa_ref[0] = jnp.concatenate([qt[0:HD], select(0)], axis=0).astype(BF16)
    qa_ref[1] = jnp.concatenate([select(1), qt[HD:2 * HD]], axis=0).astype(BF16)

    def issue(j, buf_ref):
        for hh in range(2):
            buf_ref[hh] = _nn(kaug_ref[hh, j], qa_ref[hh])

    def consume(j, buf_ref, st, bias_idx=None, first=False):
        out = []
        for hh in range(2):
            s = buf_ref[hh]
            if bias_idx is not None:
                s = s + bias_ref[hh, bias_idx]
            m_cur = jnp.max(s, axis=0, keepdims=True)
            m_new = m_cur if first else jnp.maximum(st[hh], m_cur)
            p = jnp.exp2((s - m_new).astype(BF16))
            pv = _nn(vt_ref[j, hh], p)
            if first:
                acc_ref[hh] = pv
            else:
                acc_ref[hh] = jnp.exp2(st[hh] - m_new) * acc_ref[hh] + pv
            out.append(m_new)
        return tuple(out)

    nfar = jnp.maximum(i - 1, 0)
    far = lambda j: jnp.where(j < nfar, j, nb)
    a0_ref, b0_ref, a1_ref, b1_ref = (s_ref.at[n] for n in range(4))
    issue(i, a1_ref)
    issue(jnp.where(i >= 1, i - 1, nb), b1_ref)
    issue(far(0), a0_ref)
    issue(far(1), b0_ref)
    st = consume(i, a1_ref, None, bias_idx=0, first=True)
    st = consume(jnp.where(i >= 1, i - 1, nb), b1_ref, st, bias_idx=1)

    def far_quad(u, st):
        j = 4 * u
        issue(far(j + 2), a1_ref)
        issue(far(j + 3), b1_ref)
        st = consume(far(j), a0_ref, st)
        st = consume(far(j + 1), b0_ref, st)
        issue(far(j + 4), a0_ref)
        issue(far(j + 5), b0_ref)
        st = consume(far(j + 2), a1_ref, st)
        return consume(far(j + 3), b1_ref, st)

    st = lax.fori_loop(0, nfar // 4, far_quad, st)
    base = (nfar // 4) * 4
    rem = nfar - base

    def rem_first(st):
        issue(far(base + 2), a1_ref)
        st = consume(far(base), a0_ref, st)
        return consume(far(base + 1), b0_ref, st)

    st = lax.cond(rem >= 1, rem_first, lambda st: st, st)
    lax.cond(rem >= 3, lambda st: consume(far(base + 2), a1_ref, st), lambda st: st, st)

    ot = jnp.concatenate([acc_ref[hh, 0:HD, :] / acc_ref[hh, HD:HD + 1, :] for hh in range(2)], axis=0)
    o_ref[0] = (ot.T * _silu(z_ref[0])).astype(o_ref.dtype)


def _moba(rel_bias, q, k, v, z):
    b, s, _ = q.shape
    nb = s // MOBA_BLOCK
    assert s % MOBA_BLOCK == 0 and nb < ATTN_HEAD_DIM
    pairs = ATTN_HEADS // 2
    blk = lambda bi, hp, i: (bi, i, hp)
    whole = lambda bi, hp, i: (bi, 0, hp)
    return pl.pallas_call(
        functools.partial(_moba_kernel, nb=nb),
        grid=(b, pairs, nb),
        in_specs=[
            pl.BlockSpec(memory_space=pltpu.SMEM),
            pl.BlockSpec((1, MOBA_BLOCK, 128), blk),
            pl.BlockSpec((1, s, 128), whole),
            pl.BlockSpec((1, s, 128), whole),
            pl.BlockSpec((1, MOBA_BLOCK, 128), blk),
        ],
        out_specs=pl.BlockSpec((1, MOBA_BLOCK, 128), blk),
        out_shape=jax.ShapeDtypeStruct((b, s, ATTN_WIDTH), BF16),
        scratch_shapes=[
            pltpu.VMEM((2, nb + 1, MOBA_BLOCK, 128), BF16),
            pltpu.VMEM((nb + 1, 2, VT_ROWS, MOBA_BLOCK), BF16),
            pltpu.VMEM((2, ATTN_HEAD_DIM, 128), F32),
            pltpu.VMEM((2, 2, MOBA_BLOCK, MOBA_BLOCK), F32),
            pltpu.VMEM((2, VT_ROWS, MOBA_BLOCK), F32),
            pltpu.VMEM((2, 128, MOBA_BLOCK), BF16),
            pltpu.VMEM((4, 2, MOBA_BLOCK, MOBA_BLOCK), F32),
        ],
        compiler_params=pltpu.CompilerParams(
            dimension_semantics=("arbitrary", "arbitrary", "arbitrary"), vmem_limit_bytes=VMEM_LIMIT),
        name="moba",
    )(rel_bias, q, k, v, z)


DN_ROWS = DN_HEADS * DN_CHUNK
DN_PREC = dict(kk="bf16", qk="bf16", inv="bf16", uw="bf16", state="bf16", intra="bf16")


def _split2(a):
    a0 = a.astype(BF16)
    return a0, (a - a0.astype(F32)).astype(BF16)


def _split3(a):
    a0 = a.astype(BF16)
    r1 = a - a0.astype(F32)
    a1 = r1.astype(BF16)
    return a0, a1, (r1 - a1.astype(F32)).astype(BF16)


def _dot(a, b, prec, dims=(((1,), (0,)), ((), ()))):
    f = lambda x, y: lax.dot_general(x, y, dims, preferred_element_type=F32)
    if prec == "bf16":
        return f(a.astype(BF16), b.astype(BF16))
    assert prec == "x3"
    a0, a1 = _split2(a)
    b0, b1 = _split2(b)
    return f(a0, b0) + (f(a0, b1) + f(a1, b0))


_NT = (((1,), (1,)), ((), ()))
_TN = (((0,), (0,)), ((), ()))


def _dot_sel(sel, b):
    return sum(_nn(sel, p) for p in _split3(b))


def _deltanet_kernel(alog_ref, dtb_ref, x_ref, z_ref, g_ref, cw_ref, nw_ref, o_ref,
                     state_ref, pad_ref, act_ref, rep_ref, *, tb):
    t = pl.program_id(1)
    C, R, W, H, D = DN_CHUNK, DN_ROWS, DN_WIDTH, DN_HEADS, DN_HEAD
    hist = DN_CONV_WIDTH - 1

    @pl.when(t == 0)
    def _reset():
        state_ref[...] = jnp.zeros_like(state_ref)
        pad_ref[0:8, :] = jnp.zeros((8, 3 * W), F32)

    pad_ref[8:8 + tb, :] = x_ref[0]
    for gcol in range(3 * H):
        cols = slice(gcol * D, (gcol + 1) * D)
        acc = pad_ref[pl.ds(8 - hist, tb), cols] * cw_ref[0:1, cols]
        for w in range(1, DN_CONV_WIDTH):
            acc = acc + pad_ref[pl.ds(8 - hist + w, tb), cols] * cw_ref[w:w + 1, cols]
        a = _silu(acc)
        if gcol < 2 * H:
            a = a * lax.rsqrt(jnp.sum(a * a, axis=-1, keepdims=True) + EPS)
            if gcol < H:
                a = a * (D ** -0.5)
        act_ref[:, cols] = a
    pad_ref[0:8, :] = x_ref[0, tb - 8:tb, :]

    lane1 = lax.broadcasted_iota(jnp.int32, (1, GATE_PAD), 1)
    alog_row = jnp.zeros((1, GATE_PAD), F32)
    dtb_row = jnp.zeros((1, GATE_PAD), F32)
    for h in range(H):
        alog_row = jnp.where(lane1 == H + h, alog_ref[h], alog_row)
        dtb_row = jnp.where(lane1 == H + h, dtb_ref[h], dtb_row)
    gates = g_ref[0]
    xs = gates + dtb_row
    softplus = jnp.maximum(xs, 0.0) + jnp.log(1.0 + jnp.exp(-jnp.abs(xs)))
    gval = jnp.where(lane1 < H, 1.0 / (1.0 + jnp.exp(-gates)), -jnp.exp(alog_row) * softplus)
    er = lax.broadcasted_iota(jnp.int32, (GATE_PAD, 2 * H * D), 0)
    ec = lax.broadcasted_iota(jnp.int32, (GATE_PAD, 2 * H * D), 1)
    spread = jnp.where(er == ec // D, 1.0, 0.0).astype(BF16)
    rep_ref[...] = sum(_nn(p, spread) for p in _split3(gval))

    r = lax.broadcasted_iota(jnp.int32, (R, R), 0)
    c = lax.broadcasted_iota(jnp.int32, (R, R), 1)
    same = (r // C) == (c // C)
    incl = same & (r >= c)
    strict = same & (r > c)
    eye = jnp.where(r == c, 1.0, 0.0)
    cum = jnp.where(incl, 1.0, 0.0).astype(BF16)
    lane_r = lax.broadcasted_iota(jnp.int32, (R, D), 1)
    P = DN_PREC

    chunks = range(tb // C)
    each = lambda f, *seqs: [f(*a) for a in zip(*seqs)]
    rows = [slice(ci * C, (ci + 1) * C) for ci in chunks]
    stack = lambda ref, c0: [jnp.concatenate([ref[rw, c0 + D * h:c0 + D * (h + 1)] for h in range(H)], axis=0)
                             for rw in rows]
    qc, kc, vc = stack(act_ref, 0), stack(act_ref, W), stack(act_ref, 2 * W)
    beta = stack(rep_ref, 0)
    gc = each(lambda g: _dot_sel(cum, g), stack(rep_ref, H * D))

    def pair_diff(g):
        p0, p1, p2 = (p.astype(F32) for p in _split3(g))
        u_m = jnp.where(lane_r == 0, p0, jnp.where(lane_r == 1, p1, jnp.where(lane_r == 2, p2,
                        jnp.where(lane_r < 6, 1.0, 0.0))))
        w_m = jnp.where(lane_r < 3, 1.0, jnp.where(lane_r == 3, -p0, jnp.where(lane_r == 4, -p1,
                        jnp.where(lane_r == 5, -p2, 0.0))))
        return _nt(u_m.astype(BF16), w_m.astype(BF16))

    decay = each(lambda g: jnp.where(incl, jnp.exp(jnp.where(incl, pair_diff(g), 0.0)), 0.0), gc)
    kb = each(lambda k, b: k * b, kc, beta)
    x = each(lambda a, k, d: -jnp.where(strict, _dot(a, k, P["kk"], _NT) * d, 0.0), kb, kc, decay)
    sq = lambda m: each(lambda a: _dot(a, a, P["inv"]), m)
    mul = lambda ma, mb: each(lambda a, b: _dot(a, b, P["inv"]), ma, mb)
    add = lambda ma, mb: each(lambda a, b: a + b, ma, mb)
    plus_eye = lambda m: each(lambda a: eye + a, m)
    x2 = sq(x)
    x4 = sq(x2)
    f1 = plus_eye(x)
    m1 = add(f1, mul(f1, x2))
    x8 = sq(x4)
    f4 = plus_eye(x4)
    m2 = add(f4, mul(f4, x8))
    x16 = sq(x8)
    m12 = mul(m1, m2)
    x32 = sq(x16)
    f16 = plus_eye(x16)
    m3 = add(f16, mul(f16, x32))
    tinv = mul(m12, m3)
    eg = each(jnp.exp, gc)
    uw = each(lambda t, v, b, k, e: _dot(t, jnp.concatenate([v * b, k * e], axis=1), P["uw"]),
              tinv, vc, beta, kb, eg)
    a_intra = each(lambda q, k, d: jnp.where(incl, _dot(q, k, P["qk"], _NT) * d, 0.0), qc, kc, decay)
    glast = [[g[h * C + C - 1:h * C + C, :] for h in range(H)] for g in gc]
    q_dec = each(lambda q, e: q * e, qc, eg)
    k_dec = each(lambda k, g, gl: k * jnp.exp(
        jnp.concatenate([jnp.broadcast_to(t, (C, D)) for t in gl], axis=0) - g), kc, gc, glast)

    hs = [slice(h * C, (h + 1) * C) for h in range(H)]
    for ci in chunks:
        state = [state_ref[h] for h in range(H)]
        wq = [_dot(jnp.concatenate([uw[ci][hs[h], D:2 * D], q_dec[ci][hs[h]]], axis=0), state[h], P["state"])
              for h in range(H)]
        vn = [uw[ci][hs[h], 0:D] - wq[h][0:C] for h in range(H)]
        for h in range(H):
            state_ref[h] = state[h] * jnp.exp(glast[ci][h]) + _dot(k_dec[ci][hs[h]], vn[h], P["state"], _TN)
        o = jnp.concatenate([wq[h][C:2 * C] for h in range(H)], axis=0) + _dot(
            a_intra[ci], jnp.concatenate(vn, axis=0), P["intra"])
        on = o * lax.rsqrt(jnp.mean(o * o, axis=-1, keepdims=True) + EPS) * nw_ref[...]
        for h in range(H):
            cols = slice(h * D, (h + 1) * D)
            o_ref[0, rows[ci], cols] = (on[hs[h]] * _silu(z_ref[0, rows[ci], cols])).astype(o_ref.dtype)


def _deltanet(a_log, dt_bias, qkv, z, gates, conv_w, dn_norm_w, tb):
    b, s, _ = qkv.shape
    assert s % tb == 0 and tb % DN_CHUNK == 0
    blk = lambda bi, t: (bi, t, 0)
    fixed = lambda bi, t: (0, 0)
    return pl.pallas_call(
        functools.partial(_deltanet_kernel, tb=tb),
        grid=(b, s // tb),
        in_specs=[
            pl.BlockSpec(memory_space=pltpu.SMEM),
            pl.BlockSpec(memory_space=pltpu.SMEM),
            pl.BlockSpec((1, tb, 3 * DN_WIDTH), blk),
            pl.BlockSpec((1, tb, DN_WIDTH), blk),
            pl.BlockSpec((1, tb, GATE_PAD), blk),
            pl.BlockSpec((DN_CONV_WIDTH, 3 * DN_WIDTH), fixed),
            pl.BlockSpec((1, DN_HEAD), fixed),
        ],
        out_specs=pl.BlockSpec((1, tb, DN_WIDTH), blk),
        out_shape=jax.ShapeDtypeStruct((b, s, DN_WIDTH), BF16),
        scratch_shapes=[
            pltpu.VMEM((DN_HEADS, DN_HEAD, DN_HEAD), F32),
            pltpu.VMEM((tb + 8, 3 * DN_WIDTH), F32),
            pltpu.VMEM((tb, 3 * DN_WIDTH), F32),
            pltpu.VMEM((tb, 2 * DN_HEADS * DN_HEAD), F32),
        ],
        compiler_params=pltpu.CompilerParams(
            dimension_semantics=("arbitrary", "arbitrary"), vmem_limit_bytes=VMEM_LIMIT),
        name="deltanet",
    )(a_log, dt_bias, qkv, z, gates, conv_w, dn_norm_w)


def _outproj_kernel(x_ref, ya_ref, yd_ref, wa_ref, wd_ref, o_ref):
    o_ref[...] = x_ref[...] + _nn(ya_ref[...], wa_ref[...]) + _nn(yd_ref[...], wd_ref[...])


def _outproj(x2, ya, yd, wa, wd, tm):
    n = x2.shape[0]
    row = lambda i: (i, 0)
    fixed = lambda i: (0, 0)
    return pl.pallas_call(
        _outproj_kernel,
        grid=(n // tm,),
        in_specs=[
            pl.BlockSpec((tm, D_MODEL), row),
            pl.BlockSpec((tm, ATTN_WIDTH), row),
            pl.BlockSpec((tm, DN_WIDTH), row),
            pl.BlockSpec((ATTN_WIDTH, D_MODEL), fixed),
            pl.BlockSpec((DN_WIDTH, D_MODEL), fixed),
        ],
        out_specs=pl.BlockSpec((tm, D_MODEL), row),
        out_shape=jax.ShapeDtypeStruct((n, D_MODEL), F32),
        compiler_params=pltpu.CompilerParams(dimension_semantics=("arbitrary",), vmem_limit_bytes=VMEM_LIMIT),
        name="outproj",
    )(x2, ya, yd, wa, wd)


def _row_tile(n):
    for tm in (512, 256, 128, 64, 32, 16, 8):
        if n % tm == 0:
            return tm
    raise ValueError(f"row count {n} is not a multiple of 8")


def kernel(x, rel_bias, norm_w, w_in, q_norm_w, k_norm_w, conv_w, a_log, dt_bias, dn_norm_w, w_out):
    b, s, d = x.shape
    assert d == D_MODEL and norm_w.shape[0] == 1, "single-layer kernel"
    n = b * s
    x2 = x.reshape(n, d)
    tm = _row_tile(n)

    w_main = w_in[0][:, :MAIN_COLS].astype(BF16)
    w_gate = jnp.pad(w_in[0][:, MAIN_COLS:], ((0, 0), (0, GATE_PAD - 2 * DN_HEADS))).astype(BF16)
    qnw = jnp.tile(q_norm_w[0], ATTN_HEADS)[None, :]
    knw = jnp.tile(k_norm_w[0], ATTN_HEADS)[None, :]

    q, k, v, za, qkv, zd, gates = _inproj(x2, norm_w, w_main, w_gate, qnw, knw, tm)

    r3 = lambda t: t.reshape(b, s, t.shape[-1])
    ya = _moba(rel_bias, r3(q), r3(k), r3(v), r3(za))
    tb = 256 if s % 256 == 0 else DN_CHUNK
    yd = _deltanet(a_log[0], dt_bias[0], r3(qkv), r3(zd), r3(gates), conv_w[0], dn_norm_w, tb)

    w_o = w_out[0].astype(BF16)
    out = _outproj(x2, ya.reshape(n, ATTN_WIDTH), yd.reshape(n, DN_WIDTH), w_o[:ATTN_WIDTH], w_o[ATTN_WIDTH:], tm)
    return out.reshape(b, s, d)
```

```python
import functools
import math

import jax
import jax.numpy as jnp
from jax import lax
from jax.experimental import pallas as pl
from jax.experimental.pallas import tpu as pltpu

F32 = jnp.float32
BF16 = jnp.bfloat16
HI = lax.Precision.HIGHEST

D_MODEL = 1024
ATTN_HEADS = 8
ATTN_HEAD_DIM = 64
ATTN_WIDTH = ATTN_HEADS * ATTN_HEAD_DIM
MOBA_BLOCK = 256
MOBA_TOPK = 3
REL_BUCKETS = 32
REL_MAX_DISTANCE = 128
DN_HEADS = 4
DN_HEAD = 128
DN_WIDTH = DN_HEADS * DN_HEAD
DN_CONV_WIDTH = 4
DN_CHUNK = 64
MAIN_COLS = 4 * ATTN_WIDTH + 3 * DN_WIDTH + DN_WIDTH
GATE_PAD = 128
EPS = 1e-6
LOG2E = math.log2(math.e)
NEG_BIG = -32768.0
VT_ROWS = ATTN_HEAD_DIM + 16
VMEM_LIMIT = 56 * 1024 * 1024


def _nt(a, b, precision=None):
    return lax.dot_general(a, b, (((1,), (1,)), ((), ())), preferred_element_type=F32, precision=precision)


def _nn(a, b, precision=None):
    return lax.dot_general(a, b, (((1,), (0,)), ((), ())), preferred_element_type=F32, precision=precision)


def _silu(x):
    half = 0.5 * x
    return half + half * jnp.tanh(half)


def _inproj_kernel(x_ref, nw_ref, w_ref, wg_ref, qnw_ref, knw_ref,
                   q_ref, k_ref, v_ref, za_ref, qkv_ref, zd_ref, g_ref):
    x = x_ref[...]
    ms = jnp.mean(x * x, axis=-1, keepdims=True)
    h = (x * lax.rsqrt(ms + EPS) * nw_ref[...]).astype(BF16)

    r_i = lax.broadcasted_iota(jnp.int32, (ATTN_WIDTH, 128), 0)
    c_i = lax.broadcasted_iota(jnp.int32, (ATTN_WIDTH, 128), 1)
    ind = jnp.where(r_i // ATTN_HEAD_DIM == c_i, 1.0, 0.0).astype(BF16)
    r_e = lax.broadcasted_iota(jnp.int32, (128, ATTN_WIDTH), 0)
    c_e = lax.broadcasted_iota(jnp.int32, (128, ATTN_WIDTH), 1)
    expand = jnp.where(c_e // ATTN_HEAD_DIM == r_e, 1.0, 0.0).astype(BF16)

    def split3(a):
        a0 = a.astype(BF16)
        r1 = a - a0.astype(F32)
        a1 = r1.astype(BF16)
        a2 = (r1 - a1.astype(F32)).astype(BF16)
        return a0, a1, a2

    def head_rms(t, w):
        s0, s1, s2 = split3(t * t)
        ss = _nn(s0, ind) + _nn(s1, ind) + _nn(s2, ind)
        r = lax.rsqrt(ss * (1.0 / ATTN_HEAD_DIM) + EPS)
        r0, r1, r2 = split3(r)
        rf = _nn(r0, expand) + _nn(r1, expand) + _nn(r2, expand)
        return t * rf * w

    def proj(c0, width):
        return _nn(h, w_ref[:, c0:c0 + width])

    q_ref[...] = head_rms(proj(0, ATTN_WIDTH), qnw_ref[...]).astype(q_ref.dtype)
    k_ref[...] = head_rms(proj(ATTN_WIDTH, ATTN_WIDTH), knw_ref[...]).astype(k_ref.dtype)
    v_ref[...] = proj(2 * ATTN_WIDTH, ATTN_WIDTH).astype(v_ref.dtype)
    za_ref[...] = proj(3 * ATTN_WIDTH, ATTN_WIDTH).astype(za_ref.dtype)
    for c in range(3):
        qkv_ref[:, c * DN_WIDTH:(c + 1) * DN_WIDTH] = proj(4 * ATTN_WIDTH + c * DN_WIDTH, DN_WIDTH).astype(qkv_ref.dtype)
    zd_ref[...] = proj(4 * ATTN_WIDTH + 3 * DN_WIDTH, DN_WIDTH).astype(zd_ref.dtype)
    g_ref[...] = _nn(h, wg_ref[...])


def _inproj(x2, norm_w, w_main, w_gate, qnw, knw, tm):
    n = x2.shape[0]
    row = lambda i: (i, 0)
    fixed = lambda i: (0, 0)
    outs = [
        jax.ShapeDtypeStruct((n, ATTN_WIDTH), F32),
        jax.ShapeDtypeStruct((n, ATTN_WIDTH), F32),
        jax.ShapeDtypeStruct((n, ATTN_WIDTH), BF16),
        jax.ShapeDtypeStruct((n, ATTN_WIDTH), F32),
        jax.ShapeDtypeStruct((n, 3 * DN_WIDTH), F32),
        jax.ShapeDtypeStruct((n, DN_WIDTH), F32),
        jax.ShapeDtypeStruct((n, GATE_PAD), F32),
    ]
    return pl.pallas_call(
        _inproj_kernel,
        grid=(n // tm,),
        in_specs=[
            pl.BlockSpec((tm, D_MODEL), row),
            pl.BlockSpec((1, D_MODEL), fixed),
            pl.BlockSpec((D_MODEL, MAIN_COLS), fixed),
            pl.BlockSpec((D_MODEL, GATE_PAD), fixed),
            pl.BlockSpec((1, ATTN_WIDTH), fixed),
            pl.BlockSpec((1, ATTN_WIDTH), fixed),
        ],
        out_specs=[pl.BlockSpec((tm, o.shape[1]), row) for o in outs],
        out_shape=outs,
        compiler_params=pltpu.CompilerParams(dimension_semantics=("arbitrary",), vmem_limit_bytes=VMEM_LIMIT),
        name="inproj",
    )(x2, norm_w, w_main, w_gate, qnw, knw)


def _t5_bias_tile_t(relb_ref, head, offset):
    c = lax.broadcasted_iota(jnp.int32, (MOBA_BLOCK, MOBA_BLOCK), 0)
    r = lax.broadcasted_iota(jnp.int32, (MOBA_BLOCK, MOBA_BLOCK), 1)
    dist = r - c + offset
    n = jnp.maximum(dist, 0)
    max_exact = REL_BUCKETS // 2
    nf = jnp.maximum(n, 1).astype(F32)
    large = max_exact + (jnp.log(nf / max_exact) / math.log(REL_MAX_DISTANCE / max_exact)
                         * (REL_BUCKETS - max_exact)).astype(jnp.int32)
    large = jnp.minimum(large, REL_BUCKETS - 1)
    bucket = jnp.where(n < max_exact, n, large)
    far = relb_ref[REL_BUCKETS - 1, head]
    bias = jnp.zeros((MOBA_BLOCK, MOBA_BLOCK), F32)
    for t in range(REL_BUCKETS):
        bias = jnp.where(bucket == t, relb_ref[t, head] - far, bias)
    return jnp.where(dist >= 0, bias * LOG2E, NEG_BIG)


def _moba_kernel(relb_ref, q_ref, qn_ref, k_ref, v_ref, z_ref, o_ref,
                 kaug_ref, vt_ref, kmean_ref, bias_ref, acc_ref, qa_ref, s_ref, *, nb):
    hp = pl.program_id(1)
    i = pl.program_id(2)
    par = i % 2
    HD = ATTN_HEAD_DIM
    lane = lax.broadcasted_iota(jnp.int32, (MOBA_BLOCK, 128), 1)
    lo_half = lane < HD
    er = lax.broadcasted_iota(jnp.int32, (128, 128), 0)
    ec = lax.broadcasted_iota(jnp.int32, (128, 128), 1)
    eye = jnp.where(er == ec, 1.0, 0.0).astype(BF16)
    blk = lax.broadcasted_iota(jnp.int32, (HD, MOBA_BLOCK), 0)

    def gate_scores(q):
        qt = _nt(eye, (q * (HD ** -0.5 * LOG2E)).astype(BF16))
        q0, q1 = _split2(q)
        gates = []
        for hh in range(2):
            k0, k1 = _split2(kmean_ref[hh])
            gates.append(_nt(k0, q0) + (_nt(k0, q1) + _nt(k1, q0)))
        return qt, gates

    def store_operands(slot, qt, gates, tile):
        for hh in range(2):
            g = jnp.where(blk < tile, gates[hh], -jnp.inf)
            sel = blk == tile
            for _ in range(MOBA_TOPK):
                mx = jnp.max(g, axis=0, keepdims=True)
                first = jnp.min(jnp.where(g == mx, blk, HD), axis=0, keepdims=True)
                hit = (blk == first) & (mx > -jnp.inf)
                sel = sel | hit
                g = jnp.where(hit, -jnp.inf, g)
            mask = jnp.where(sel, 0.0, NEG_BIG)
            rows = [qt[0:HD], mask] if hh == 0 else [mask, qt[HD:2 * HD]]
            qa_ref[slot, hh] = jnp.concatenate(rows, axis=0).astype(BF16)

    @pl.when(i == 0)
    def _prepare():
        kmean_ref[...] = jnp.zeros_like(kmean_ref)
        lane1 = lax.broadcasted_iota(jnp.int32, (1, 128), 1)
        ones_rows = jnp.ones((VT_ROWS - HD, MOBA_BLOCK), BF16)

        def prep(j, carry):
            rows = pl.ds(pl.multiple_of(j * MOBA_BLOCK, MOBA_BLOCK), MOBA_BLOCK)
            kb = k_ref[0, rows, :]
            km = jnp.mean(kb, axis=0, keepdims=True)
            kmean_ref[0, pl.ds(j, 1), :] = jnp.where(lane1 < HD, km, 0.0)
            kmean_ref[1, pl.ds(j, 1), :] = jnp.where(lane1 < HD, 0.0, km)
            kaug_ref[0, j] = jnp.where(lo_half, kb, jnp.where(lane == HD + j, 1.0, 0.0)).astype(BF16)
            kaug_ref[1, j] = jnp.where(lo_half, jnp.where(lane == j, 1.0, 0.0), kb).astype(BF16)
            vt = _nt(eye, v_ref[0, rows, :]).astype(BF16)
            for hh in range(2):
                vt_ref[j, hh] = jnp.concatenate([vt[hh * HD:(hh + 1) * HD], ones_rows], axis=0)
            return carry

        lax.fori_loop(0, nb, prep, 0)
        kaug_ref[0, nb] = jnp.where(lane == 2 * HD - 1, 1.0, 0.0).astype(BF16)
        kaug_ref[1, nb] = jnp.where(lane == HD - 1, 1.0, 0.0).astype(BF16)
        vt_ref[nb] = jnp.zeros((2, VT_ROWS, MOBA_BLOCK), BF16)
        for hh in range(2):
            bias_ref[hh, 0] = _t5_bias_tile_t(relb_ref, 2 * hp + hh, 0)
            bias_ref[hh, 1] = _t5_bias_tile_t(relb_ref, 2 * hp + hh, MOBA_BLOCK)
        store_operands(0, *gate_scores(q_ref[0]), 0)

    def issue(j, buf_ref):
        for hh in range(2):
            buf_ref[hh] = _nn(kaug_ref[hh, j], qa_ref[par, hh])

    def consume(j, buf_ref, st, bias_idx=None, first=False):
        out = []
        for hh in range(2):
            s = buf_ref[hh]
            if bias_idx is not None:
                s = s + bias_ref[hh, bias_idx]
            m_cur = jnp.max(s, axis=0, keepdims=True)
            m_new = m_cur if first else jnp.maximum(st[hh], m_cur)
            p = jnp.exp2((s - m_new).astype(BF16))
            pv = _nn(vt_ref[j, hh], p)
            if first:
                acc_ref[hh] = pv
            else:
                acc_ref[hh] = jnp.exp2(st[hh] - m_new) * acc_ref[hh] + pv
            out.append(m_new)
        return tuple(out)

    nfar = jnp.maximum(i - 1, 0)
    far = lambda j: jnp.where(j < nfar, j, nb)
    a0_ref, b0_ref, a1_ref, b1_ref = (s_ref.at[n] for n in range(4))
    qt_next, gates_next = gate_scores(qn_ref[0])
    issue(i, a1_ref)
    issue(jnp.where(i >= 1, i - 1, nb), b1_ref)
    issue(far(0), a0_ref)
    issue(far(1), b0_ref)
    store_operands(1 - par, qt_next, gates_next, i + 1)
    st = consume(i, a1_ref, None, bias_idx=0, first=True)
    st = consume(jnp.where(i >= 1, i - 1, nb), b1_ref, st, bias_idx=1)

    def far_quad(u, st):
        j = 4 * u
        issue(far(j + 2), a1_ref)
        issue(far(j + 3), b1_ref)
        st = consume(far(j), a0_ref, st)
        issue(far(j + 4), a0_ref)
        st = consume(far(j + 1), b0_ref, st)
        issue(far(j + 5), b0_ref)
        st = consume(far(j + 2), a1_ref, st)
        return consume(far(j + 3), b1_ref, st)

    st = lax.fori_loop(0, nfar // 4, far_quad, st)
    base = (nfar // 4) * 4
    rem = nfar - base

    def rem_first(st):
        issue(far(base + 2), a1_ref)
        st = consume(far(base), a0_ref, st)
        return consume(far(base + 1), b0_ref, st)

    st = lax.cond(rem >= 1, rem_first, lambda st: st, st)
    lax.cond(rem >= 3, lambda st: consume(far(base + 2), a1_ref, st), lambda st: st, st)

    ot = jnp.concatenate([acc_ref[hh, 0:HD, :] / acc_ref[hh, HD:HD + 1, :] for hh in range(2)], axis=0)
    o_ref[0] = (ot.T * _silu(z_ref[0])).astype(o_ref.dtype)


def _moba(rel_bias, q, k, v, z):
    b, s, _ = q.shape
    nb = s // MOBA_BLOCK
    assert s % MOBA_BLOCK == 0 and nb < ATTN_HEAD_DIM
    pairs = ATTN_HEADS // 2
    blk = lambda bi, hp, i: (bi, i, hp)
    whole = lambda bi, hp, i: (bi, 0, hp)
    return pl.pallas_call(
        functools.partial(_moba_kernel, nb=nb),
        grid=(b, pairs, nb),
        in_specs=[
            pl.BlockSpec(memory_space=pltpu.SMEM),
            pl.BlockSpec((1, MOBA_BLOCK, 128), blk),
            pl.BlockSpec((1, MOBA_BLOCK, 128), lambda bi, hp, i: (bi, jnp.minimum(i + 1, nb - 1), hp)),
            pl.BlockSpec((1, s, 128), whole),
            pl.BlockSpec((1, s, 128), whole),
            pl.BlockSpec((1, MOBA_BLOCK, 128), blk),
        ],
        out_specs=pl.BlockSpec((1, MOBA_BLOCK, 128), blk),
        out_shape=jax.ShapeDtypeStruct((b, s, ATTN_WIDTH), BF16),
        scratch_shapes=[
            pltpu.VMEM((2, nb + 1, MOBA_BLOCK, 128), BF16),
            pltpu.VMEM((nb + 1, 2, VT_ROWS, MOBA_BLOCK), BF16),
            pltpu.VMEM((2, ATTN_HEAD_DIM, 128), F32),
            pltpu.VMEM((2, 2, MOBA_BLOCK, MOBA_BLOCK), F32),
            pltpu.VMEM((2, VT_ROWS, MOBA_BLOCK), F32),
            pltpu.VMEM((2, 2, 128, MOBA_BLOCK), BF16),
            pltpu.VMEM((4, 2, MOBA_BLOCK, MOBA_BLOCK), F32),
        ],
        compiler_params=pltpu.CompilerParams(
            dimension_semantics=("arbitrary", "arbitrary", "arbitrary"), vmem_limit_bytes=VMEM_LIMIT),
        name="moba",
    )(rel_bias, q, q, k, v, z)


DN_ROWS = DN_HEADS * DN_CHUNK
DN_PREC = dict(kk="bf16", qk="bf16", inv="bf16", uw="bf16", state="bf16", intra="bf16")


def _split2(a):
    a0 = a.astype(BF16)
    return a0, (a - a0.astype(F32)).astype(BF16)


def _split3(a):
    a0 = a.astype(BF16)
    r1 = a - a0.astype(F32)
    a1 = r1.astype(BF16)
    return a0, a1, (r1 - a1.astype(F32)).astype(BF16)


def _dot(a, b, prec, dims=(((1,), (0,)), ((), ()))):
    f = lambda x, y: lax.dot_general(x, y, dims, preferred_element_type=F32)
    if prec == "bf16":
        return f(a.astype(BF16), b.astype(BF16))
    assert prec == "x3"
    a0, a1 = _split2(a)
    b0, b1 = _split2(b)
    return f(a0, b0) + (f(a0, b1) + f(a1, b0))


_NT = (((1,), (1,)), ((), ()))
_TN = (((0,), (0,)), ((), ()))


def _dot_sel(sel, b):
    return sum(_nn(sel, p) for p in _split3(b))


def _deltanet_kernel(alog_ref, dtb_ref, x_ref, z_ref, g_ref, cw_ref, nw_ref, o_ref,
                     state_ref, pad_ref, act_ref, rep_ref, *, tb):
    t = pl.program_id(1)
    C, R, W, H, D = DN_CHUNK, DN_ROWS, DN_WIDTH, DN_HEADS, DN_HEAD
    hist = DN_CONV_WIDTH - 1

    @pl.when(t == 0)
    def _reset():
        state_ref[...] = jnp.zeros_like(state_ref)
        pad_ref[0:8, :] = jnp.zeros((8, 3 * W), F32)

    pad_ref[8:8 + tb, :] = x_ref[0]
    for gcol in range(3 * H):
        cols = slice(gcol * D, (gcol + 1) * D)
        acc = pad_ref[pl.ds(8 - hist, tb), cols] * cw_ref[0:1, cols]
        for w in range(1, DN_CONV_WIDTH):
            acc = acc + pad_ref[pl.ds(8 - hist + w, tb), cols] * cw_ref[w:w + 1, cols]
        a = _silu(acc)
        if gcol < 2 * H:
            a = a * lax.rsqrt(jnp.sum(a * a, axis=-1, keepdims=True) + EPS)
            if gcol < H:
                a = a * (D ** -0.5)
        act_ref[:, cols] = a
    pad_ref[0:8, :] = x_ref[0, tb - 8:tb, :]

    lane1 = lax.broadcasted_iota(jnp.int32, (1, GATE_PAD), 1)
    alog_row = jnp.zeros((1, GATE_PAD), F32)
    dtb_row = jnp.zeros((1, GATE_PAD), F32)
    for h in range(H):
        alog_row = jnp.where(lane1 == H + h, alog_ref[h], alog_row)
        dtb_row = jnp.where(lane1 == H + h, dtb_ref[h], dtb_row)
    gates = g_ref[0]
    xs = gates + dtb_row
    softplus = jnp.maximum(xs, 0.0) + jnp.log(1.0 + jnp.exp(-jnp.abs(xs)))
    gval = jnp.where(lane1 < H, 1.0 / (1.0 + jnp.exp(-gates)), -jnp.exp(alog_row) * softplus)
    er = lax.broadcasted_iota(jnp.int32, (GATE_PAD, 2 * H * D), 0)
    ec = lax.broadcasted_iota(jnp.int32, (GATE_PAD, 2 * H * D), 1)
    spread = jnp.where(er == ec // D, 1.0, 0.0).astype(BF16)
    rep_ref[...] = sum(_nn(p, spread) for p in _split3(gval))

    r = lax.broadcasted_iota(jnp.int32, (R, R), 0)
    c = lax.broadcasted_iota(jnp.int32, (R, R), 1)
    same = (r // C) == (c // C)
    incl = same & (r >= c)
    strict = same & (r > c)
    eye = jnp.where(r == c, 1.0, 0.0)
    cum = jnp.where(incl, 1.0, 0.0).astype(BF16)
    lane_r = lax.broadcasted_iota(jnp.int32, (R, D), 1)
    P = DN_PREC

    chunks = range(tb // C)
    each = lambda f, *seqs: [f(*a) for a in zip(*seqs)]
    rows = [slice(ci * C, (ci + 1) * C) for ci in chunks]
    stack = lambda ref, c0: [jnp.concatenate([ref[rw, c0 + D * h:c0 + D * (h + 1)] for h in range(H)], axis=0)
                             for rw in rows]
    qc, kc, vc = stack(act_ref, 0), stack(act_ref, W), stack(act_ref, 2 * W)
    beta = stack(rep_ref, 0)
    gc = each(lambda g: _dot_sel(cum, g), stack(rep_ref, H * D))

    def pair_diff(g):
        p0, p1, p2 = (p.astype(F32) for p in _split3(g))
        u_m = jnp.where(lane_r == 0, p0, jnp.where(lane_r == 1, p1, jnp.where(lane_r == 2, p2,
                        jnp.where(lane_r < 6, 1.0, 0.0))))
        w_m = jnp.where(lane_r < 3, 1.0, jnp.where(lane_r == 3, -p0, jnp.where(lane_r == 4, -p1,
                        jnp.where(lane_r == 5, -p2, 0.0))))
        return _nt(u_m.astype(BF16), w_m.astype(BF16))

    decay = each(lambda g: jnp.where(incl, jnp.exp(jnp.where(incl, pair_diff(g), 0.0)), 0.0), gc)
    kb = each(lambda k, b: k * b, kc, beta)
    x = each(lambda a, k, d: -jnp.where(strict, _dot(a, k, P["kk"], _NT) * d, 0.0), kb, kc, decay)
    sq = lambda m: each(lambda a: _dot(a, a, P["inv"]), m)
    mul = lambda ma, mb: each(lambda a, b: _dot(a, b, P["inv"]), ma, mb)
    add = lambda ma, mb: each(lambda a, b: a + b, ma, mb)
    plus_eye = lambda m: each(lambda a: eye + a, m)
    x2 = sq(x)
    x4 = sq(x2)
    f1 = plus_eye(x)
    m1 = add(f1, mul(f1, x2))
    x8 = sq(x4)
    f4 = plus_eye(x4)
    m2 = add(f4, mul(f4, x8))
    x16 = sq(x8)
    m12 = mul(m1, m2)
    x32 = sq(x16)
    f16 = plus_eye(x16)
    m3 = add(f16, mul(f16, x32))
    tinv = mul(m12, m3)
    eg = each(jnp.exp, gc)
    uw = each(lambda t, v, b, k, e: _dot(t, jnp.concatenate([v * b, k * e], axis=1), P["uw"]),
              tinv, vc, beta, kb, eg)
    a_intra = each(lambda q, k, d: jnp.where(incl, _dot(q, k, P["qk"], _NT) * d, 0.0), qc, kc, decay)
    glast = [[g[h * C + C - 1:h * C + C, :] for h in range(H)] for g in gc]
    q_dec = each(lambda q, e: q * e, qc, eg)
    k_dec = each(lambda k, g, gl: k * jnp.exp(
        jnp.concatenate([jnp.broadcast_to(t, (C, D)) for t in gl], axis=0) - g), kc, gc, glast)

    hs = [slice(h * C, (h + 1) * C) for h in range(H)]
    for ci in chunks:
        state = [state_ref[h] for h in range(H)]
        wq = [_dot(jnp.concatenate([uw[ci][hs[h], D:2 * D], q_dec[ci][hs[h]]], axis=0), state[h], P["state"])
              for h in range(H)]
        vn = [uw[ci][hs[h], 0:D] - wq[h][0:C] for h in range(H)]
        for h in range(H):
            state_ref[h] = state[h] * jnp.exp(glast[ci][h]) + _dot(k_dec[ci][hs[h]], vn[h], P["state"], _TN)
        o = jnp.concatenate([wq[h][C:2 * C] for h in range(H)], axis=0) + _dot(
            a_intra[ci], jnp.concatenate(vn, axis=0), P["intra"])
        on = o * lax.rsqrt(jnp.mean(o * o, axis=-1, keepdims=True) + EPS) * nw_ref[...]
        for h in range(H):
            cols = slice(h * D, (h + 1) * D)
            o_ref[0, rows[ci], cols] = (on[hs[h]] * _silu(z_ref[0, rows[ci], cols])).astype(o_ref.dtype)


def _deltanet(a_log, dt_bias, qkv, z, gates, conv_w, dn_norm_w, tb):
    b, s, _ = qkv.shape
    assert s % tb == 0 and tb % DN_CHUNK == 0
    blk = lambda bi, t: (bi, t, 0)
    fixed = lambda bi, t: (0, 0)
    return pl.pallas_call(
        functools.partial(_deltanet_kernel, tb=tb),
        grid=(b, s // tb),
        in_specs=[
            pl.BlockSpec(memory_space=pltpu.SMEM),
            pl.BlockSpec(memory_space=pltpu.SMEM),
            pl.BlockSpec((1, tb, 3 * DN_WIDTH), blk),
            pl.BlockSpec((1, tb, DN_WIDTH), blk),
            pl.BlockSpec((1, tb, GATE_PAD), blk),
            pl.BlockSpec((DN_CONV_WIDTH, 3 * DN_WIDTH), fixed),
            pl.BlockSpec((1, DN_HEAD), fixed),
        ],
        out_specs=pl.BlockSpec((1, tb, DN_WIDTH), blk),
        out_shape=jax.ShapeDtypeStruct((b, s, DN_WIDTH), BF16),
        scratch_shapes=[
            pltpu.VMEM((DN_HEADS, DN_HEAD, DN_HEAD), F32),
            pltpu.VMEM((tb + 8, 3 * DN_WIDTH), F32),
            pltpu.VMEM((tb, 3 * DN_WIDTH), F32),
            pltpu.VMEM((tb, 2 * DN_HEADS * DN_HEAD), F32),
        ],
        compiler_params=pltpu.CompilerParams(
            dimension_semantics=("arbitrary", "arbitrary"), vmem_limit_bytes=VMEM_LIMIT),
        name="deltanet",
    )(a_log, dt_bias, qkv, z, gates, conv_w, dn_norm_w)


def _outproj_kernel(x_ref, ya_ref, yd_ref, wa_ref, wd_ref, o_ref):
    o_ref[...] = x_ref[...] + _nn(ya_ref[...], wa_ref[...]) + _nn(yd_ref[...], wd_ref[...])


def _outproj(x2, ya, yd, wa, wd, tm):
    n = x2.shape[0]
    row = lambda i: (i, 0)
    fixed = lambda i: (0, 0)
    return pl.pallas_call(
        _outproj_kernel,
        grid=(n // tm,),
        in_specs=[
            pl.BlockSpec((tm, D_MODEL), row),
            pl.BlockSpec((tm, ATTN_WIDTH), row),
            pl.BlockSpec((tm, DN_WIDTH), row),
            pl.BlockSpec((ATTN_WIDTH, D_MODEL), fixed),
            pl.BlockSpec((DN_WIDTH, D_MODEL), fixed),
        ],
        out_specs=pl.BlockSpec((tm, D_MODEL), row),
        out_shape=jax.ShapeDtypeStruct((n, D_MODEL), F32),
        compiler_params=pltpu.CompilerParams(dimension_semantics=("arbitrary",), vmem_limit_bytes=VMEM_LIMIT),
        name="outproj",
    )(x2, ya, yd, wa, wd)


def _row_tile(n):
    for tm in (512, 256, 128, 64, 32, 16, 8):
        if n % tm == 0:
            return tm
    raise ValueError(f"row count {n} is not a multiple of 8")


def kernel(x, rel_bias, norm_w, w_in, q_norm_w, k_norm_w, conv_w, a_log, dt_bias, dn_norm_w, w_out):
    b, s, d = x.shape
    assert d == D_MODEL and norm_w.shape[0] == 1, "single-layer kernel"
    n = b * s
    x2 = x.reshape(n, d)
    tm = _row_tile(n)

    w_main = w_in[0][:, :MAIN_COLS].astype(BF16)
    w_gate = jnp.pad(w_in[0][:, MAIN_COLS:], ((0, 0), (0, GATE_PAD - 2 * DN_HEADS))).astype(BF16)
    qnw = jnp.tile(q_norm_w[0], ATTN_HEADS)[None, :]
    knw = jnp.tile(k_norm_w[0], ATTN_HEADS)[None, :]

    q, k, v, za, qkv, zd, gates = _inproj(x2, norm_w, w_main, w_gate, qnw, knw, tm)

    r3 = lambda t: t.reshape(b, s, t.shape[-1])
    ya = _moba(rel_bias, r3(q), r3(k), r3(v), r3(za))
    tb = 256 if s % 256 == 0 else DN_CHUNK
    yd = _deltanet(a_log[0], dt_bias[0], r3(qkv), r3(zd), r3(gates), conv_w[0], dn_norm_w, tb)

    w_o = w_out[0].astype(BF16)
    out = _outproj(x2, ya.reshape(n, ATTN_WIDTH), yd.reshape(n, DN_WIDTH), w_o[:ATTN_WIDTH], w_o[ATTN_WIDTH:], tm)
    return out.reshape(b, s, d)
```

```python
import functools
import math

import jax
import jax.numpy as jnp
from jax import lax
from jax.experimental import pallas as pl
from jax.experimental.pallas import tpu as pltpu

F32 = jnp.float32
BF16 = jnp.bfloat16
HI = lax.Precision.HIGHEST

D_MODEL = 1024
ATTN_HEADS = 8
ATTN_HEAD_DIM = 64
ATTN_WIDTH = ATTN_HEADS * ATTN_HEAD_DIM
MOBA_BLOCK = 256
MOBA_TOPK = 3
REL_BUCKETS = 32
REL_MAX_DISTANCE = 128
DN_HEADS = 4
DN_HEAD = 128
DN_WIDTH = DN_HEADS * DN_HEAD
DN_CONV_WIDTH = 4
DN_CHUNK = 64
MAIN_COLS = 4 * ATTN_WIDTH + 3 * DN_WIDTH + DN_WIDTH
GATE_PAD = 128
EPS = 1e-6
LOG2E = math.log2(math.e)
NEG_BIG = -32768.0
VT_ROWS = ATTN_HEAD_DIM + 16
VMEM_LIMIT = 56 * 1024 * 1024


def _nt(a, b, precision=None):
    return lax.dot_general(a, b, (((1,), (1,)), ((), ())), preferred_element_type=F32, precision=precision)


def _nn(a, b, precision=None):
    return lax.dot_general(a, b, (((1,), (0,)), ((), ())), preferred_element_type=F32, precision=precision)


def _silu(x):
    half = 0.5 * x
    return half + half * jnp.tanh(half)


def _inproj_kernel(x_ref, nw_ref, w_ref, wg_ref, qnw_ref, knw_ref,
                   q_ref, k_ref, v_ref, za_ref, qkv_ref, zd_ref, g_ref):
    x = x_ref[...]
    ms = jnp.mean(x * x, axis=-1, keepdims=True)
    h = (x * lax.rsqrt(ms + EPS) * nw_ref[...]).astype(BF16)

    r_i = lax.broadcasted_iota(jnp.int32, (ATTN_WIDTH, 128), 0)
    c_i = lax.broadcasted_iota(jnp.int32, (ATTN_WIDTH, 128), 1)
    ind = jnp.where(r_i // ATTN_HEAD_DIM == c_i, 1.0, 0.0).astype(BF16)
    r_e = lax.broadcasted_iota(jnp.int32, (128, ATTN_WIDTH), 0)
    c_e = lax.broadcasted_iota(jnp.int32, (128, ATTN_WIDTH), 1)
    expand = jnp.where(c_e // ATTN_HEAD_DIM == r_e, 1.0, 0.0).astype(BF16)

    def head_rms(t, w):
        s0, s1 = _split2(t * t)
        ss = _nn(s0, ind) + _nn(s1, ind)
        r = lax.rsqrt(ss * (1.0 / ATTN_HEAD_DIM) + EPS)
        r0, r1 = _split2(r)
        rf = _nn(r0, expand) + _nn(r1, expand)
        return t * rf * w

    def proj(c0, width):
        return _nn(h, w_ref[:, c0:c0 + width])

    q_ref[...] = head_rms(proj(0, ATTN_WIDTH), qnw_ref[...]).astype(q_ref.dtype)
    k_ref[...] = head_rms(proj(ATTN_WIDTH, ATTN_WIDTH), knw_ref[...]).astype(k_ref.dtype)
    v_ref[...] = proj(2 * ATTN_WIDTH, ATTN_WIDTH).astype(v_ref.dtype)
    za_ref[...] = proj(3 * ATTN_WIDTH, ATTN_WIDTH).astype(za_ref.dtype)
    for c in range(3):
        qkv_ref[:, c * DN_WIDTH:(c + 1) * DN_WIDTH] = proj(4 * ATTN_WIDTH + c * DN_WIDTH, DN_WIDTH).astype(qkv_ref.dtype)
    zd_ref[...] = proj(4 * ATTN_WIDTH + 3 * DN_WIDTH, DN_WIDTH).astype(zd_ref.dtype)
    g_ref[...] = _nn(h, wg_ref[...])


def _inproj(x2, norm_w, w_main, w_gate, qnw, knw, tm):
    n = x2.shape[0]
    row = lambda i: (i, 0)
    fixed = lambda i: (0, 0)
    outs = [
        jax.ShapeDtypeStruct((n, ATTN_WIDTH), F32),
        jax.ShapeDtypeStruct((n, ATTN_WIDTH), F32),
        jax.ShapeDtypeStruct((n, ATTN_WIDTH), BF16),
        jax.ShapeDtypeStruct((n, ATTN_WIDTH), F32),
        jax.ShapeDtypeStruct((n, 3 * DN_WIDTH), F32),
        jax.ShapeDtypeStruct((n, DN_WIDTH), F32),
        jax.ShapeDtypeStruct((n, GATE_PAD), F32),
    ]
    return pl.pallas_call(
        _inproj_kernel,
        grid=(n // tm,),
        in_specs=[
            pl.BlockSpec((tm, D_MODEL), row),
            pl.BlockSpec((1, D_MODEL), fixed),
            pl.BlockSpec((D_MODEL, MAIN_COLS), fixed),
            pl.BlockSpec((D_MODEL, GATE_PAD), fixed),
            pl.BlockSpec((1, ATTN_WIDTH), fixed),
            pl.BlockSpec((1, ATTN_WIDTH), fixed),
        ],
        out_specs=[pl.BlockSpec((tm, o.shape[1]), row) for o in outs],
        out_shape=outs,
        compiler_params=pltpu.CompilerParams(dimension_semantics=("arbitrary",), vmem_limit_bytes=VMEM_LIMIT),
        name="inproj",
    )(x2, norm_w, w_main, w_gate, qnw, knw)


def _t5_bias_tile_t(relb_ref, head, offset):
    c = lax.broadcasted_iota(jnp.int32, (MOBA_BLOCK, MOBA_BLOCK), 0)
    r = lax.broadcasted_iota(jnp.int32, (MOBA_BLOCK, MOBA_BLOCK), 1)
    dist = r - c + offset
    n = jnp.maximum(dist, 0)
    max_exact = REL_BUCKETS // 2
    nf = jnp.maximum(n, 1).astype(F32)
    large = max_exact + (jnp.log(nf / max_exact) / math.log(REL_MAX_DISTANCE / max_exact)
                         * (REL_BUCKETS - max_exact)).astype(jnp.int32)
    large = jnp.minimum(large, REL_BUCKETS - 1)
    bucket = jnp.where(n < max_exact, n, large)
    far = relb_ref[REL_BUCKETS - 1, head]
    bias = jnp.zeros((MOBA_BLOCK, MOBA_BLOCK), F32)
    for t in range(REL_BUCKETS):
        bias = jnp.where(bucket == t, relb_ref[t, head] - far, bias)
    return jnp.where(dist >= 0, bias * LOG2E, NEG_BIG)


def _moba_kernel(relb_ref, q_ref, qn_ref, k_ref, v_ref, z_ref, o_ref,
                 kaug_ref, vt_ref, kmean_ref, bias_ref, acc_ref, qa_ref, s_ref, *, nb):
    hp = pl.program_id(1)
    i = pl.program_id(2)
    par = i % 2
    HD = ATTN_HEAD_DIM
    lane = lax.broadcasted_iota(jnp.int32, (MOBA_BLOCK, 128), 1)
    lo_half = lane < HD
    er = lax.broadcasted_iota(jnp.int32, (128, 128), 0)
    ec = lax.broadcasted_iota(jnp.int32, (128, 128), 1)
    eye = jnp.where(er == ec, 1.0, 0.0).astype(BF16)
    blk = lax.broadcasted_iota(jnp.int32, (HD, MOBA_BLOCK), 0)

    def gate_scores(q):
        qt = _nt(eye, (q * (HD ** -0.5 * LOG2E)).astype(BF16))
        q0, q1 = _split2(q)
        gates = []
        for hh in range(2):
            k0, k1 = _split2(kmean_ref[hh])
            gates.append(_nt(k0, q0) + (_nt(k0, q1) + _nt(k1, q0)))
        return qt, gates

    def store_operands(slot, qt, gates, tile):
        for hh in range(2):
            g = jnp.where(blk < tile, gates[hh], -jnp.inf)
            sel = blk == tile
            for _ in range(MOBA_TOPK):
                mx = jnp.max(g, axis=0, keepdims=True)
                first = jnp.min(jnp.where(g == mx, blk, HD), axis=0, keepdims=True)
                hit = (blk == first) & (mx > -jnp.inf)
                sel = sel | hit
                g = jnp.where(hit, -jnp.inf, g)
            mask = jnp.where(sel, 0.0, NEG_BIG)
            rows = [qt[0:HD], mask] if hh == 0 else [mask, qt[HD:2 * HD]]
            qa_ref[slot, hh] = jnp.concatenate(rows, axis=0).astype(BF16)

    @pl.when(i == 0)
    def _prepare():
        kmean_ref[...] = jnp.zeros_like(kmean_ref)
        lane1 = lax.broadcasted_iota(jnp.int32, (1, 128), 1)
        ones_rows = jnp.ones((VT_ROWS - HD, MOBA_BLOCK), BF16)

        def prep(j, carry):
            rows = pl.ds(pl.multiple_of(j * MOBA_BLOCK, MOBA_BLOCK), MOBA_BLOCK)
            kb = k_ref[0, rows, :]
            km = jnp.mean(kb, axis=0, keepdims=True)
            kmean_ref[0, pl.ds(j, 1), :] = jnp.where(lane1 < HD, km, 0.0)
            kmean_ref[1, pl.ds(j, 1), :] = jnp.where(lane1 < HD, 0.0, km)
            kaug_ref[0, j] = jnp.where(lo_half, kb, jnp.where(lane == HD + j, 1.0, 0.0)).astype(BF16)
            kaug_ref[1, j] = jnp.where(lo_half, jnp.where(lane == j, 1.0, 0.0), kb).astype(BF16)
            vt = _nt(eye, v_ref[0, rows, :]).astype(BF16)
            for hh in range(2):
                vt_ref[j, hh] = jnp.concatenate([vt[hh * HD:(hh + 1) * HD], ones_rows], axis=0)
            return carry

        lax.fori_loop(0, nb, prep, 0)
        kaug_ref[0, nb] = jnp.where(lane == 2 * HD - 1, 1.0, 0.0).astype(BF16)
        kaug_ref[1, nb] = jnp.where(lane == HD - 1, 1.0, 0.0).astype(BF16)
        vt_ref[nb] = jnp.zeros((2, VT_ROWS, MOBA_BLOCK), BF16)
        for hh in range(2):
            bias_ref[hh, 0] = _t5_bias_tile_t(relb_ref, 2 * hp + hh, 0)
            bias_ref[hh, 1] = _t5_bias_tile_t(relb_ref, 2 * hp + hh, MOBA_BLOCK)
        store_operands(0, *gate_scores(q_ref[0]), 0)

    def issue(j, buf_ref):
        for hh in range(2):
            buf_ref[hh] = _nn(kaug_ref[hh, j], qa_ref[par, hh])

    def consume(j, buf_ref, st, bias_idx=None, first=False):
        out = []
        for hh in range(2):
            s = buf_ref[hh]
            if bias_idx is not None:
                s = s + bias_ref[hh, bias_idx]
            m_cur = jnp.max(s, axis=0, keepdims=True)
            m_new = m_cur if first else jnp.maximum(st[hh], m_cur)
            p = jnp.exp2((s - m_new).astype(BF16))
            pv = _nn(vt_ref[j, hh], p)
            if first:
                acc_ref[hh] = pv
            else:
                acc_ref[hh] = jnp.exp2(st[hh] - m_new) * acc_ref[hh] + pv
            out.append(m_new)
        return tuple(out)

    nfar = jnp.maximum(i - 1, 0)
    far = lambda j: jnp.where(j < nfar, j, nb)
    a0_ref, b0_ref, a1_ref, b1_ref = (s_ref.at[n] for n in range(4))
    qt_next, gates_next = gate_scores(qn_ref[0])
    issue(i, a1_ref)
    issue(jnp.where(i >= 1, i - 1, nb), b1_ref)
    issue(far(0), a0_ref)
    issue(far(1), b0_ref)
    store_operands(1 - par, qt_next, gates_next, i + 1)
    st = consume(i, a1_ref, None, bias_idx=0, first=True)
    st = consume(jnp.where(i >= 1, i - 1, nb), b1_ref, st, bias_idx=1)

    def far_quad(j, st):
        issue(far(j + 2), a1_ref)
        issue(far(j + 3), b1_ref)
        st = consume(far(j), a0_ref, st)
        issue(far(j + 4), a0_ref)
        st = consume(far(j + 1), b0_ref, st)
        issue(far(j + 5), b0_ref)
        st = consume(far(j + 2), a1_ref, st)
        return consume(far(j + 3), b1_ref, st)

    n8 = nfar // 8
    st = lax.fori_loop(0, n8, lambda u, st: far_quad(8 * u + 4, far_quad(8 * u, st)), st)
    n4 = (nfar - 8 * n8) // 4
    st = lax.fori_loop(0, n4, lambda u, st: far_quad(8 * n8 + 4 * u, st), st)
    base = 8 * n8 + 4 * n4
    rem = nfar - base

    def rem_first(st):
        issue(far(base + 2), a1_ref)
        st = consume(far(base), a0_ref, st)
        return consume(far(base + 1), b0_ref, st)

    st = lax.cond(rem >= 1, rem_first, lambda st: st, st)
    lax.cond(rem >= 3, lambda st: consume(far(base + 2), a1_ref, st), lambda st: st, st)

    ot = jnp.concatenate([acc_ref[hh, 0:HD, :] / acc_ref[hh, HD:HD + 1, :] for hh in range(2)], axis=0)
    o_ref[0] = (ot.T * _silu(z_ref[0])).astype(o_ref.dtype)


def _moba(rel_bias, q, k, v, z):
    b, s, _ = q.shape
    nb = s // MOBA_BLOCK
    assert s % MOBA_BLOCK == 0 and nb < ATTN_HEAD_DIM
    pairs = ATTN_HEADS // 2
    blk = lambda bi, hp, i: (bi, i, hp)
    whole = lambda bi, hp, i: (bi, 0, hp)
    return pl.pallas_call(
        functools.partial(_moba_kernel, nb=nb),
        grid=(b, pairs, nb),
        in_specs=[
            pl.BlockSpec(memory_space=pltpu.SMEM),
            pl.BlockSpec((1, MOBA_BLOCK, 128), blk),
            pl.BlockSpec((1, MOBA_BLOCK, 128), lambda bi, hp, i: (bi, jnp.minimum(i + 1, nb - 1), hp)),
            pl.BlockSpec((1, s, 128), whole),
            pl.BlockSpec((1, s, 128), whole),
            pl.BlockSpec((1, MOBA_BLOCK, 128), blk),
        ],
        out_specs=pl.BlockSpec((1, MOBA_BLOCK, 128), blk),
        out_shape=jax.ShapeDtypeStruct((b, s, ATTN_WIDTH), BF16),
        scratch_shapes=[
            pltpu.VMEM((2, nb + 1, MOBA_BLOCK, 128), BF16),
            pltpu.VMEM((nb + 1, 2, VT_ROWS, MOBA_BLOCK), BF16),
            pltpu.VMEM((2, ATTN_HEAD_DIM, 128), F32),
            pltpu.VMEM((2, 2, MOBA_BLOCK, MOBA_BLOCK), F32),
            pltpu.VMEM((2, VT_ROWS, MOBA_BLOCK), F32),
            pltpu.VMEM((2, 2, 128, MOBA_BLOCK), BF16),
            pltpu.VMEM((4, 2, MOBA_BLOCK, MOBA_BLOCK), F32),
        ],
        compiler_params=pltpu.CompilerParams(
            dimension_semantics=("arbitrary", "arbitrary", "arbitrary"), vmem_limit_bytes=VMEM_LIMIT),
        name="moba",
    )(rel_bias, q, q, k, v, z)


DN_ROWS = DN_HEADS * DN_CHUNK
DN_PREC = dict(kk="bf16", qk="bf16", inv="bf16", uw="bf16", state="bf16", intra="bf16")


def _split2(a):
    a0 = a.astype(BF16)
    return a0, (a - a0.astype(F32)).astype(BF16)


def _split3(a):
    a0 = a.astype(BF16)
    r1 = a - a0.astype(F32)
    a1 = r1.astype(BF16)
    return a0, a1, (r1 - a1.astype(F32)).astype(BF16)


def _dot(a, b, prec, dims=(((1,), (0,)), ((), ()))):
    f = lambda x, y: lax.dot_general(x, y, dims, preferred_element_type=F32)
    if prec == "bf16":
        return f(a.astype(BF16), b.astype(BF16))
    assert prec == "x3"
    a0, a1 = _split2(a)
    b0, b1 = _split2(b)
    return f(a0, b0) + (f(a0, b1) + f(a1, b0))


_NT = (((1,), (1,)), ((), ()))
_TN = (((0,), (0,)), ((), ()))


def _dot_sel(sel, b):
    return sum(_nn(sel, p) for p in _split3(b))


def _deltanet_kernel(alog_ref, dtb_ref, x_ref, z_ref, g_ref, cw_ref, nw_ref, o_ref,
                     state_ref, pad_ref, act_ref, rep_ref, *, tb):
    t = pl.program_id(1)
    C, R, W, H, D = DN_CHUNK, DN_ROWS, DN_WIDTH, DN_HEADS, DN_HEAD
    hist = DN_CONV_WIDTH - 1

    @pl.when(t == 0)
    def _reset():
        state_ref[...] = jnp.zeros_like(state_ref)
        pad_ref[0:8, :] = jnp.zeros((8, 3 * W), F32)

    pad_ref[8:8 + tb, :] = x_ref[0]
    for gcol in range(3 * H):
        cols = slice(gcol * D, (gcol + 1) * D)
        acc = pad_ref[pl.ds(8 - hist, tb), cols] * cw_ref[0:1, cols]
        for w in range(1, DN_CONV_WIDTH):
            acc = acc + pad_ref[pl.ds(8 - hist + w, tb), cols] * cw_ref[w:w + 1, cols]
        a = _silu(acc)
        if gcol < 2 * H:
            a = a * lax.rsqrt(jnp.sum(a * a, axis=-1, keepdims=True) + EPS)
            if gcol < H:
                a = a * (D ** -0.5)
        act_ref[:, cols] = a
    pad_ref[0:8, :] = x_ref[0, tb - 8:tb, :]

    lane1 = lax.broadcasted_iota(jnp.int32, (1, GATE_PAD), 1)
    alog_row = jnp.zeros((1, GATE_PAD), F32)
    dtb_row = jnp.zeros((1, GATE_PAD), F32)
    for h in range(H):
        alog_row = jnp.where(lane1 == H + h, alog_ref[h], alog_row)
        dtb_row = jnp.where(lane1 == H + h, dtb_ref[h], dtb_row)
    gates = g_ref[0]
    xs = gates + dtb_row
    softplus = jnp.maximum(xs, 0.0) + jnp.log(1.0 + jnp.exp(-jnp.abs(xs)))
    gval = jnp.where(lane1 < H, 1.0 / (1.0 + jnp.exp(-gates)), -jnp.exp(alog_row) * softplus)
    er = lax.broadcasted_iota(jnp.int32, (GATE_PAD, 2 * H * D), 0)
    ec = lax.broadcasted_iota(jnp.int32, (GATE_PAD, 2 * H * D), 1)
    spread = jnp.where(er == ec // D, 1.0, 0.0).astype(BF16)
    rep_ref[...] = sum(_nn(p, spread) for p in _split3(gval))

    r = lax.broadcasted_iota(jnp.int32, (R, R), 0)
    c = lax.broadcasted_iota(jnp.int32, (R, R), 1)
    same = (r // C) == (c // C)
    incl = same & (r >= c)
    strict = same & (r > c)
    eye = jnp.where(r == c, 1.0, 0.0)
    cum = jnp.where(incl, 1.0, 0.0).astype(BF16)
    lane_r = lax.broadcasted_iota(jnp.int32, (R, D), 1)
    P = DN_PREC

    chunks = range(tb // C)
    each = lambda f, *seqs: [f(*a) for a in zip(*seqs)]
    rows = [slice(ci * C, (ci + 1) * C) for ci in chunks]
    stack = lambda ref, c0: [jnp.concatenate([ref[rw, c0 + D * h:c0 + D * (h + 1)] for h in range(H)], axis=0)
                             for rw in rows]
    qc, kc, vc = stack(act_ref, 0), stack(act_ref, W), stack(act_ref, 2 * W)
    beta = stack(rep_ref, 0)
    gc = each(lambda g: _dot_sel(cum, g), stack(rep_ref, H * D))

    def pair_diff(g):
        p0, p1, p2 = (p.astype(F32) for p in _split3(g))
        u_m = jnp.where(lane_r == 0, p0, jnp.where(lane_r == 1, p1, jnp.where(lane_r == 2, p2,
                        jnp.where(lane_r < 6, 1.0, 0.0))))
        w_m = jnp.where(lane_r < 3, 1.0, jnp.where(lane_r == 3, -p0, jnp.where(lane_r == 4, -p1,
                        jnp.where(lane_r == 5, -p2, 0.0))))
        return _nt(u_m.astype(BF16), w_m.astype(BF16))

    decay = each(lambda g: jnp.where(incl, jnp.exp(jnp.where(incl, pair_diff(g), 0.0)), 0.0), gc)
    kb = each(lambda k, b: k * b, kc, beta)
    x = each(lambda a, k, d: -jnp.where(strict, _dot(a, k, P["kk"], _NT) * d, 0.0), kb, kc, decay)
    sq = lambda m: each(lambda a: _dot(a, a, P["inv"]), m)
    mul = lambda ma, mb: each(lambda a, b: _dot(a, b, P["inv"]), ma, mb)
    add = lambda ma, mb: each(lambda a, b: a + b, ma, mb)
    plus_eye = lambda m: each(lambda a: eye + a, m)
    x2 = sq(x)
    x4 = sq(x2)
    f1 = plus_eye(x)
    m1 = add(f1, mul(f1, x2))
    x8 = sq(x4)
    f4 = plus_eye(x4)
    m2 = add(f4, mul(f4, x8))
    x16 = sq(x8)
    m12 = mul(m1, m2)
    x32 = sq(x16)
    f16 = plus_eye(x16)
    m3 = add(f16, mul(f16, x32))
    tinv = mul(m12, m3)
    eg = each(jnp.exp, gc)
    uw = each(lambda t, v, b, k, e: _dot(t, jnp.concatenate([v * b, k * e], axis=1), P["uw"]),
              tinv, vc, beta, kb, eg)
    a_intra = each(lambda q, k, d: jnp.where(incl, _dot(q, k, P["qk"], _NT) * d, 0.0), qc, kc, decay)
    glast = [[g[h * C + C - 1:h * C + C, :] for h in range(H)] for g in gc]
    q_dec = each(lambda q, e: q * e, qc, eg)
    k_dec = each(lambda k, g, gl: k * jnp.exp(
        jnp.concatenate([jnp.broadcast_to(t, (C, D)) for t in gl], axis=0) - g), kc, gc, glast)

    hs = [slice(h * C, (h + 1) * C) for h in range(H)]
    for ci in chunks:
        state = [state_ref[h] for h in range(H)]
        wq = [_dot(jnp.concatenate([uw[ci][hs[h], D:2 * D], q_dec[ci][hs[h]]], axis=0), state[h], P["state"])
              for h in range(H)]
        vn = [uw[ci][hs[h], 0:D] - wq[h][0:C] for h in range(H)]
        for h in range(H):
            state_ref[h] = state[h] * jnp.exp(glast[ci][h]) + _dot(k_dec[ci][hs[h]], vn[h], P["state"], _TN)
        o = jnp.concatenate([wq[h][C:2 * C] for h in range(H)], axis=0) + _dot(
            a_intra[ci], jnp.concatenate(vn, axis=0), P["intra"])
        on = o * lax.rsqrt(jnp.mean(o * o, axis=-1, keepdims=True) + EPS) * nw_ref[...]
        for h in range(H):
            cols = slice(h * D, (h + 1) * D)
            o_ref[0, rows[ci], cols] = (on[hs[h]] * _silu(z_ref[0, rows[ci], cols])).astype(o_ref.dtype)


def _deltanet(a_log, dt_bias, qkv, z, gates, conv_w, dn_norm_w, tb):
    b, s, _ = qkv.shape
    assert s % tb == 0 and tb % DN_CHUNK == 0
    blk = lambda bi, t: (bi, t, 0)
    fixed = lambda bi, t: (0, 0)
    return pl.pallas_call(
        functools.partial(_deltanet_kernel, tb=tb),
        grid=(b, s // tb),
        in_specs=[
            pl.BlockSpec(memory_space=pltpu.SMEM),
            pl.BlockSpec(memory_space=pltpu.SMEM),
            pl.BlockSpec((1, tb, 3 * DN_WIDTH), blk),
            pl.BlockSpec((1, tb, DN_WIDTH), blk),
            pl.BlockSpec((1, tb, GATE_PAD), blk),
            pl.BlockSpec((DN_CONV_WIDTH, 3 * DN_WIDTH), fixed),
            pl.BlockSpec((1, DN_HEAD), fixed),
        ],
        out_specs=pl.BlockSpec((1, tb, DN_WIDTH), blk),
        out_shape=jax.ShapeDtypeStruct((b, s, DN_WIDTH), BF16),
        scratch_shapes=[
            pltpu.VMEM((DN_HEADS, DN_HEAD, DN_HEAD), F32),
            pltpu.VMEM((tb + 8, 3 * DN_WIDTH), F32),
            pltpu.VMEM((tb, 3 * DN_WIDTH), F32),
            pltpu.VMEM((tb, 2 * DN_HEADS * DN_HEAD), F32),
        ],
        compiler_params=pltpu.CompilerParams(
            dimension_semantics=("arbitrary", "arbitrary"), vmem_limit_bytes=VMEM_LIMIT),
        name="deltanet",
    )(a_log, dt_bias, qkv, z, gates, conv_w, dn_norm_w)


def _outproj_kernel(x_ref, ya_ref, yd_ref, wa_ref, wd_ref, o_ref):
    o_ref[...] = x_ref[...] + _nn(ya_ref[...], wa_ref[...]) + _nn(yd_ref[...], wd_ref[...])


def _outproj(x2, ya, yd, wa, wd, tm):
    n = x2.shape[0]
    row = lambda i: (i, 0)
    fixed = lambda i: (0, 0)
    return pl.pallas_call(
        _outproj_kernel,
        grid=(n // tm,),
        in_specs=[
            pl.BlockSpec((tm, D_MODEL), row),
            pl.BlockSpec((tm, ATTN_WIDTH), row),
            pl.BlockSpec((tm, DN_WIDTH), row),
            pl.BlockSpec((ATTN_WIDTH, D_MODEL), fixed),
            pl.BlockSpec((DN_WIDTH, D_MODEL), fixed),
        ],
        out_specs=pl.BlockSpec((tm, D_MODEL), row),
        out_shape=jax.ShapeDtypeStruct((n, D_MODEL), F32),
        compiler_params=pltpu.CompilerParams(dimension_semantics=("arbitrary",), vmem_limit_bytes=VMEM_LIMIT),
        name="outproj",
    )(x2, ya, yd, wa, wd)


def _row_tile(n):
    for tm in (512, 256, 128, 64, 32, 16, 8):
        if n % tm == 0:
            return tm
    raise ValueError(f"row count {n} is not a multiple of 8")


def kernel(x, rel_bias, norm_w, w_in, q_norm_w, k_norm_w, conv_w, a_log, dt_bias, dn_norm_w, w_out):
    b, s, d = x.shape
    assert d == D_MODEL and norm_w.shape[0] == 1, "single-layer kernel"
    n = b * s
    x2 = x.reshape(n, d)
    tm = _row_tile(n)

    w_main = w_in[0][:, :MAIN_COLS].astype(BF16)
    w_gate = jnp.pad(w_in[0][:, MAIN_COLS:], ((0, 0), (0, GATE_PAD - 2 * DN_HEADS))).astype(BF16)
    qnw = jnp.tile(q_norm_w[0], ATTN_HEADS)[None, :]
    knw = jnp.tile(k_norm_w[0], ATTN_HEADS)[None, :]

    q, k, v, za, qkv, zd, gates = _inproj(x2, norm_w, w_main, w_gate, qnw, knw, tm)

    r3 = lambda t: t.reshape(b, s, t.shape[-1])
    ya = _moba(rel_bias, r3(q), r3(k), r3(v), r3(za))
    tb = 256 if s % 256 == 0 else DN_CHUNK
    yd = _deltanet(a_log[0], dt_bias[0], r3(qkv), r3(zd), r3(gates), conv_w[0], dn_norm_w, tb)

    w_o = w_out[0].astype(BF16)
    out = _outproj(x2, ya.reshape(n, ATTN_WIDTH), yd.reshape(n, DN_WIDTH), w_o[:ATTN_WIDTH], w_o[ATTN_WIDTH:], tm)
    return out.reshape(b, s, d)
```

```python
import functools
import math

import jax
import jax.numpy as jnp
from jax import lax
from jax.experimental import pallas as pl
from jax.experimental.pallas import tpu as pltpu

F32 = jnp.float32
BF16 = jnp.bfloat16
HI = lax.Precision.HIGHEST

D_MODEL = 1024
ATTN_HEADS = 8
ATTN_HEAD_DIM = 64
ATTN_WIDTH = ATTN_HEADS * ATTN_HEAD_DIM
MOBA_BLOCK = 256
MOBA_TOPK = 3
REL_BUCKETS = 32
REL_MAX_DISTANCE = 128
DN_HEADS = 4
DN_HEAD = 128
DN_WIDTH = DN_HEADS * DN_HEAD
DN_CONV_WIDTH = 4
DN_CHUNK = 64
MAIN_COLS = 4 * ATTN_WIDTH + 3 * DN_WIDTH + DN_WIDTH
GATE_PAD = 128
EPS = 1e-6
LOG2E = math.log2(math.e)
NEG_BIG = -32768.0
VT_ROWS = ATTN_HEAD_DIM + 16
VMEM_LIMIT = 56 * 1024 * 1024


def _nt(a, b, precision=None):
    return lax.dot_general(a, b, (((1,), (1,)), ((), ())), preferred_element_type=F32, precision=precision)


def _nn(a, b, precision=None):
    return lax.dot_general(a, b, (((1,), (0,)), ((), ())), preferred_element_type=F32, precision=precision)


def _silu(x):
    half = 0.5 * x
    return half + half * jnp.tanh(half)


def _inproj_kernel(x_ref, nw_ref, w_ref, wg_ref, qnw_ref, knw_ref,
                   q_ref, k_ref, v_ref, za_ref, qkv_ref, zd_ref, g_ref):
    x = x_ref[...]
    ms = jnp.mean(x * x, axis=-1, keepdims=True)
    h = (x * lax.rsqrt(ms + EPS) * nw_ref[...]).astype(BF16)

    r_i = lax.broadcasted_iota(jnp.int32, (ATTN_WIDTH, 128), 0)
    c_i = lax.broadcasted_iota(jnp.int32, (ATTN_WIDTH, 128), 1)
    ind = jnp.where(r_i // ATTN_HEAD_DIM == c_i, 1.0, 0.0).astype(BF16)
    r_e = lax.broadcasted_iota(jnp.int32, (128, ATTN_WIDTH), 0)
    c_e = lax.broadcasted_iota(jnp.int32, (128, ATTN_WIDTH), 1)
    expand = jnp.where(c_e // ATTN_HEAD_DIM == r_e, 1.0, 0.0).astype(BF16)

    def head_rms(t, w):
        ss = _nn((t * t).astype(BF16), ind)
        r = lax.rsqrt(ss * (1.0 / ATTN_HEAD_DIM) + EPS)
        r0, r1 = _split2(r)
        rf = _nn(r0, expand) + _nn(r1, expand)
        return t * rf * w

    def proj(c0, width):
        return _nn(h, w_ref[:, c0:c0 + width])

    q_ref[...] = head_rms(proj(0, ATTN_WIDTH), qnw_ref[...]).astype(q_ref.dtype)
    k_ref[...] = head_rms(proj(ATTN_WIDTH, ATTN_WIDTH), knw_ref[...]).astype(k_ref.dtype)
    v_ref[...] = proj(2 * ATTN_WIDTH, ATTN_WIDTH).astype(v_ref.dtype)
    za_ref[...] = proj(3 * ATTN_WIDTH, ATTN_WIDTH).astype(za_ref.dtype)
    for c in range(3):
        qkv_ref[:, c * DN_WIDTH:(c + 1) * DN_WIDTH] = proj(4 * ATTN_WIDTH + c * DN_WIDTH, DN_WIDTH).astype(qkv_ref.dtype)
    zd_ref[...] = proj(4 * ATTN_WIDTH + 3 * DN_WIDTH, DN_WIDTH).astype(zd_ref.dtype)
    g_ref[...] = _nn(h, wg_ref[...])


def _inproj(x2, norm_w, w_main, w_gate, qnw, knw, tm):
    n = x2.shape[0]
    row = lambda i: (i, 0)
    fixed = lambda i: (0, 0)
    outs = [
        jax.ShapeDtypeStruct((n, ATTN_WIDTH), F32),
        jax.ShapeDtypeStruct((n, ATTN_WIDTH), F32),
        jax.ShapeDtypeStruct((n, ATTN_WIDTH), BF16),
        jax.ShapeDtypeStruct((n, ATTN_WIDTH), F32),
        jax.ShapeDtypeStruct((n, 3 * DN_WIDTH), F32),
        jax.ShapeDtypeStruct((n, DN_WIDTH), F32),
        jax.ShapeDtypeStruct((n, GATE_PAD), F32),
    ]
    return pl.pallas_call(
        _inproj_kernel,
        grid=(n // tm,),
        in_specs=[
            pl.BlockSpec((tm, D_MODEL), row),
            pl.BlockSpec((1, D_MODEL), fixed),
            pl.BlockSpec((D_MODEL, MAIN_COLS), fixed),
            pl.BlockSpec((D_MODEL, GATE_PAD), fixed),
            pl.BlockSpec((1, ATTN_WIDTH), fixed),
            pl.BlockSpec((1, ATTN_WIDTH), fixed),
        ],
        out_specs=[pl.BlockSpec((tm, o.shape[1]), row) for o in outs],
        out_shape=outs,
        compiler_params=pltpu.CompilerParams(dimension_semantics=("arbitrary",), vmem_limit_bytes=VMEM_LIMIT),
        name="inproj",
    )(x2, norm_w, w_main, w_gate, qnw, knw)


def _t5_bias_tile_t(relb_ref, head, offset):
    c = lax.broadcasted_iota(jnp.int32, (MOBA_BLOCK, MOBA_BLOCK), 0)
    r = lax.broadcasted_iota(jnp.int32, (MOBA_BLOCK, MOBA_BLOCK), 1)
    dist = r - c + offset
    n = jnp.maximum(dist, 0)
    max_exact = REL_BUCKETS // 2
    nf = jnp.maximum(n, 1).astype(F32)
    large = max_exact + (jnp.log(nf / max_exact) / math.log(REL_MAX_DISTANCE / max_exact)
                         * (REL_BUCKETS - max_exact)).astype(jnp.int32)
    large = jnp.minimum(large, REL_BUCKETS - 1)
    bucket = jnp.where(n < max_exact, n, large)
    far = relb_ref[REL_BUCKETS - 1, head]
    bias = jnp.zeros((MOBA_BLOCK, MOBA_BLOCK), F32)
    for t in range(REL_BUCKETS):
        bias = jnp.where(bucket == t, relb_ref[t, head] - far, bias)
    return jnp.where(dist >= 0, bias * LOG2E, NEG_BIG)


def _moba_kernel(relb_ref, q_ref, qn_ref, k_ref, v_ref, z_ref, o_ref,
                 kaug_ref, vt_ref, kmean_ref, bias_ref, acc_ref, qa_ref, s_ref, *, nb):
    hp = pl.program_id(0)
    bi = pl.program_id(1)
    i = pl.program_id(2)
    par = i % 2
    HD = ATTN_HEAD_DIM
    lane = lax.broadcasted_iota(jnp.int32, (MOBA_BLOCK, 128), 1)
    lo_half = lane < HD
    er = lax.broadcasted_iota(jnp.int32, (128, 128), 0)
    ec = lax.broadcasted_iota(jnp.int32, (128, 128), 1)
    eye = jnp.where(er == ec, 1.0, 0.0).astype(BF16)
    blk = lax.broadcasted_iota(jnp.int32, (HD, MOBA_BLOCK), 0)

    def gate_scores(q):
        qt = _nt(eye, (q * (HD ** -0.5 * LOG2E)).astype(BF16))
        q0, q1 = _split2(q)
        gates = []
        for hh in range(2):
            k0, k1 = _split2(kmean_ref[hh])
            gates.append(_nt(k0, q0) + (_nt(k0, q1) + _nt(k1, q0)))
        return qt, gates

    def store_operands(slot, qt, gates, tile):
        for hh in range(2):
            g = jnp.where(blk < tile, gates[hh], -jnp.inf)
            sel = blk == tile
            for _ in range(MOBA_TOPK):
                mx = jnp.max(g, axis=0, keepdims=True)
                first = jnp.min(jnp.where(g == mx, blk, HD), axis=0, keepdims=True)
                hit = (blk == first) & (mx > -jnp.inf)
                sel = sel | hit
                g = jnp.where(hit, -jnp.inf, g)
            mask = jnp.where(sel, 0.0, NEG_BIG)
            rows = [qt[0:HD], mask] if hh == 0 else [mask, qt[HD:2 * HD]]
            qa_ref[slot, hh] = jnp.concatenate(rows, axis=0).astype(BF16)

    @pl.when(i == 0)
    def _prepare():
        kmean_ref[...] = jnp.zeros_like(kmean_ref)
        lane1 = lax.broadcasted_iota(jnp.int32, (1, 128), 1)
        ones_rows = jnp.ones((VT_ROWS - HD, MOBA_BLOCK), BF16)

        def prep(j, carry):
            rows = pl.ds(pl.multiple_of(j * MOBA_BLOCK, MOBA_BLOCK), MOBA_BLOCK)
            kb = k_ref[0, rows, :]
            km = jnp.mean(kb, axis=0, keepdims=True)
            kmean_ref[0, pl.ds(j, 1), :] = jnp.where(lane1 < HD, km, 0.0)
            kmean_ref[1, pl.ds(j, 1), :] = jnp.where(lane1 < HD, 0.0, km)
            kaug_ref[0, j] = jnp.where(lo_half, kb, jnp.where(lane == HD + j, 1.0, 0.0)).astype(BF16)
            kaug_ref[1, j] = jnp.where(lo_half, jnp.where(lane == j, 1.0, 0.0), kb).astype(BF16)
            vt = _nt(eye, v_ref[0, rows, :]).astype(BF16)
            for hh in range(2):
                vt_ref[j, hh] = jnp.concatenate([vt[hh * HD:(hh + 1) * HD], ones_rows], axis=0)
            return carry

        lax.fori_loop(0, nb, prep, 0)
        kaug_ref[0, nb] = jnp.where(lane == 2 * HD - 1, 1.0, 0.0).astype(BF16)
        kaug_ref[1, nb] = jnp.where(lane == HD - 1, 1.0, 0.0).astype(BF16)
        vt_ref[nb] = jnp.zeros((2, VT_ROWS, MOBA_BLOCK), BF16)
        store_operands(0, *gate_scores(q_ref[0]), 0)

    @pl.when((i == 0) & (bi == 0))
    def _bias_tiles():
        for hh in range(2):
            bias_ref[hh, 0] = _t5_bias_tile_t(relb_ref, 2 * hp + hh, 0)
            bias_ref[hh, 1] = _t5_bias_tile_t(relb_ref, 2 * hp + hh, MOBA_BLOCK)

    def issue(j, buf_ref):
        for hh in range(2):
            buf_ref[hh] = _nn(kaug_ref[hh, j], qa_ref[par, hh])

    def consume(j, buf_ref, st, bias_idx=None):
        out = []
        for hh in range(2):
            s = buf_ref[hh]
            if bias_idx is not None:
                s = s + bias_ref[hh, bias_idx]
            m_new = jnp.maximum(st[hh], jnp.max(s, axis=0, keepdims=True))
            p = jnp.exp2((s - m_new).astype(BF16))
            pv = _nn(vt_ref[j, hh], p)
            acc_ref[hh] = jnp.exp2(st[hh] - m_new) * acc_ref[hh] + pv
            out.append(m_new)
        return tuple(out)

    nfar = jnp.maximum(i - 1, 0)
    far = lambda j: jnp.where(j < nfar, j, nb)
    prev = jnp.where(i >= 1, i - 1, nb)
    a0_ref, b0_ref, a1_ref, b1_ref = (s_ref.at[n] for n in range(4))
    qt_next, gates_next = gate_scores(qn_ref[0])
    issue(i, a0_ref)
    issue(prev, b0_ref)
    issue(far(0), a1_ref)
    issue(far(1), b1_ref)
    store_operands(1 - par, qt_next, gates_next, i + 1)
    acc_ref[...] = jnp.zeros_like(acc_ref)
    st = (jnp.full((1, MOBA_BLOCK), 2 * NEG_BIG, F32),) * 2
    st = consume(i, a0_ref, st, bias_idx=0)
    issue(far(2), a0_ref)
    st = consume(prev, b0_ref, st, bias_idx=1)
    issue(far(3), b0_ref)
    st = consume(far(0), a1_ref, st)
    st = consume(far(1), b1_ref, st)

    def far_quad(j, st):
        issue(far(j + 2), a1_ref)
        issue(far(j + 3), b1_ref)
        st = consume(far(j), a0_ref, st)
        issue(far(j + 4), a0_ref)
        st = consume(far(j + 1), b0_ref, st)
        issue(far(j + 5), b0_ref)
        st = consume(far(j + 2), a1_ref, st)
        return consume(far(j + 3), b1_ref, st)

    nrest = jnp.maximum(nfar - 2, 0)
    n8 = nrest // 8
    st = lax.fori_loop(0, n8, lambda u, st: far_quad(8 * u + 6, far_quad(8 * u + 2, st)), st)
    n4 = (nrest - 8 * n8) // 4
    st = lax.fori_loop(0, n4, lambda u, st: far_quad(8 * n8 + 4 * u + 2, st), st)
    base = 8 * n8 + 4 * n4 + 2
    rem = nfar - base

    def rem_first(st):
        issue(far(base + 2), a1_ref)
        st = consume(far(base), a0_ref, st)
        return consume(far(base + 1), b0_ref, st)

    st = lax.cond(rem >= 1, rem_first, lambda st: st, st)
    lax.cond(rem >= 3, lambda st: consume(far(base + 2), a1_ref, st), lambda st: st, st)

    ot = jnp.concatenate([acc_ref[hh, 0:HD, :] / acc_ref[hh, HD:HD + 1, :] for hh in range(2)], axis=0)
    o_ref[0] = (ot.T * _silu(z_ref[0])).astype(o_ref.dtype)


def _moba(rel_bias, q, k, v, z):
    b, s, _ = q.shape
    nb = s // MOBA_BLOCK
    assert s % MOBA_BLOCK == 0 and nb < ATTN_HEAD_DIM
    pairs = ATTN_HEADS // 2
    blk = lambda hp, bi, i: (bi, i, hp)
    whole = lambda hp, bi, i: (bi, 0, hp)
    return pl.pallas_call(
        functools.partial(_moba_kernel, nb=nb),
        grid=(pairs, b, nb),
        in_specs=[
            pl.BlockSpec(memory_space=pltpu.SMEM),
            pl.BlockSpec((1, MOBA_BLOCK, 128), blk),
            pl.BlockSpec((1, MOBA_BLOCK, 128), lambda hp, bi, i: (bi, jnp.minimum(i + 1, nb - 1), hp)),
            pl.BlockSpec((1, s, 128), whole),
            pl.BlockSpec((1, s, 128), whole),
            pl.BlockSpec((1, MOBA_BLOCK, 128), blk),
        ],
        out_specs=pl.BlockSpec((1, MOBA_BLOCK, 128), blk),
        out_shape=jax.ShapeDtypeStruct((b, s, ATTN_WIDTH), BF16),
        scratch_shapes=[
            pltpu.VMEM((2, nb + 1, MOBA_BLOCK, 128), BF16),
            pltpu.VMEM((nb + 1, 2, VT_ROWS, MOBA_BLOCK), BF16),
            pltpu.VMEM((2, ATTN_HEAD_DIM, 128), F32),
            pltpu.VMEM((2, 2, MOBA_BLOCK, MOBA_BLOCK), F32),
            pltpu.VMEM((2, VT_ROWS, MOBA_BLOCK), F32),
            pltpu.VMEM((2, 2, 128, MOBA_BLOCK), BF16),
            pltpu.VMEM((4, 2, MOBA_BLOCK, MOBA_BLOCK), F32),
        ],
        compiler_params=pltpu.CompilerParams(
            dimension_semantics=("arbitrary", "arbitrary", "arbitrary"), vmem_limit_bytes=VMEM_LIMIT),
        name="moba",
    )(rel_bias, q, q, k, v, z)


DN_ROWS = DN_HEADS * DN_CHUNK
DN_PREC = dict(kk="bf16", qk="bf16", inv="bf16", uw="bf16", state="bf16", intra="bf16")


def _split2(a):
    a0 = a.astype(BF16)
    return a0, (a - a0.astype(F32)).astype(BF16)


def _split3(a):
    a0 = a.astype(BF16)
    r1 = a - a0.astype(F32)
    a1 = r1.astype(BF16)
    return a0, a1, (r1 - a1.astype(F32)).astype(BF16)


def _dot(a, b, prec, dims=(((1,), (0,)), ((), ()))):
    f = lambda x, y: lax.dot_general(x, y, dims, preferred_element_type=F32)
    if prec == "bf16":
        return f(a.astype(BF16), b.astype(BF16))
    assert prec == "x3"
    a0, a1 = _split2(a)
    b0, b1 = _split2(b)
    return f(a0, b0) + (f(a0, b1) + f(a1, b0))


_NT = (((1,), (1,)), ((), ()))
_TN = (((0,), (0,)), ((), ()))


def _dot_sel(sel, b):
    return sum(_nn(sel, p) for p in _split3(b))


def _deltanet_kernel(alog_ref, dtb_ref, x_ref, z_ref, g_ref, cw_ref, nw_ref, o_ref,
                     state_ref, pad_ref, act_ref, rep_ref, *, tb):
    t = pl.program_id(1)
    C, R, W, H, D = DN_CHUNK, DN_ROWS, DN_WIDTH, DN_HEADS, DN_HEAD
    hist = DN_CONV_WIDTH - 1

    @pl.when(t == 0)
    def _reset():
        state_ref[...] = jnp.zeros_like(state_ref)
        pad_ref[0:8, :] = jnp.zeros((8, 3 * W), F32)

    pad_ref[8:8 + tb, :] = x_ref[0]
    for gcol in range(3 * H):
        cols = slice(gcol * D, (gcol + 1) * D)
        acc = pad_ref[pl.ds(8 - hist, tb), cols] * cw_ref[0:1, cols]
        for w in range(1, DN_CONV_WIDTH):
            acc = acc + pad_ref[pl.ds(8 - hist + w, tb), cols] * cw_ref[w:w + 1, cols]
        a = _silu(acc)
        if gcol < 2 * H:
            a = a * lax.rsqrt(jnp.sum(a * a, axis=-1, keepdims=True) + EPS)
            if gcol < H:
                a = a * (D ** -0.5)
        act_ref[:, cols] = a
    pad_ref[0:8, :] = x_ref[0, tb - 8:tb, :]

    lane1 = lax.broadcasted_iota(jnp.int32, (1, GATE_PAD), 1)
    alog_row = jnp.zeros((1, GATE_PAD), F32)
    dtb_row = jnp.zeros((1, GATE_PAD), F32)
    for h in range(H):
        alog_row = jnp.where(lane1 == H + h, alog_ref[h], alog_row)
        dtb_row = jnp.where(lane1 == H + h, dtb_ref[h], dtb_row)
    gates = g_ref[0]
    xs = gates + dtb_row
    softplus = jnp.maximum(xs, 0.0) + jnp.log(1.0 + jnp.exp(-jnp.abs(xs)))
    gval = jnp.where(lane1 < H, 1.0 / (1.0 + jnp.exp(-gates)), -jnp.exp(alog_row) * softplus)
    tr = lax.broadcasted_iota(jnp.int32, (tb, tb), 0)
    tc = lax.broadcasted_iota(jnp.int32, (tb, tb), 1)
    gcum = _dot_sel(jnp.where((tr // C == tc // C) & (tr >= tc), 1.0, 0.0).astype(BF16), gval)
    er = lax.broadcasted_iota(jnp.int32, (GATE_PAD, 2 * H * D), 0)
    ec = lax.broadcasted_iota(jnp.int32, (GATE_PAD, 2 * H * D), 1)
    spread = jnp.where(er == ec // D, 1.0, 0.0).astype(BF16)
    rep_ref[...] = sum(_nn(p, spread) for p in _split3(jnp.where(lane1 < H, gval, gcum)))

    r = lax.broadcasted_iota(jnp.int32, (R, R), 0)
    c = lax.broadcasted_iota(jnp.int32, (R, R), 1)
    same = (r // C) == (c // C)
    incl = same & (r >= c)
    strict = same & (r > c)
    eye = jnp.where(r == c, 1.0, 0.0)
    lane_r = lax.broadcasted_iota(jnp.int32, (R, D), 1)
    P = DN_PREC

    chunks = range(tb // C)
    each = lambda f, *seqs: [f(*a) for a in zip(*seqs)]
    rows = [slice(ci * C, (ci + 1) * C) for ci in chunks]
    stack = lambda ref, c0: [jnp.concatenate([ref[rw, c0 + D * h:c0 + D * (h + 1)] for h in range(H)], axis=0)
                             for rw in rows]
    qc, kc, vc = stack(act_ref, 0), stack(act_ref, W), stack(act_ref, 2 * W)
    beta = stack(rep_ref, 0)
    gc = stack(rep_ref, H * D)

    def pair_diff(g):
        p0, p1, p2 = (p.astype(F32) for p in _split3(g))
        u_m = jnp.where(lane_r == 0, p0, jnp.where(lane_r == 1, p1, jnp.where(lane_r == 2, p2,
                        jnp.where(lane_r < 6, 1.0, 0.0))))
        w_m = jnp.where(lane_r < 3, 1.0, jnp.where(lane_r == 3, -p0, jnp.where(lane_r == 4, -p1,
                        jnp.where(lane_r == 5, -p2, 0.0))))
        return _nt(u_m.astype(BF16), w_m.astype(BF16))

    decay = each(lambda g: jnp.where(incl, jnp.exp(jnp.where(incl, pair_diff(g), 0.0)), 0.0), gc)
    kb = each(lambda k, b: k * b, kc, beta)
    x = each(lambda a, k, d: -jnp.where(strict, _dot(a, k, P["kk"], _NT) * d, 0.0), kb, kc, decay)
    sq = lambda m: each(lambda a: _dot(a, a, P["inv"]), m)
    mul = lambda ma, mb: each(lambda a, b: _dot(a, b, P["inv"]), ma, mb)
    add = lambda ma, mb: each(lambda a, b: a + b, ma, mb)
    plus_eye = lambda m: each(lambda a: eye + a, m)
    x2 = sq(x)
    x4 = sq(x2)
    f1 = plus_eye(x)
    m1 = add(f1, mul(f1, x2))
    x8 = sq(x4)
    f4 = plus_eye(x4)
    m2 = add(f4, mul(f4, x8))
    x16 = sq(x8)
    m12 = mul(m1, m2)
    x32 = sq(x16)
    f16 = plus_eye(x16)
    m3 = add(f16, mul(f16, x32))
    tinv = mul(m12, m3)
    eg = each(jnp.exp, gc)
    uw = each(lambda t, v, b, k, e: _dot(t, jnp.concatenate([v * b, k * e], axis=1), P["uw"]),
              tinv, vc, beta, kb, eg)
    a_intra = each(lambda q, k, d: jnp.where(incl, _dot(q, k, P["qk"], _NT) * d, 0.0), qc, kc, decay)
    glast = [[g[h * C + C - 1:h * C + C, :] for h in range(H)] for g in gc]
    q_dec = each(lambda q, e: q * e, qc, eg)
    k_dec = each(lambda k, g, gl: k * jnp.exp(
        jnp.concatenate([jnp.broadcast_to(t, (C, D)) for t in gl], axis=0) - g), kc, gc, glast)

    hs = [slice(h * C, (h + 1) * C) for h in range(H)]
    for ci in chunks:
        state = [state_ref[h] for h in range(H)]
        wq = [_dot(jnp.concatenate([uw[ci][hs[h], D:2 * D], q_dec[ci][hs[h]]], axis=0), state[h], P["state"])
              for h in range(H)]
        vn = [uw[ci][hs[h], 0:D] - wq[h][0:C] for h in range(H)]
        for h in range(H):
            state_ref[h] = state[h] * jnp.exp(glast[ci][h]) + _dot(k_dec[ci][hs[h]], vn[h], P["state"], _TN)
        o = jnp.concatenate([wq[h][C:2 * C] for h in range(H)], axis=0) + _dot(
            a_intra[ci], jnp.concatenate(vn, axis=0), P["intra"])
        on = o * lax.rsqrt(jnp.mean(o * o, axis=-1, keepdims=True) + EPS) * nw_ref[...]
        for h in range(H):
            cols = slice(h * D, (h + 1) * D)
            o_ref[0, rows[ci], cols] = (on[hs[h]] * _silu(z_ref[0, rows[ci], cols])).astype(o_ref.dtype)


def _deltanet(a_log, dt_bias, qkv, z, gates, conv_w, dn_norm_w, tb):
    b, s, _ = qkv.shape
    assert s % tb == 0 and tb % DN_CHUNK == 0
    blk = lambda bi, t: (bi, t, 0)
    fixed = lambda bi, t: (0, 0)
    return pl.pallas_call(
        functools.partial(_deltanet_kernel, tb=tb),
        grid=(b, s // tb),
        in_specs=[
            pl.BlockSpec(memory_space=pltpu.SMEM),
            pl.BlockSpec(memory_space=pltpu.SMEM),
            pl.BlockSpec((1, tb, 3 * DN_WIDTH), blk),
            pl.BlockSpec((1, tb, DN_WIDTH), blk),
            pl.BlockSpec((1, tb, GATE_PAD), blk),
            pl.BlockSpec((DN_CONV_WIDTH, 3 * DN_WIDTH), fixed),
            pl.BlockSpec((1, DN_HEAD), fixed),
        ],
        out_specs=pl.BlockSpec((1, tb, DN_WIDTH), blk),
        out_shape=jax.ShapeDtypeStruct((b, s, DN_WIDTH), BF16),
        scratch_shapes=[
            pltpu.VMEM((DN_HEADS, DN_HEAD, DN_HEAD), F32),
            pltpu.VMEM((tb + 8, 3 * DN_WIDTH), F32),
            pltpu.VMEM((tb, 3 * DN_WIDTH), F32),
            pltpu.VMEM((tb, 2 * DN_HEADS * DN_HEAD), F32),
        ],
        compiler_params=pltpu.CompilerParams(
            dimension_semantics=("arbitrary", "arbitrary"), vmem_limit_bytes=VMEM_LIMIT),
        name="deltanet",
    )(a_log, dt_bias, qkv, z, gates, conv_w, dn_norm_w)


def _outproj_kernel(x_ref, ya_ref, yd_ref, wa_ref, wd_ref, o_ref):
    o_ref[...] = x_ref[...] + _nn(ya_ref[...], wa_ref[...]) + _nn(yd_ref[...], wd_ref[...])


def _outproj(x2, ya, yd, wa, wd, tm):
    n = x2.shape[0]
    row = lambda i: (i, 0)
    fixed = lambda i: (0, 0)
    return pl.pallas_call(
        _outproj_kernel,
        grid=(n // tm,),
        in_specs=[
            pl.BlockSpec((tm, D_MODEL), row),
            pl.BlockSpec((tm, ATTN_WIDTH), row),
            pl.BlockSpec((tm, DN_WIDTH), row),
            pl.BlockSpec((ATTN_WIDTH, D_MODEL), fixed),
            pl.BlockSpec((DN_WIDTH, D_MODEL), fixed),
        ],
        out_specs=pl.BlockSpec((tm, D_MODEL), row),
        out_shape=jax.ShapeDtypeStruct((n, D_MODEL), F32),
        compiler_params=pltpu.CompilerParams(dimension_semantics=("arbitrary",), vmem_limit_bytes=VMEM_LIMIT),
        name="outproj",
    )(x2, ya, yd, wa, wd)


def _row_tile(n):
    for tm in (512, 256, 128, 64, 32, 16, 8):
        if n % tm == 0:
            return tm
    raise ValueError(f"row count {n} is not a multiple of 8")


def kernel(x, rel_bias, norm_w, w_in, q_norm_w, k_norm_w, conv_w, a_log, dt_bias, dn_norm_w, w_out):
    b, s, d = x.shape
    assert d == D_MODEL and norm_w.shape[0] == 1, "single-layer kernel"
    n = b * s
    x2 = x.reshape(n, d)
    tm = _row_tile(n)

    w_main = w_in[0][:, :MAIN_COLS].astype(BF16)
    w_gate = jnp.pad(w_in[0][:, MAIN_COLS:], ((0, 0), (0, GATE_PAD - 2 * DN_HEADS))).astype(BF16)
    qnw = jnp.tile(q_norm_w[0], ATTN_HEADS)[None, :]
    knw = jnp.tile(k_norm_w[0], ATTN_HEADS)[None, :]

    q, k, v, za, qkv, zd, gates = _inproj(x2, norm_w, w_main, w_gate, qnw, knw, tm)

    r3 = lambda t: t.reshape(b, s, t.shape[-1])
    ya = _moba(rel_bias, r3(q), r3(k), r3(v), r3(za))
    tb = 256 if s % 256 == 0 else DN_CHUNK
    yd = _deltanet(a_log[0], dt_bias[0], r3(qkv), r3(zd), r3(gates), conv_w[0], dn_norm_w, tb)

    w_o = w_out[0].astype(BF16)
    out = _outproj(x2, ya.reshape(n, ATTN_WIDTH), yd.reshape(n, DN_WIDTH), w_o[:ATTN_WIDTH], w_o[ATTN_WIDTH:], tm)
    return out.reshape(b, s, d)
```

```python
import functools
import math

import jax
import jax.numpy as jnp
from jax import lax
from jax.experimental import pallas as pl
from jax.experimental.pallas import tpu as pltpu

F32 = jnp.float32
BF16 = jnp.bfloat16
HI = lax.Precision.HIGHEST

D_MODEL = 1024
ATTN_HEADS = 8
ATTN_HEAD_DIM = 64
ATTN_WIDTH = ATTN_HEADS * ATTN_HEAD_DIM
MOBA_BLOCK = 256
MOBA_TOPK = 3
REL_BUCKETS = 32
REL_MAX_DISTANCE = 128
DN_HEADS = 4
DN_HEAD = 128
DN_WIDTH = DN_HEADS * DN_HEAD
DN_CONV_WIDTH = 4
DN_CHUNK = 64
MAIN_COLS = 4 * ATTN_WIDTH + 3 * DN_WIDTH + DN_WIDTH
GATE_PAD = 128
EPS = 1e-6
LOG2E = math.log2(math.e)
NEG_BIG = -32768.0
VT_ROWS = ATTN_HEAD_DIM + 16
Q_TILE = 2 * MOBA_BLOCK
VMEM_LIMIT = 56 * 1024 * 1024


def _nt(a, b, precision=None):
    return lax.dot_general(a, b, (((1,), (1,)), ((), ())), preferred_element_type=F32, precision=precision)


def _nn(a, b, precision=None):
    return lax.dot_general(a, b, (((1,), (0,)), ((), ())), preferred_element_type=F32, precision=precision)


def _silu(x):
    half = 0.5 * x
    return half + half * jnp.tanh(half)


def _inproj_kernel(x_ref, nw_ref, w_ref, wg_ref, qnw_ref, knw_ref,
                   q_ref, k_ref, v_ref, za_ref, qkv_ref, zd_ref, g_ref):
    x = x_ref[...]
    ms = jnp.mean(x * x, axis=-1, keepdims=True)
    h = (x * lax.rsqrt(ms + EPS) * nw_ref[...]).astype(BF16)

    r_i = lax.broadcasted_iota(jnp.int32, (ATTN_WIDTH, 128), 0)
    c_i = lax.broadcasted_iota(jnp.int32, (ATTN_WIDTH, 128), 1)
    ind = jnp.where(r_i // ATTN_HEAD_DIM == c_i, 1.0, 0.0).astype(BF16)
    r_e = lax.broadcasted_iota(jnp.int32, (128, ATTN_WIDTH), 0)
    c_e = lax.broadcasted_iota(jnp.int32, (128, ATTN_WIDTH), 1)
    expand = jnp.where(c_e // ATTN_HEAD_DIM == r_e, 1.0, 0.0).astype(BF16)

    def head_rms(t, w):
        ss = _nn((t * t).astype(BF16), ind)
        r = lax.rsqrt(ss * (1.0 / ATTN_HEAD_DIM) + EPS)
        r0, r1 = _split2(r)
        rf = _nn(r0, expand) + _nn(r1, expand)
        return t * rf * w

    def proj(c0, width):
        return _nn(h, w_ref[:, c0:c0 + width])

    q_ref[...] = head_rms(proj(0, ATTN_WIDTH), qnw_ref[...]).astype(q_ref.dtype)
    k_ref[...] = head_rms(proj(ATTN_WIDTH, ATTN_WIDTH), knw_ref[...]).astype(k_ref.dtype)
    v_ref[...] = proj(2 * ATTN_WIDTH, ATTN_WIDTH).astype(v_ref.dtype)
    za_ref[...] = proj(3 * ATTN_WIDTH, ATTN_WIDTH).astype(za_ref.dtype)
    for c in range(3):
        qkv_ref[:, c * DN_WIDTH:(c + 1) * DN_WIDTH] = proj(4 * ATTN_WIDTH + c * DN_WIDTH, DN_WIDTH).astype(qkv_ref.dtype)
    zd_ref[...] = proj(4 * ATTN_WIDTH + 3 * DN_WIDTH, DN_WIDTH).astype(zd_ref.dtype)
    g_ref[...] = _nn(h, wg_ref[...])


def _inproj(x2, norm_w, w_main, w_gate, qnw, knw, tm):
    n = x2.shape[0]
    row = lambda i: (i, 0)
    fixed = lambda i: (0, 0)
    outs = [
        jax.ShapeDtypeStruct((n, ATTN_WIDTH), F32),
        jax.ShapeDtypeStruct((n, ATTN_WIDTH), F32),
        jax.ShapeDtypeStruct((n, ATTN_WIDTH), BF16),
        jax.ShapeDtypeStruct((n, ATTN_WIDTH), F32),
        jax.ShapeDtypeStruct((n, 3 * DN_WIDTH), F32),
        jax.ShapeDtypeStruct((n, DN_WIDTH), F32),
        jax.ShapeDtypeStruct((n, GATE_PAD), F32),
    ]
    return pl.pallas_call(
        _inproj_kernel,
        grid=(n // tm,),
        in_specs=[
            pl.BlockSpec((tm, D_MODEL), row),
            pl.BlockSpec((1, D_MODEL), fixed),
            pl.BlockSpec((D_MODEL, MAIN_COLS), fixed),
            pl.BlockSpec((D_MODEL, GATE_PAD), fixed),
            pl.BlockSpec((1, ATTN_WIDTH), fixed),
            pl.BlockSpec((1, ATTN_WIDTH), fixed),
        ],
        out_specs=[pl.BlockSpec((tm, o.shape[1]), row) for o in outs],
        out_shape=outs,
        compiler_params=pltpu.CompilerParams(dimension_semantics=("arbitrary",), vmem_limit_bytes=VMEM_LIMIT),
        name="inproj",
    )(x2, norm_w, w_main, w_gate, qnw, knw)


def _t5_bias_tile_t(relb_ref, head, offset):
    c = lax.broadcasted_iota(jnp.int32, (MOBA_BLOCK, MOBA_BLOCK), 0)
    r = lax.broadcasted_iota(jnp.int32, (MOBA_BLOCK, MOBA_BLOCK), 1)
    dist = r - c + offset
    n = jnp.maximum(dist, 0)
    max_exact = REL_BUCKETS // 2
    nf = jnp.maximum(n, 1).astype(F32)
    large = max_exact + (jnp.log(nf / max_exact) / math.log(REL_MAX_DISTANCE / max_exact)
                         * (REL_BUCKETS - max_exact)).astype(jnp.int32)
    large = jnp.minimum(large, REL_BUCKETS - 1)
    bucket = jnp.where(n < max_exact, n, large)
    far = relb_ref[REL_BUCKETS - 1, head]
    bias = jnp.zeros((MOBA_BLOCK, MOBA_BLOCK), F32)
    for t in range(REL_BUCKETS):
        bias = jnp.where(bucket == t, relb_ref[t, head] - far, bias)
    return jnp.where(dist >= 0, bias * LOG2E, NEG_BIG)


def _moba_kernel(relb_ref, q_ref, qn_ref, k_ref, v_ref, z_ref, o_ref,
                 kaug_ref, vt_ref, kmean_ref, bias_ref, acc_ref, qa_ref, s_ref, *, nb):
    hp = pl.program_id(0)
    bi = pl.program_id(1)
    t = pl.program_id(2)
    par = t % 2
    HD = ATTN_HEAD_DIM
    lane = lax.broadcasted_iota(jnp.int32, (MOBA_BLOCK, 128), 1)
    lo_half = lane < HD
    er = lax.broadcasted_iota(jnp.int32, (128, 128), 0)
    ec = lax.broadcasted_iota(jnp.int32, (128, 128), 1)
    eye = jnp.where(er == ec, 1.0, 0.0).astype(BF16)
    blk = lax.broadcasted_iota(jnp.int32, (HD, Q_TILE), 0)
    upper = lax.broadcasted_iota(jnp.int32, (1, Q_TILE), 1) // MOBA_BLOCK

    def gate_scores(q):
        qt = _nt(eye, (q * (HD ** -0.5 * LOG2E)).astype(BF16))
        q0, q1 = _split2(q)
        gates = []
        for hh in range(2):
            k0, k1 = _split2(kmean_ref[hh])
            gates.append(_nt(k0, q0) + (_nt(k0, q1) + _nt(k1, q0)))
        return qt, gates

    def store_operands(slot, qt, gates, tile_idx):
        tile = 2 * tile_idx + upper
        for hh in range(2):
            g = jnp.where(blk < tile, gates[hh], -jnp.inf)
            sel = blk == tile
            for _ in range(MOBA_TOPK):
                mx = jnp.max(g, axis=0, keepdims=True)
                first = jnp.min(jnp.where(g == mx, blk, HD), axis=0, keepdims=True)
                hit = (blk == first) & (mx > -jnp.inf)
                sel = sel | hit
                g = jnp.where(hit, -jnp.inf, g)
            mask = jnp.where(sel, 0.0, NEG_BIG)
            rows = [qt[0:HD], mask] if hh == 0 else [mask, qt[HD:2 * HD]]
            qa_ref[slot, hh] = jnp.concatenate(rows, axis=0).astype(BF16)

    @pl.when(t == 0)
    def _prepare():
        kmean_ref[...] = jnp.zeros_like(kmean_ref)
        lane1 = lax.broadcasted_iota(jnp.int32, (1, 128), 1)
        ones_rows = jnp.ones((VT_ROWS - HD, MOBA_BLOCK), BF16)

        def prep(j, carry):
            rows = pl.ds(pl.multiple_of(j * MOBA_BLOCK, MOBA_BLOCK), MOBA_BLOCK)
            kb = k_ref[0, rows, :]
            km = jnp.mean(kb, axis=0, keepdims=True)
            kmean_ref[0, pl.ds(j, 1), :] = jnp.where(lane1 < HD, km, 0.0)
            kmean_ref[1, pl.ds(j, 1), :] = jnp.where(lane1 < HD, 0.0, km)
            kaug_ref[0, j] = jnp.where(lo_half, kb, jnp.where(lane == HD + j, 1.0, 0.0)).astype(BF16)
            kaug_ref[1, j] = jnp.where(lo_half, jnp.where(lane == j, 1.0, 0.0), kb).astype(BF16)
            vt = _nt(eye, v_ref[0, rows, :]).astype(BF16)
            for hh in range(2):
                vt_ref[j, hh] = jnp.concatenate([vt[hh * HD:(hh + 1) * HD], ones_rows], axis=0)
            return carry

        lax.fori_loop(0, nb, prep, 0)
        kaug_ref[0, nb] = jnp.where(lane == 2 * HD - 1, 1.0, 0.0).astype(BF16)
        kaug_ref[1, nb] = jnp.where(lane == HD - 1, 1.0, 0.0).astype(BF16)
        vt_ref[nb] = jnp.zeros((2, VT_ROWS, MOBA_BLOCK), BF16)
        store_operands(0, *gate_scores(q_ref[0]), 0)

    @pl.when((t == 0) & (bi == 0))
    def _bias_tiles():
        none = jnp.zeros((MOBA_BLOCK, MOBA_BLOCK), F32)
        for hh in range(2):
            own = _t5_bias_tile_t(relb_ref, 2 * hp + hh, 0)
            prev = _t5_bias_tile_t(relb_ref, 2 * hp + hh, MOBA_BLOCK)
            bias_ref[hh, 0] = jnp.concatenate([prev, none], axis=1)
            bias_ref[hh, 1] = jnp.concatenate([own, prev], axis=1)
            bias_ref[hh, 2] = jnp.concatenate([none, own], axis=1)

    def issue(j, buf_ref):
        for hh in range(2):
            buf_ref[hh] = _nn(kaug_ref[hh, j], qa_ref[par, hh])

    def consume(j, buf_ref, st, bias_idx=None):
        out = []
        for hh in range(2):
            s = buf_ref[hh]
            if bias_idx is not None:
                s = s + bias_ref[hh, bias_idx]
            m_new = jnp.maximum(st[hh], jnp.max(s, axis=0, keepdims=True))
            p = jnp.exp2((s - m_new).astype(BF16))
            pv = _nn(vt_ref[j, hh], p)
            acc_ref[hh] = jnp.exp2(st[hh] - m_new) * acc_ref[hh] + pv
            out.append(m_new)
        return tuple(out)

    nfar = jnp.maximum(2 * t - 1, 0)
    far = lambda j: jnp.where(j < nfar, j, nb)
    a0_ref, b0_ref, a1_ref, b1_ref = (s_ref.at[n] for n in range(4))
    qt_next, gates_next = gate_scores(qn_ref[0])
    issue(2 * t + 1, a0_ref)
    issue(2 * t, b0_ref)
    issue(jnp.where(t >= 1, 2 * t - 1, nb), a1_ref)
    issue(far(0), b1_ref)
    store_operands(1 - par, qt_next, gates_next, t + 1)
    acc_ref[...] = jnp.zeros_like(acc_ref)
    st = (jnp.full((1, Q_TILE), 2 * NEG_BIG, F32),) * 2
    st = consume(2 * t + 1, a0_ref, st, bias_idx=2)
    issue(far(1), a0_ref)
    st = consume(2 * t, b0_ref, st, bias_idx=1)
    issue(far(2), b0_ref)
    st = consume(jnp.where(t >= 1, 2 * t - 1, nb), a1_ref, st, bias_idx=0)
    st = consume(far(0), b1_ref, st)

    def far_quad(j, st):
        issue(far(j + 2), a1_ref)
        issue(far(j + 3), b1_ref)
        st = consume(far(j), a0_ref, st)
        issue(far(j + 4), a0_ref)
        st = consume(far(j + 1), b0_ref, st)
        issue(far(j + 5), b0_ref)
        st = consume(far(j + 2), a1_ref, st)
        return consume(far(j + 3), b1_ref, st)

    nrest = jnp.maximum(nfar - 1, 0)
    n8 = nrest // 8
    st = lax.fori_loop(0, n8, lambda u, st: far_quad(8 * u + 5, far_quad(8 * u + 1, st)), st)
    n4 = (nrest - 8 * n8) // 4
    st = lax.fori_loop(0, n4, lambda u, st: far_quad(8 * n8 + 4 * u + 1, st), st)
    base = 8 * n8 + 4 * n4 + 1
    rem = nfar - base

    def rem_first(st):
        issue(far(base + 2), a1_ref)
        st = consume(far(base), a0_ref, st)
        return consume(far(base + 1), b0_ref, st)

    st = lax.cond(rem >= 1, rem_first, lambda st: st, st)
    lax.cond(rem >= 3, lambda st: consume(far(base + 2), a1_ref, st), lambda st: st, st)

    ot = jnp.concatenate([acc_ref[hh, 0:HD, :] / acc_ref[hh, HD:HD + 1, :] for hh in range(2)], axis=0)
    o_ref[0] = (ot.T * _silu(z_ref[0])).astype(o_ref.dtype)


def _moba(rel_bias, q, k, v, z):
    b, s, _ = q.shape
    nb = s // MOBA_BLOCK
    nt = s // Q_TILE
    assert s % Q_TILE == 0 and nb < ATTN_HEAD_DIM
    pairs = ATTN_HEADS // 2
    blk = lambda hp, bi, i: (bi, i, hp)
    whole = lambda hp, bi, i: (bi, 0, hp)
    return pl.pallas_call(
        functools.partial(_moba_kernel, nb=nb),
        grid=(pairs, b, nt),
        in_specs=[
            pl.BlockSpec(memory_space=pltpu.SMEM),
            pl.BlockSpec((1, Q_TILE, 128), blk),
            pl.BlockSpec((1, Q_TILE, 128), lambda hp, bi, i: (bi, jnp.minimum(i + 1, nt - 1), hp)),
            pl.BlockSpec((1, s, 128), whole),
            pl.BlockSpec((1, s, 128), whole),
            pl.BlockSpec((1, Q_TILE, 128), blk),
        ],
        out_specs=pl.BlockSpec((1, Q_TILE, 128), blk),
        out_shape=jax.ShapeDtypeStruct((b, s, ATTN_WIDTH), BF16),
        scratch_shapes=[
            pltpu.VMEM((2, nb + 1, MOBA_BLOCK, 128), BF16),
            pltpu.VMEM((nb + 1, 2, VT_ROWS, MOBA_BLOCK), BF16),
            pltpu.VMEM((2, ATTN_HEAD_DIM, 128), F32),
            pltpu.VMEM((2, 3, MOBA_BLOCK, Q_TILE), F32),
            pltpu.VMEM((2, VT_ROWS, Q_TILE), F32),
            pltpu.VMEM((2, 2, 128, Q_TILE), BF16),
            pltpu.VMEM((4, 2, MOBA_BLOCK, Q_TILE), F32),
        ],
        compiler_params=pltpu.CompilerParams(
            dimension_semantics=("arbitrary", "arbitrary", "arbitrary"), vmem_limit_bytes=VMEM_LIMIT),
        name="moba",
    )(rel_bias, q, q, k, v, z)


DN_ROWS = DN_HEADS * DN_CHUNK
DN_PREC = dict(kk="bf16", qk="bf16", inv="bf16", uw="bf16", state="bf16", intra="bf16")


def _split2(a):
    a0 = a.astype(BF16)
    return a0, (a - a0.astype(F32)).astype(BF16)


def _split3(a):
    a0 = a.astype(BF16)
    r1 = a - a0.astype(F32)
    a1 = r1.astype(BF16)
    return a0, a1, (r1 - a1.astype(F32)).astype(BF16)


def _dot(a, b, prec, dims=(((1,), (0,)), ((), ()))):
    f = lambda x, y: lax.dot_general(x, y, dims, preferred_element_type=F32)
    if prec == "bf16":
        return f(a.astype(BF16), b.astype(BF16))
    assert prec == "x3"
    a0, a1 = _split2(a)
    b0, b1 = _split2(b)
    return f(a0, b0) + (f(a0, b1) + f(a1, b0))


_NT = (((1,), (1,)), ((), ()))
_TN = (((0,), (0,)), ((), ()))


def _dot_sel(sel, b):
    return sum(_nn(sel, p) for p in _split3(b))


def _deltanet_kernel(alog_ref, dtb_ref, x_ref, z_ref, g_ref, cw_ref, nw_ref, o_ref,
                     state_ref, pad_ref, act_ref, rep_ref, *, tb):
    t = pl.program_id(1)
    C, R, W, H, D = DN_CHUNK, DN_ROWS, DN_WIDTH, DN_HEADS, DN_HEAD
    hist = DN_CONV_WIDTH - 1

    @pl.when(t == 0)
    def _reset():
        state_ref[...] = jnp.zeros_like(state_ref)
        pad_ref[0:8, :] = jnp.zeros((8, 3 * W), F32)

    pad_ref[8:8 + tb, :] = x_ref[0]
    for gcol in range(3 * H):
        cols = slice(gcol * D, (gcol + 1) * D)
        acc = pad_ref[pl.ds(8 - hist, tb), cols] * cw_ref[0:1, cols]
        for w in range(1, DN_CONV_WIDTH):
            acc = acc + pad_ref[pl.ds(8 - hist + w, tb), cols] * cw_ref[w:w + 1, cols]
        a = _silu(acc)
        if gcol < 2 * H:
            a = a * lax.rsqrt(jnp.sum(a * a, axis=-1, keepdims=True) + EPS)
            if gcol < H:
                a = a * (D ** -0.5)
        act_ref[:, cols] = a
    pad_ref[0:8, :] = x_ref[0, tb - 8:tb, :]

    lane1 = lax.broadcasted_iota(jnp.int32, (1, GATE_PAD), 1)
    alog_row = jnp.zeros((1, GATE_PAD), F32)
    dtb_row = jnp.zeros((1, GATE_PAD), F32)
    for h in range(H):
        alog_row = jnp.where(lane1 == H + h, alog_ref[h], alog_row)
        dtb_row = jnp.where(lane1 == H + h, dtb_ref[h], dtb_row)
    gates = g_ref[0]
    xs = gates + dtb_row
    softplus = jnp.maximum(xs, 0.0) + jnp.log(1.0 + jnp.exp(-jnp.abs(xs)))
    gval = jnp.where(lane1 < H, 1.0 / (1.0 + jnp.exp(-gates)), -jnp.exp(alog_row) * softplus)
    tr = lax.broadcasted_iota(jnp.int32, (tb, tb), 0)
    tc = lax.broadcasted_iota(jnp.int32, (tb, tb), 1)
    gcum = _dot_sel(jnp.where((tr // C == tc // C) & (tr >= tc), 1.0, 0.0).astype(BF16), gval)
    er = lax.broadcasted_iota(jnp.int32, (GATE_PAD, 2 * H * D), 0)
    ec = lax.broadcasted_iota(jnp.int32, (GATE_PAD, 2 * H * D), 1)
    spread = jnp.where(er == ec // D, 1.0, 0.0).astype(BF16)
    rep_ref[...] = sum(_nn(p, spread) for p in _split3(jnp.where(lane1 < H, gval, gcum)))

    r = lax.broadcasted_iota(jnp.int32, (R, R), 0)
    c = lax.broadcasted_iota(jnp.int32, (R, R), 1)
    same = (r // C) == (c // C)
    incl = same & (r >= c)
    strict = same & (r > c)
    eye = jnp.where(r == c, 1.0, 0.0)
    lane_r = lax.broadcasted_iota(jnp.int32, (R, D), 1)
    P = DN_PREC

    chunks = range(tb // C)
    each = lambda f, *seqs: [f(*a) for a in zip(*seqs)]
    rows = [slice(ci * C, (ci + 1) * C) for ci in chunks]
    stack = lambda ref, c0: [jnp.concatenate([ref[rw, c0 + D * h:c0 + D * (h + 1)] for h in range(H)], axis=0)
                             for rw in rows]
    qc, kc, vc = stack(act_ref, 0), stack(act_ref, W), stack(act_ref, 2 * W)
    beta = stack(rep_ref, 0)
    gc = stack(rep_ref, H * D)

    def pair_diff(g):
        p0, p1, p2 = (p.astype(F32) for p in _split3(g))
        u_m = jnp.where(lane_r == 0, p0, jnp.where(lane_r == 1, p1, jnp.where(lane_r == 2, p2,
                        jnp.where(lane_r < 6, 1.0, 0.0))))
        w_m = jnp.where(lane_r < 3, 1.0, jnp.where(lane_r == 3, -p0, jnp.where(lane_r == 4, -p1,
                        jnp.where(lane_r == 5, -p2, 0.0))))
        return _nt(u_m.astype(BF16), w_m.astype(BF16))

    decay = each(lambda g: jnp.where(incl, jnp.exp(jnp.where(incl, pair_diff(g), 0.0)), 0.0), gc)
    kb = each(lambda k, b: k * b, kc, beta)
    x = each(lambda a, k, d: -jnp.where(strict, _dot(a, k, P["kk"], _NT) * d, 0.0), kb, kc, decay)
    sq = lambda m: each(lambda a: _dot(a, a, P["inv"]), m)
    mul = lambda ma, mb: each(lambda a, b: _dot(a, b, P["inv"]), ma, mb)
    add = lambda ma, mb: each(lambda a, b: a + b, ma, mb)
    plus_eye = lambda m: each(lambda a: eye + a, m)
    x2 = sq(x)
    x4 = sq(x2)
    f1 = plus_eye(x)
    m1 = add(f1, mul(f1, x2))
    x8 = sq(x4)
    f4 = plus_eye(x4)
    m2 = add(f4, mul(f4, x8))
    x16 = sq(x8)
    m12 = mul(m1, m2)
    x32 = sq(x16)
    f16 = plus_eye(x16)
    m3 = add(f16, mul(f16, x32))
    tinv = mul(m12, m3)
    eg = each(jnp.exp, gc)
    uw = each(lambda t, v, b, k, e: _dot(t, jnp.concatenate([v * b, k * e], axis=1), P["uw"]),
              tinv, vc, beta, kb, eg)
    a_intra = each(lambda q, k, d: jnp.where(incl, _dot(q, k, P["qk"], _NT) * d, 0.0), qc, kc, decay)
    glast = [[g[h * C + C - 1:h * C + C, :] for h in range(H)] for g in gc]
    q_dec = each(lambda q, e: q * e, qc, eg)
    k_dec = each(lambda k, g, gl: k * jnp.exp(
        jnp.concatenate([jnp.broadcast_to(t, (C, D)) for t in gl], axis=0) - g), kc, gc, glast)

    hs = [slice(h * C, (h + 1) * C) for h in range(H)]
    for ci in chunks:
        state = [state_ref[h] for h in range(H)]
        wq = [_dot(jnp.concatenate([uw[ci][hs[h], D:2 * D], q_dec[ci][hs[h]]], axis=0), state[h], P["state"])
              for h in range(H)]
        vn = [uw[ci][hs[h], 0:D] - wq[h][0:C] for h in range(H)]
        for h in range(H):
            state_ref[h] = state[h] * jnp.exp(glast[ci][h]) + _dot(k_dec[ci][hs[h]], vn[h], P["state"], _TN)
        o = jnp.concatenate([wq[h][C:2 * C] for h in range(H)], axis=0) + _dot(
            a_intra[ci], jnp.concatenate(vn, axis=0), P["intra"])
        on = o * lax.rsqrt(jnp.mean(o * o, axis=-1, keepdims=True) + EPS) * nw_ref[...]
        for h in range(H):
            cols = slice(h * D, (h + 1) * D)
            o_ref[0, rows[ci], cols] = (on[hs[h]] * _silu(z_ref[0, rows[ci], cols])).astype(o_ref.dtype)


def _deltanet(a_log, dt_bias, qkv, z, gates, conv_w, dn_norm_w, tb):
    b, s, _ = qkv.shape
    assert s % tb == 0 and tb % DN_CHUNK == 0
    blk = lambda bi, t: (bi, t, 0)
    fixed = lambda bi, t: (0, 0)
    return pl.pallas_call(
        functools.partial(_deltanet_kernel, tb=tb),
        grid=(b, s // tb),
        in_specs=[
            pl.BlockSpec(memory_space=pltpu.SMEM),
            pl.BlockSpec(memory_space=pltpu.SMEM),
            pl.BlockSpec((1, tb, 3 * DN_WIDTH), blk),
            pl.BlockSpec((1, tb, DN_WIDTH), blk),
            pl.BlockSpec((1, tb, GATE_PAD), blk),
            pl.BlockSpec((DN_CONV_WIDTH, 3 * DN_WIDTH), fixed),
            pl.BlockSpec((1, DN_HEAD), fixed),
        ],
        out_specs=pl.BlockSpec((1, tb, DN_WIDTH), blk),
        out_shape=jax.ShapeDtypeStruct((b, s, DN_WIDTH), BF16),
        scratch_shapes=[
            pltpu.VMEM((DN_HEADS, DN_HEAD, DN_HEAD), F32),
            pltpu.VMEM((tb + 8, 3 * DN_WIDTH), F32),
            pltpu.VMEM((tb, 3 * DN_WIDTH), F32),
            pltpu.VMEM((tb, 2 * DN_HEADS * DN_HEAD), F32),
        ],
        compiler_params=pltpu.CompilerParams(
            dimension_semantics=("arbitrary", "arbitrary"), vmem_limit_bytes=VMEM_LIMIT),
        name="deltanet",
    )(a_log, dt_bias, qkv, z, gates, conv_w, dn_norm_w)


def _outproj_kernel(x_ref, ya_ref, yd_ref, wa_ref, wd_ref, o_ref):
    o_ref[...] = x_ref[...] + _nn(ya_ref[...], wa_ref[...]) + _nn(yd_ref[...], wd_ref[...])


def _outproj(x2, ya, yd, wa, wd, tm):
    n = x2.shape[0]
    row = lambda i: (i, 0)
    fixed = lambda i: (0, 0)
    return pl.pallas_call(
        _outproj_kernel,
        grid=(n // tm,),
        in_specs=[
            pl.BlockSpec((tm, D_MODEL), row),
            pl.BlockSpec((tm, ATTN_WIDTH), row),
            pl.BlockSpec((tm, DN_WIDTH), row),
            pl.BlockSpec((ATTN_WIDTH, D_MODEL), fixed),
            pl.BlockSpec((DN_WIDTH, D_MODEL), fixed),
        ],
        out_specs=pl.BlockSpec((tm, D_MODEL), row),
        out_shape=jax.ShapeDtypeStruct((n, D_MODEL), F32),
        compiler_params=pltpu.CompilerParams(dimension_semantics=("arbitrary",), vmem_limit_bytes=VMEM_LIMIT),
        name="outproj",
    )(x2, ya, yd, wa, wd)


def _row_tile(n):
    for tm in (512, 256, 128, 64, 32, 16, 8):
        if n % tm == 0:
            return tm
    raise ValueError(f"row count {n} is not a multiple of 8")


def kernel(x, rel_bias, norm_w, w_in, q_norm_w, k_norm_w, conv_w, a_log, dt_bias, dn_norm_w, w_out):
    b, s, d = x.shape
    assert d == D_MODEL and norm_w.shape[0] == 1, "single-layer kernel"
    n = b * s
    x2 = x.reshape(n, d)
    tm = _row_tile(n)

    w_main = w_in[0][:, :MAIN_COLS].astype(BF16)
    w_gate = jnp.pad(w_in[0][:, MAIN_COLS:], ((0, 0), (0, GATE_PAD - 2 * DN_HEADS))).astype(BF16)
    qnw = jnp.tile(q_norm_w[0], ATTN_HEADS)[None, :]
    knw = jnp.tile(k_norm_w[0], ATTN_HEADS)[None, :]

    q, k, v, za, qkv, zd, gates = _inproj(x2, norm_w, w_main, w_gate, qnw, knw, tm)

    r3 = lambda t: t.reshape(b, s, t.shape[-1])
    ya = _moba(rel_bias, r3(q), r3(k), r3(v), r3(za))
    tb = 256 if s % 256 == 0 else DN_CHUNK
    yd = _deltanet(a_log[0], dt_bias[0], r3(qkv), r3(zd), r3(gates), conv_w[0], dn_norm_w, tb)

    w_o = w_out[0].astype(BF16)
    out = _outproj(x2, ya.reshape(n, ATTN_WIDTH), yd.reshape(n, DN_WIDTH), w_o[:ATTN_WIDTH], w_o[ATTN_WIDTH:], tm)
    return out.reshape(b, s, d)
```

```python
import functools
import math

import jax
import jax.numpy as jnp
from jax import lax
from jax.experimental import pallas as pl
from jax.experimental.pallas import tpu as pltpu

F32 = jnp.float32
BF16 = jnp.bfloat16
HI = lax.Precision.HIGHEST

D_MODEL = 1024
ATTN_HEADS = 8
ATTN_HEAD_DIM = 64
ATTN_WIDTH = ATTN_HEADS * ATTN_HEAD_DIM
MOBA_BLOCK = 256
MOBA_TOPK = 3
REL_BUCKETS = 32
REL_MAX_DISTANCE = 128
DN_HEADS = 4
DN_HEAD = 128
DN_WIDTH = DN_HEADS * DN_HEAD
DN_CONV_WIDTH = 4
DN_CHUNK = 64
MAIN_COLS = 4 * ATTN_WIDTH + 3 * DN_WIDTH + DN_WIDTH
GATE_PAD = 128
EPS = 1e-6
LOG2E = math.log2(math.e)
NEG_BIG = -32768.0
VT_ROWS = ATTN_HEAD_DIM + 16
VMEM_LIMIT = 56 * 1024 * 1024


def _nt(a, b, precision=None):
    return lax.dot_general(a, b, (((1,), (1,)), ((), ())), preferred_element_type=F32, precision=precision)


def _nn(a, b, precision=None):
    return lax.dot_general(a, b, (((1,), (0,)), ((), ())), preferred_element_type=F32, precision=precision)


def _silu(x):
    half = 0.5 * x
    return half + half * jnp.tanh(half)


def _inproj_kernel(x_ref, nw_ref, w_ref, wg_ref, qnw_ref, knw_ref,
                   q_ref, k_ref, v_ref, za_ref, qkv_ref, zd_ref, g_ref):
    x = x_ref[...]
    ms = jnp.mean(x * x, axis=-1, keepdims=True)
    h = (x * lax.rsqrt(ms + EPS) * nw_ref[...]).astype(BF16)

    r_i = lax.broadcasted_iota(jnp.int32, (ATTN_WIDTH, 128), 0)
    c_i = lax.broadcasted_iota(jnp.int32, (ATTN_WIDTH, 128), 1)
    ind = jnp.where(r_i // ATTN_HEAD_DIM == c_i, 1.0, 0.0).astype(BF16)
    r_e = lax.broadcasted_iota(jnp.int32, (128, ATTN_WIDTH), 0)
    c_e = lax.broadcasted_iota(jnp.int32, (128, ATTN_WIDTH), 1)
    expand = jnp.where(c_e // ATTN_HEAD_DIM == r_e, 1.0, 0.0).astype(BF16)

    def head_rms(t, w):
        ss = _nn((t * t).astype(BF16), ind)
        r = lax.rsqrt(ss * (1.0 / ATTN_HEAD_DIM) + EPS)
        r0, r1 = _split2(r)
        rf = _nn(r0, expand) + _nn(r1, expand)
        return t * rf * w

    def proj(c0, width):
        return _nn(h, w_ref[:, c0:c0 + width])

    q_ref[...] = head_rms(proj(0, ATTN_WIDTH), qnw_ref[...]).astype(q_ref.dtype)
    k_ref[...] = head_rms(proj(ATTN_WIDTH, ATTN_WIDTH), knw_ref[...]).astype(k_ref.dtype)
    v_ref[...] = proj(2 * ATTN_WIDTH, ATTN_WIDTH).astype(v_ref.dtype)
    za_ref[...] = proj(3 * ATTN_WIDTH, ATTN_WIDTH).astype(za_ref.dtype)
    for c in range(3):
        qkv_ref[:, c * DN_WIDTH:(c + 1) * DN_WIDTH] = proj(4 * ATTN_WIDTH + c * DN_WIDTH, DN_WIDTH).astype(qkv_ref.dtype)
    zd_ref[...] = proj(4 * ATTN_WIDTH + 3 * DN_WIDTH, DN_WIDTH).astype(zd_ref.dtype)
    g_ref[...] = _nn(h, wg_ref[...])


def _inproj(x2, norm_w, w_main, w_gate, qnw, knw, tm):
    n = x2.shape[0]
    row = lambda i: (i, 0)
    fixed = lambda i: (0, 0)
    outs = [
        jax.ShapeDtypeStruct((n, ATTN_WIDTH), F32),
        jax.ShapeDtypeStruct((n, ATTN_WIDTH), F32),
        jax.ShapeDtypeStruct((n, ATTN_WIDTH), BF16),
        jax.ShapeDtypeStruct((n, ATTN_WIDTH), F32),
        jax.ShapeDtypeStruct((n, 3 * DN_WIDTH), F32),
        jax.ShapeDtypeStruct((n, DN_WIDTH), F32),
        jax.ShapeDtypeStruct((n, GATE_PAD), F32),
    ]
    return pl.pallas_call(
        _inproj_kernel,
        grid=(n // tm,),
        in_specs=[
            pl.BlockSpec((tm, D_MODEL), row),
            pl.BlockSpec((1, D_MODEL), fixed),
            pl.BlockSpec((D_MODEL, MAIN_COLS), fixed),
            pl.BlockSpec((D_MODEL, GATE_PAD), fixed),
            pl.BlockSpec((1, ATTN_WIDTH), fixed),
            pl.BlockSpec((1, ATTN_WIDTH), fixed),
        ],
        out_specs=[pl.BlockSpec((tm, o.shape[1]), row) for o in outs],
        out_shape=outs,
        compiler_params=pltpu.CompilerParams(dimension_semantics=("arbitrary",), vmem_limit_bytes=VMEM_LIMIT),
        name="inproj",
    )(x2, norm_w, w_main, w_gate, qnw, knw)


def _t5_bias_tile_t(relb_ref, head, offset):
    c = lax.broadcasted_iota(jnp.int32, (MOBA_BLOCK, MOBA_BLOCK), 0)
    r = lax.broadcasted_iota(jnp.int32, (MOBA_BLOCK, MOBA_BLOCK), 1)
    dist = r - c + offset
    n = jnp.maximum(dist, 0)
    max_exact = REL_BUCKETS // 2
    nf = jnp.maximum(n, 1).astype(F32)
    large = max_exact + (jnp.log(nf / max_exact) / math.log(REL_MAX_DISTANCE / max_exact)
                         * (REL_BUCKETS - max_exact)).astype(jnp.int32)
    large = jnp.minimum(large, REL_BUCKETS - 1)
    bucket = jnp.where(n < max_exact, n, large)
    far = relb_ref[REL_BUCKETS - 1, head]
    bias = jnp.zeros((MOBA_BLOCK, MOBA_BLOCK), F32)
    for t in range(REL_BUCKETS):
        bias = jnp.where(bucket == t, relb_ref[t, head] - far, bias)
    return jnp.where(dist >= 0, bias * LOG2E, NEG_BIG)


def _moba_kernel(relb_ref, q_ref, qn_ref, k_ref, v_ref, z_ref, o_ref,
                 kaug_ref, vt_ref, kmean_ref, bias_ref, acc_ref, qa_ref, s_ref, *, nb):
    hp = pl.program_id(0)
    bi = pl.program_id(1)
    i = pl.program_id(2)
    par = i % 2
    HD = ATTN_HEAD_DIM
    lane = lax.broadcasted_iota(jnp.int32, (MOBA_BLOCK, 128), 1)
    lo_half = lane < HD
    er = lax.broadcasted_iota(jnp.int32, (128, 128), 0)
    ec = lax.broadcasted_iota(jnp.int32, (128, 128), 1)
    eye = jnp.where(er == ec, 1.0, 0.0).astype(BF16)
    blk = lax.broadcasted_iota(jnp.int32, (HD, MOBA_BLOCK), 0)

    def gate_scores(q):
        qt = _nt(eye, (q * (HD ** -0.5 * LOG2E)).astype(BF16))
        q0, q1 = _split2(q)
        gates = []
        for hh in range(2):
            k0, k1 = _split2(kmean_ref[hh])
            gates.append(_nt(k0, q0) + (_nt(k0, q1) + _nt(k1, q0)))
        return qt, gates

    def store_operands(slot, qt, gates, tile):
        for hh in range(2):
            g = jnp.where(blk < tile, gates[hh], -jnp.inf)
            sel = blk == tile
            for _ in range(MOBA_TOPK):
                mx = jnp.max(g, axis=0, keepdims=True)
                first = jnp.min(jnp.where(g == mx, blk, HD), axis=0, keepdims=True)
                hit = (blk == first) & (mx > -jnp.inf)
                sel = sel | hit
                g = jnp.where(hit, -jnp.inf, g)
            mask = jnp.where(sel, 0.0, NEG_BIG)
            rows = [qt[0:HD], mask] if hh == 0 else [mask, qt[HD:2 * HD]]
            qa_ref[slot, hh] = jnp.concatenate(rows, axis=0).astype(BF16)

    @pl.when(i == 0)
    def _prepare():
        kmean_ref[...] = jnp.zeros_like(kmean_ref)
        lane1 = lax.broadcasted_iota(jnp.int32, (1, 128), 1)
        ones_rows = jnp.ones((VT_ROWS - HD, MOBA_BLOCK), BF16)

        def prep(j, carry):
            rows = pl.ds(pl.multiple_of(j * MOBA_BLOCK, MOBA_BLOCK), MOBA_BLOCK)
            kb = k_ref[0, rows, :]
            km = jnp.mean(kb, axis=0, keepdims=True)
            kmean_ref[0, pl.ds(j, 1), :] = jnp.where(lane1 < HD, km, 0.0)
            kmean_ref[1, pl.ds(j, 1), :] = jnp.where(lane1 < HD, 0.0, km)
            kaug_ref[0, j] = jnp.where(lo_half, kb, jnp.where(lane == HD + j, 1.0, 0.0)).astype(BF16)
            kaug_ref[1, j] = jnp.where(lo_half, jnp.where(lane == j, 1.0, 0.0), kb).astype(BF16)
            vt = _nt(eye, v_ref[0, rows, :]).astype(BF16)
            for hh in range(2):
                vt_ref[j, hh] = jnp.concatenate([vt[hh * HD:(hh + 1) * HD], ones_rows], axis=0)
            return carry

        lax.fori_loop(0, nb, prep, 0)
        kaug_ref[0, nb] = jnp.where(lane == 2 * HD - 1, 1.0, 0.0).astype(BF16)
        kaug_ref[1, nb] = jnp.where(lane == HD - 1, 1.0, 0.0).astype(BF16)
        vt_ref[nb] = jnp.zeros((2, VT_ROWS, MOBA_BLOCK), BF16)
        store_operands(0, *gate_scores(q_ref[0]), 0)

    @pl.when((i == 0) & (bi == 0))
    def _bias_tiles():
        for hh in range(2):
            bias_ref[hh, 0] = _t5_bias_tile_t(relb_ref, 2 * hp + hh, 0)
            bias_ref[hh, 1] = _t5_bias_tile_t(relb_ref, 2 * hp + hh, MOBA_BLOCK)

    def issue(j, buf_ref):
        for hh in range(2):
            buf_ref[hh] = _nn(kaug_ref[hh, j], qa_ref[par, hh])

    def consume(j, buf_ref, st, bias_idx=None):
        out = []
        for hh in range(2):
            s = buf_ref[hh]
            if bias_idx is not None:
                s = s + bias_ref[hh, bias_idx]
            m_new = jnp.maximum(st[hh], jnp.max(s, axis=0, keepdims=True))
            p = jnp.exp2((s - m_new).astype(BF16))
            pv = _nn(vt_ref[j, hh], p)
            acc_ref[hh] = jnp.exp2(st[hh] - m_new) * acc_ref[hh] + pv
            out.append(m_new)
        return tuple(out)

    nfar = jnp.maximum(i - 1, 0)
    far = lambda j: jnp.where(j < nfar, j, nb)
    prev = jnp.where(i >= 1, i - 1, nb)
    a0_ref, b0_ref, a1_ref, b1_ref = (s_ref.at[n] for n in range(4))
    qt_next, gates_next = gate_scores(qn_ref[0])
    issue(i, a0_ref)
    issue(prev, b0_ref)
    issue(far(0), a1_ref)
    issue(far(1), b1_ref)
    store_operands(1 - par, qt_next, gates_next, i + 1)
    acc_ref[...] = jnp.zeros_like(acc_ref)
    st = (jnp.full((1, MOBA_BLOCK), 2 * NEG_BIG, F32),) * 2
    st = consume(i, a0_ref, st, bias_idx=0)
    issue(far(2), a0_ref)
    st = consume(prev, b0_ref, st, bias_idx=1)
    issue(far(3), b0_ref)
    st = consume(far(0), a1_ref, st)
    st = consume(far(1), b1_ref, st)

    def far_quad(j, st):
        issue(far(j + 2), a1_ref)
        issue(far(j + 3), b1_ref)
        st = consume(far(j), a0_ref, st)
        issue(far(j + 4), a0_ref)
        st = consume(far(j + 1), b0_ref, st)
        issue(far(j + 5), b0_ref)
        st = consume(far(j + 2), a1_ref, st)
        return consume(far(j + 3), b1_ref, st)

    nrest = jnp.maximum(nfar - 2, 0)
    n8 = nrest // 8
    st = lax.fori_loop(0, n8, lambda u, st: far_quad(8 * u + 6, far_quad(8 * u + 2, st)), st)
    n4 = (nrest - 8 * n8) // 4
    st = lax.fori_loop(0, n4, lambda u, st: far_quad(8 * n8 + 4 * u + 2, st), st)
    base = 8 * n8 + 4 * n4 + 2
    rem = nfar - base

    def rem_first(st):
        issue(far(base + 2), a1_ref)
        st = consume(far(base), a0_ref, st)
        return consume(far(base + 1), b0_ref, st)

    st = lax.cond(rem >= 1, rem_first, lambda st: st, st)
    lax.cond(rem >= 3, lambda st: consume(far(base + 2), a1_ref, st), lambda st: st, st)

    ot = jnp.concatenate([acc_ref[hh, 0:HD, :] / acc_ref[hh, HD:HD + 1, :] for hh in range(2)], axis=0)
    o_ref[0] = (ot.T * _silu(z_ref[0])).astype(o_ref.dtype)


def _moba(rel_bias, q, k, v, z):
    b, s, _ = q.shape
    nb = s // MOBA_BLOCK
    assert s % MOBA_BLOCK == 0 and nb < ATTN_HEAD_DIM
    pairs = ATTN_HEADS // 2
    blk = lambda hp, bi, i: (bi, i, hp)
    whole = lambda hp, bi, i: (bi, 0, hp)
    return pl.pallas_call(
        functools.partial(_moba_kernel, nb=nb),
        grid=(pairs, b, nb),
        in_specs=[
            pl.BlockSpec(memory_space=pltpu.SMEM),
            pl.BlockSpec((1, MOBA_BLOCK, 128), blk),
            pl.BlockSpec((1, MOBA_BLOCK, 128), lambda hp, bi, i: (bi, jnp.minimum(i + 1, nb - 1), hp)),
            pl.BlockSpec((1, s, 128), whole),
            pl.BlockSpec((1, s, 128), whole),
            pl.BlockSpec((1, MOBA_BLOCK, 128), blk),
        ],
        out_specs=pl.BlockSpec((1, MOBA_BLOCK, 128), blk),
        out_shape=jax.ShapeDtypeStruct((b, s, ATTN_WIDTH), BF16),
        scratch_shapes=[
            pltpu.VMEM((2, nb + 1, MOBA_BLOCK, 128), BF16),
            pltpu.VMEM((nb + 1, 2, VT_ROWS, MOBA_BLOCK), BF16),
            pltpu.VMEM((2, ATTN_HEAD_DIM, 128), F32),
            pltpu.VMEM((2, 2, MOBA_BLOCK, MOBA_BLOCK), F32),
            pltpu.VMEM((2, VT_ROWS, MOBA_BLOCK), F32),
            pltpu.VMEM((2, 2, 128, MOBA_BLOCK), BF16),
            pltpu.VMEM((4, 2, MOBA_BLOCK, MOBA_BLOCK), F32),
        ],
        compiler_params=pltpu.CompilerParams(
            dimension_semantics=("arbitrary", "arbitrary", "arbitrary"), vmem_limit_bytes=VMEM_LIMIT),
        name="moba",
    )(rel_bias, q, q, k, v, z)


DN_ROWS = DN_HEADS * DN_CHUNK
DN_PREC = dict(kk="bf16", qk="bf16", inv="bf16", uw="bf16", state="bf16", intra="bf16")


def _split2(a):
    a0 = a.astype(BF16)
    return a0, (a - a0.astype(F32)).astype(BF16)


def _split3(a):
    a0 = a.astype(BF16)
    r1 = a - a0.astype(F32)
    a1 = r1.astype(BF16)
    return a0, a1, (r1 - a1.astype(F32)).astype(BF16)


def _dot(a, b, prec, dims=(((1,), (0,)), ((), ()))):
    f = lambda x, y: lax.dot_general(x, y, dims, preferred_element_type=F32)
    if prec == "bf16":
        return f(a.astype(BF16), b.astype(BF16))
    assert prec == "x3"
    a0, a1 = _split2(a)
    b0, b1 = _split2(b)
    return f(a0, b0) + (f(a0, b1) + f(a1, b0))


_NT = (((1,), (1,)), ((), ()))
_TN = (((0,), (0,)), ((), ()))


def _dot_sel(sel, b):
    return sum(_nn(sel, p) for p in _split3(b))


def _deltanet_kernel(alog_ref, dtb_ref, x_ref, z_ref, g_ref, cw_ref, nw_ref, res_ref, ya_ref, wa_ref, wd_ref,
                     o_ref, state_ref, pad_ref, act_ref, rep_ref, yd_ref, *, tb):
    t = pl.program_id(1)
    C, R, W, H, D = DN_CHUNK, DN_ROWS, DN_WIDTH, DN_HEADS, DN_HEAD
    hist = DN_CONV_WIDTH - 1

    @pl.when(t == 0)
    def _reset():
        state_ref[...] = jnp.zeros_like(state_ref)
        pad_ref[0:8, :] = jnp.zeros((8, 3 * W), F32)

    pad_ref[8:8 + tb, :] = x_ref[0]
    for gcol in range(3 * H):
        cols = slice(gcol * D, (gcol + 1) * D)
        acc = pad_ref[pl.ds(8 - hist, tb), cols] * cw_ref[0:1, cols]
        for w in range(1, DN_CONV_WIDTH):
            acc = acc + pad_ref[pl.ds(8 - hist + w, tb), cols] * cw_ref[w:w + 1, cols]
        a = _silu(acc)
        if gcol < 2 * H:
            a = a * lax.rsqrt(jnp.sum(a * a, axis=-1, keepdims=True) + EPS)
            if gcol < H:
                a = a * (D ** -0.5)
        act_ref[:, cols] = a
    pad_ref[0:8, :] = x_ref[0, tb - 8:tb, :]

    lane1 = lax.broadcasted_iota(jnp.int32, (1, GATE_PAD), 1)
    alog_row = jnp.zeros((1, GATE_PAD), F32)
    dtb_row = jnp.zeros((1, GATE_PAD), F32)
    for h in range(H):
        alog_row = jnp.where(lane1 == H + h, alog_ref[h], alog_row)
        dtb_row = jnp.where(lane1 == H + h, dtb_ref[h], dtb_row)
    gates = g_ref[0]
    xs = gates + dtb_row
    softplus = jnp.maximum(xs, 0.0) + jnp.log(1.0 + jnp.exp(-jnp.abs(xs)))
    gval = jnp.where(lane1 < H, 1.0 / (1.0 + jnp.exp(-gates)), -jnp.exp(alog_row) * softplus)
    tr = lax.broadcasted_iota(jnp.int32, (tb, tb), 0)
    tc = lax.broadcasted_iota(jnp.int32, (tb, tb), 1)
    gcum = _dot_sel(jnp.where((tr // C == tc // C) & (tr >= tc), 1.0, 0.0).astype(BF16), gval)
    er = lax.broadcasted_iota(jnp.int32, (GATE_PAD, 2 * H * D), 0)
    ec = lax.broadcasted_iota(jnp.int32, (GATE_PAD, 2 * H * D), 1)
    spread = jnp.where(er == ec // D, 1.0, 0.0).astype(BF16)
    rep_ref[...] = sum(_nn(p, spread) for p in _split3(jnp.where(lane1 < H, gval, gcum)))

    r = lax.broadcasted_iota(jnp.int32, (R, R), 0)
    c = lax.broadcasted_iota(jnp.int32, (R, R), 1)
    same = (r // C) == (c // C)
    incl = same & (r >= c)
    strict = same & (r > c)
    eye = jnp.where(r == c, 1.0, 0.0)
    lane_r = lax.broadcasted_iota(jnp.int32, (R, D), 1)
    P = DN_PREC

    chunks = range(tb // C)
    each = lambda f, *seqs: [f(*a) for a in zip(*seqs)]
    rows = [slice(ci * C, (ci + 1) * C) for ci in chunks]
    stack = lambda ref, c0: [jnp.concatenate([ref[rw, c0 + D * h:c0 + D * (h + 1)] for h in range(H)], axis=0)
                             for rw in rows]
    qc, kc, vc = stack(act_ref, 0), stack(act_ref, W), stack(act_ref, 2 * W)
    beta = stack(rep_ref, 0)
    gc = stack(rep_ref, H * D)

    def pair_diff(g):
        p0, p1, p2 = (p.astype(F32) for p in _split3(g))
        u_m = jnp.where(lane_r == 0, p0, jnp.where(lane_r == 1, p1, jnp.where(lane_r == 2, p2,
                        jnp.where(lane_r < 6, 1.0, 0.0))))
        w_m = jnp.where(lane_r < 3, 1.0, jnp.where(lane_r == 3, -p0, jnp.where(lane_r == 4, -p1,
                        jnp.where(lane_r == 5, -p2, 0.0))))
        return _nt(u_m.astype(BF16), w_m.astype(BF16))

    decay = each(lambda g: jnp.where(incl, jnp.exp(jnp.where(incl, pair_diff(g), 0.0)), 0.0), gc)
    kb = each(lambda k, b: k * b, kc, beta)
    x = each(lambda a, k, d: -jnp.where(strict, _dot(a, k, P["kk"], _NT) * d, 0.0), kb, kc, decay)
    sq = lambda m: each(lambda a: _dot(a, a, P["inv"]), m)
    mul = lambda ma, mb: each(lambda a, b: _dot(a, b, P["inv"]), ma, mb)
    add = lambda ma, mb: each(lambda a, b: a + b, ma, mb)
    plus_eye = lambda m: each(lambda a: eye + a, m)
    x2 = sq(x)
    x4 = sq(x2)
    f1 = plus_eye(x)
    m1 = add(f1, mul(f1, x2))
    x8 = sq(x4)
    f4 = plus_eye(x4)
    m2 = add(f4, mul(f4, x8))
    x16 = sq(x8)
    m12 = mul(m1, m2)
    x32 = sq(x16)
    f16 = plus_eye(x16)
    m3 = add(f16, mul(f16, x32))
    tinv = mul(m12, m3)
    eg = each(jnp.exp, gc)
    uw = each(lambda t, v, b, k, e: _dot(t, jnp.concatenate([v * b, k * e], axis=1), P["uw"]),
              tinv, vc, beta, kb, eg)
    a_intra = each(lambda q, k, d: jnp.where(incl, _dot(q, k, P["qk"], _NT) * d, 0.0), qc, kc, decay)
    glast = [[g[h * C + C - 1:h * C + C, :] for h in range(H)] for g in gc]
    q_dec = each(lambda q, e: q * e, qc, eg)
    k_dec = each(lambda k, g, gl: k * jnp.exp(
        jnp.concatenate([jnp.broadcast_to(t, (C, D)) for t in gl], axis=0) - g), kc, gc, glast)

    hs = [slice(h * C, (h + 1) * C) for h in range(H)]
    for ci in chunks:
        state = [state_ref[h] for h in range(H)]
        wq = [_dot(jnp.concatenate([uw[ci][hs[h], D:2 * D], q_dec[ci][hs[h]]], axis=0), state[h], P["state"])
              for h in range(H)]
        vn = [uw[ci][hs[h], 0:D] - wq[h][0:C] for h in range(H)]
        for h in range(H):
            state_ref[h] = state[h] * jnp.exp(glast[ci][h]) + _dot(k_dec[ci][hs[h]], vn[h], P["state"], _TN)
        o = jnp.concatenate([wq[h][C:2 * C] for h in range(H)], axis=0) + _dot(
            a_intra[ci], jnp.concatenate(vn, axis=0), P["intra"])
        on = o * lax.rsqrt(jnp.mean(o * o, axis=-1, keepdims=True) + EPS) * nw_ref[...]
        for h in range(H):
            cols = slice(h * D, (h + 1) * D)
            yd_ref[rows[ci], cols] = (on[hs[h]] * _silu(z_ref[0, rows[ci], cols])).astype(yd_ref.dtype)

    o_ref[0] = res_ref[0] + _nn(ya_ref[0], wa_ref[...]) + _nn(yd_ref[...], wd_ref[...])


def _deltanet(a_log, dt_bias, qkv, z, gates, conv_w, dn_norm_w, x, ya, wa, wd, tb):
    b, s, _ = qkv.shape
    assert s % tb == 0 and tb % DN_CHUNK == 0
    blk = lambda bi, t: (bi, t, 0)
    fixed = lambda bi, t: (0, 0)
    return pl.pallas_call(
        functools.partial(_deltanet_kernel, tb=tb),
        grid=(b, s // tb),
        in_specs=[
            pl.BlockSpec(memory_space=pltpu.SMEM),
            pl.BlockSpec(memory_space=pltpu.SMEM),
            pl.BlockSpec((1, tb, 3 * DN_WIDTH), blk),
            pl.BlockSpec((1, tb, DN_WIDTH), blk),
            pl.BlockSpec((1, tb, GATE_PAD), blk),
            pl.BlockSpec((DN_CONV_WIDTH, 3 * DN_WIDTH), fixed),
            pl.BlockSpec((1, DN_HEAD), fixed),
            pl.BlockSpec((1, tb, D_MODEL), blk),
            pl.BlockSpec((1, tb, ATTN_WIDTH), blk),
            pl.BlockSpec((ATTN_WIDTH, D_MODEL), fixed),
            pl.BlockSpec((DN_WIDTH, D_MODEL), fixed),
        ],
        out_specs=pl.BlockSpec((1, tb, D_MODEL), blk),
        out_shape=jax.ShapeDtypeStruct((b, s, D_MODEL), F32),
        scratch_shapes=[
            pltpu.VMEM((DN_HEADS, DN_HEAD, DN_HEAD), F32),
            pltpu.VMEM((tb + 8, 3 * DN_WIDTH), F32),
            pltpu.VMEM((tb, 3 * DN_WIDTH), F32),
            pltpu.VMEM((tb, 2 * DN_HEADS * DN_HEAD), F32),
            pltpu.VMEM((tb, DN_WIDTH), BF16),
        ],
        compiler_params=pltpu.CompilerParams(
            dimension_semantics=("arbitrary", "arbitrary"), vmem_limit_bytes=VMEM_LIMIT),
        name="deltanet",
    )(a_log, dt_bias, qkv, z, gates, conv_w, dn_norm_w, x, ya, wa, wd)


def _row_tile(n):
    for tm in (512, 256, 128, 64, 32, 16, 8):
        if n % tm == 0:
            return tm
    raise ValueError(f"row count {n} is not a multiple of 8")


def kernel(x, rel_bias, norm_w, w_in, q_norm_w, k_norm_w, conv_w, a_log, dt_bias, dn_norm_w, w_out):
    b, s, d = x.shape
    assert d == D_MODEL and norm_w.shape[0] == 1, "single-layer kernel"
    n = b * s
    x2 = x.reshape(n, d)
    tm = _row_tile(n)

    w_main = w_in[0][:, :MAIN_COLS].astype(BF16)
    w_gate = jnp.pad(w_in[0][:, MAIN_COLS:], ((0, 0), (0, GATE_PAD - 2 * DN_HEADS))).astype(BF16)
    qnw = jnp.tile(q_norm_w[0], ATTN_HEADS)[None, :]
    knw = jnp.tile(k_norm_w[0], ATTN_HEADS)[None, :]

    q, k, v, za, qkv, zd, gates = _inproj(x2, norm_w, w_main, w_gate, qnw, knw, tm)

    r3 = lambda t: t.reshape(b, s, t.shape[-1])
    ya = _moba(rel_bias, r3(q), r3(k), r3(v), r3(za))
    tb = 256 if s % 256 == 0 else DN_CHUNK
    w_o = w_out[0].astype(BF16)
    return _deltanet(a_log[0], dt_bias[0], r3(qkv), r3(zd), r3(gates), conv_w[0], dn_norm_w,
                     x, ya, w_o[:ATTN_WIDTH], w_o[ATTN_WIDTH:], tb)
```

```python
import functools
import math

import jax
import jax.numpy as jnp
from jax import lax
from jax.experimental import pallas as pl
from jax.experimental.pallas import tpu as pltpu

F32 = jnp.float32
BF16 = jnp.bfloat16

D_MODEL = 1024
ATTN_HEADS = 8
ATTN_HEAD_DIM = 64
ATTN_WIDTH = ATTN_HEADS * ATTN_HEAD_DIM
MOBA_BLOCK = 256
MOBA_TOPK = 3
REL_BUCKETS = 32
REL_MAX_DISTANCE = 128
DN_HEADS = 4
DN_HEAD = 128
DN_WIDTH = DN_HEADS * DN_HEAD
DN_CONV_WIDTH = 4
DN_CHUNK = 64
MAIN_COLS = 4 * ATTN_WIDTH + 3 * DN_WIDTH + DN_WIDTH
GATE_PAD = 128
EPS = 1e-6
LOG2E = math.log2(math.e)
NEG_BIG = -32768.0
VT_ROWS = ATTN_HEAD_DIM + 16
VMEM_LIMIT = 56 * 1024 * 1024


def _nt(a, b):
    return lax.dot_general(a, b, (((1,), (1,)), ((), ())), preferred_element_type=F32)


def _nn(a, b):
    return lax.dot_general(a, b, (((1,), (0,)), ((), ())), preferred_element_type=F32)


def _silu(x):
    half = 0.5 * x
    return half + half * jnp.tanh(half)


def _inproj_kernel(x_ref, nw_ref, w_ref, wg_ref, qnw_ref, knw_ref,
                   q_ref, k_ref, v_ref, za_ref, qkv_ref, zd_ref, g_ref):
    x = x_ref[...]
    ms = jnp.mean(x * x, axis=-1, keepdims=True)
    h = (x * lax.rsqrt(ms + EPS) * nw_ref[...]).astype(BF16)

    r_i = lax.broadcasted_iota(jnp.int32, (ATTN_WIDTH, 128), 0)
    c_i = lax.broadcasted_iota(jnp.int32, (ATTN_WIDTH, 128), 1)
    ind = jnp.where(r_i // ATTN_HEAD_DIM == c_i, 1.0, 0.0).astype(BF16)
    r_e = lax.broadcasted_iota(jnp.int32, (128, ATTN_WIDTH), 0)
    c_e = lax.broadcasted_iota(jnp.int32, (128, ATTN_WIDTH), 1)
    expand = jnp.where(c_e // ATTN_HEAD_DIM == r_e, 1.0, 0.0).astype(BF16)

    def head_rms(t, w):
        ss = _nn((t * t).astype(BF16), ind)
        r = lax.rsqrt(ss * (1.0 / ATTN_HEAD_DIM) + EPS)
        r0, r1 = _split2(r)
        rf = _nn(r0, expand) + _nn(r1, expand)
        return t * rf * w

    def proj(c0, width):
        return _nn(h, w_ref[:, c0:c0 + width])

    q_ref[...] = head_rms(proj(0, ATTN_WIDTH), qnw_ref[...]).astype(q_ref.dtype)
    k_ref[...] = head_rms(proj(ATTN_WIDTH, ATTN_WIDTH), knw_ref[...]).astype(k_ref.dtype)
    v_ref[...] = proj(2 * ATTN_WIDTH, ATTN_WIDTH).astype(v_ref.dtype)
    za_ref[...] = proj(3 * ATTN_WIDTH, ATTN_WIDTH).astype(za_ref.dtype)
    for c in range(3):
        qkv_ref[:, c * DN_WIDTH:(c + 1) * DN_WIDTH] = proj(4 * ATTN_WIDTH + c * DN_WIDTH, DN_WIDTH).astype(qkv_ref.dtype)
    zd_ref[...] = proj(4 * ATTN_WIDTH + 3 * DN_WIDTH, DN_WIDTH).astype(zd_ref.dtype)
    g_ref[...] = _nn(h, wg_ref[...])


def _inproj(x2, norm_w, w_main, w_gate, qnw, knw, tm):
    n = x2.shape[0]
    row = lambda i: (i, 0)
    fixed = lambda i: (0, 0)
    outs = [
        jax.ShapeDtypeStruct((n, ATTN_WIDTH), F32),
        jax.ShapeDtypeStruct((n, ATTN_WIDTH), F32),
        jax.ShapeDtypeStruct((n, ATTN_WIDTH), BF16),
        jax.ShapeDtypeStruct((n, ATTN_WIDTH), F32),
        jax.ShapeDtypeStruct((n, 3 * DN_WIDTH), F32),
        jax.ShapeDtypeStruct((n, DN_WIDTH), F32),
        jax.ShapeDtypeStruct((n, GATE_PAD), F32),
    ]
    return pl.pallas_call(
        _inproj_kernel,
        grid=(n // tm,),
        in_specs=[
            pl.BlockSpec((tm, D_MODEL), row),
            pl.BlockSpec((1, D_MODEL), fixed),
            pl.BlockSpec((D_MODEL, MAIN_COLS), fixed),
            pl.BlockSpec((D_MODEL, GATE_PAD), fixed),
            pl.BlockSpec((1, ATTN_WIDTH), fixed),
            pl.BlockSpec((1, ATTN_WIDTH), fixed),
        ],
        out_specs=[pl.BlockSpec((tm, o.shape[1]), row) for o in outs],
        out_shape=outs,
        compiler_params=pltpu.CompilerParams(dimension_semantics=("arbitrary",), vmem_limit_bytes=VMEM_LIMIT),
        name="inproj",
    )(x2, norm_w, w_main, w_gate, qnw, knw)


def _t5_bias_tile_t(relb_ref, head, offset):
    c = lax.broadcasted_iota(jnp.int32, (MOBA_BLOCK, MOBA_BLOCK), 0)
    r = lax.broadcasted_iota(jnp.int32, (MOBA_BLOCK, MOBA_BLOCK), 1)
    dist = r - c + offset
    n = jnp.maximum(dist, 0)
    max_exact = REL_BUCKETS // 2
    nf = jnp.maximum(n, 1).astype(F32)
    large = max_exact + (jnp.log(nf / max_exact) / math.log(REL_MAX_DISTANCE / max_exact)
                         * (REL_BUCKETS - max_exact)).astype(jnp.int32)
    large = jnp.minimum(large, REL_BUCKETS - 1)
    bucket = jnp.where(n < max_exact, n, large)
    far = relb_ref[REL_BUCKETS - 1, head]
    bias = jnp.zeros((MOBA_BLOCK, MOBA_BLOCK), F32)
    for t in range(REL_BUCKETS):
        bias = jnp.where(bucket == t, relb_ref[t, head] - far, bias)
    return jnp.where(dist >= 0, bias * LOG2E, NEG_BIG)


def _moba_kernel(relb_ref, q_ref, qn_ref, k_ref, v_ref, z_ref, o_ref,
                 kaug_ref, vt_ref, kmean_ref, bias_ref, acc_ref, qa_ref, s_ref, *, nb):
    hp = pl.program_id(0)
    bi = pl.program_id(1)
    i = pl.program_id(2)
    par = i % 2
    HD = ATTN_HEAD_DIM
    lane = lax.broadcasted_iota(jnp.int32, (MOBA_BLOCK, 128), 1)
    lo_half = lane < HD
    er = lax.broadcasted_iota(jnp.int32, (128, 128), 0)
    ec = lax.broadcasted_iota(jnp.int32, (128, 128), 1)
    eye = jnp.where(er == ec, 1.0, 0.0).astype(BF16)
    blk = lax.broadcasted_iota(jnp.int32, (HD, MOBA_BLOCK), 0)

    def gate_scores(q):
        qt = _nt(eye, (q * (HD ** -0.5 * LOG2E)).astype(BF16))
        q0, q1 = _split2(q)
        gates = []
        for hh in range(2):
            k0, k1 = _split2(kmean_ref[hh])
            gates.append(_nt(k0, q0) + (_nt(k0, q1) + _nt(k1, q0)))
        return qt, gates

    def store_operands(slot, qt, gates, tile):
        for hh in range(2):
            g = jnp.where(blk < tile, gates[hh], -jnp.inf)
            sel = blk == tile
            for _ in range(MOBA_TOPK):
                mx = jnp.max(g, axis=0, keepdims=True)
                first = jnp.min(jnp.where(g == mx, blk, HD), axis=0, keepdims=True)
                hit = (blk == first) & (mx > -jnp.inf)
                sel = sel | hit
                g = jnp.where(hit, -jnp.inf, g)
            mask = jnp.where(sel, 0.0, NEG_BIG)
            rows = [qt[0:HD], mask] if hh == 0 else [mask, qt[HD:2 * HD]]
            qa_ref[slot, hh] = jnp.concatenate(rows, axis=0).astype(BF16)

    @pl.when(i == 0)
    def _prepare():
        kmean_ref[...] = jnp.zeros_like(kmean_ref)
        lane1 = lax.broadcasted_iota(jnp.int32, (1, 128), 1)
        ones_rows = jnp.ones((VT_ROWS - HD, MOBA_BLOCK), BF16)

        def prep(j, carry):
            rows = pl.ds(pl.multiple_of(j * MOBA_BLOCK, MOBA_BLOCK), MOBA_BLOCK)
            kb = k_ref[0, rows, :]
            km = jnp.mean(kb, axis=0, keepdims=True)
            kmean_ref[0, pl.ds(j, 1), :] = jnp.where(lane1 < HD, km, 0.0)
            kmean_ref[1, pl.ds(j, 1), :] = jnp.where(lane1 < HD, 0.0, km)
            kaug_ref[0, j] = jnp.where(lo_half, kb, jnp.where(lane == HD + j, 1.0, 0.0)).astype(BF16)
            kaug_ref[1, j] = jnp.where(lo_half, jnp.where(lane == j, 1.0, 0.0), kb).astype(BF16)
            vt = _nt(eye, v_ref[0, rows, :]).astype(BF16)
            for hh in range(2):
                vt_ref[j, hh] = jnp.concatenate([vt[hh * HD:(hh + 1) * HD], ones_rows], axis=0)
            return carry

        lax.fori_loop(0, nb, prep, 0, unroll=4)
        kaug_ref[0, nb] = jnp.where(lane == 2 * HD - 1, 1.0, 0.0).astype(BF16)
        kaug_ref[1, nb] = jnp.where(lane == HD - 1, 1.0, 0.0).astype(BF16)
        vt_ref[nb] = jnp.zeros((2, VT_ROWS, MOBA_BLOCK), BF16)
        store_operands(0, *gate_scores(q_ref[0]), 0)

    @pl.when((i == 0) & (bi == 0))
    def _bias_tiles():
        for hh in range(2):
            bias_ref[hh, 0] = _t5_bias_tile_t(relb_ref, 2 * hp + hh, 0)
            bias_ref[hh, 1] = _t5_bias_tile_t(relb_ref, 2 * hp + hh, MOBA_BLOCK)

    def issue(j, buf_ref):
        for hh in range(2):
            buf_ref[hh] = _nn(kaug_ref[hh, j], qa_ref[par, hh])

    def consume(j, buf_ref, st, bias_idx=None):
        out = []
        for hh in range(2):
            s = buf_ref[hh]
            if bias_idx is not None:
                s = s + bias_ref[hh, bias_idx]
            m_new = jnp.maximum(st[hh], jnp.max(s, axis=0, keepdims=True))
            p = jnp.exp2((s - m_new).astype(BF16))
            pv = _nn(vt_ref[j, hh], p)
            acc_ref[hh] = jnp.exp2(st[hh] - m_new) * acc_ref[hh] + pv
            out.append(m_new)
        return tuple(out)

    nfar = jnp.maximum(i - 1, 0)
    far = lambda j: jnp.where(j < nfar, j, nb)
    prev = jnp.where(i >= 1, i - 1, nb)
    a0_ref, b0_ref, a1_ref, b1_ref = (s_ref.at[n] for n in range(4))
    qt_next, gates_next = gate_scores(qn_ref[0])
    issue(i, a0_ref)
    issue(prev, b0_ref)
    issue(far(0), a1_ref)
    issue(far(1), b1_ref)
    store_operands(1 - par, qt_next, gates_next, i + 1)
    acc_ref[...] = jnp.zeros_like(acc_ref)
    st = (jnp.full((1, MOBA_BLOCK), 2 * NEG_BIG, F32),) * 2
    st = consume(i, a0_ref, st, bias_idx=0)
    issue(far(2), a0_ref)
    st = consume(prev, b0_ref, st, bias_idx=1)
    issue(far(3), b0_ref)
    st = consume(far(0), a1_ref, st)
    st = consume(far(1), b1_ref, st)

    def far_quad(j, st):
        issue(far(j + 2), a1_ref)
        issue(far(j + 3), b1_ref)
        st = consume(far(j), a0_ref, st)
        issue(far(j + 4), a0_ref)
        st = consume(far(j + 1), b0_ref, st)
        issue(far(j + 5), b0_ref)
        st = consume(far(j + 2), a1_ref, st)
        return consume(far(j + 3), b1_ref, st)

    nrest = jnp.maximum(nfar - 2, 0)
    n8 = nrest // 8
    st = lax.fori_loop(0, n8, lambda u, st: far_quad(8 * u + 6, far_quad(8 * u + 2, st)), st)
    n4 = (nrest - 8 * n8) // 4
    st = lax.fori_loop(0, n4, lambda u, st: far_quad(8 * n8 + 4 * u + 2, st), st)
    base = 8 * n8 + 4 * n4 + 2
    rem = nfar - base

    def rem_first(st):
        issue(far(base + 2), a1_ref)
        st = consume(far(base), a0_ref, st)
        return consume(far(base + 1), b0_ref, st)

    st = lax.cond(rem >= 1, rem_first, lambda st: st, st)
    lax.cond(rem >= 3, lambda st: consume(far(base + 2), a1_ref, st), lambda st: st, st)

    ot = jnp.concatenate([acc_ref[hh, 0:HD, :] / acc_ref[hh, HD:HD + 1, :] for hh in range(2)], axis=0)
    o_ref[0] = (ot.T * _silu(z_ref[0])).astype(o_ref.dtype)


def _moba(rel_bias, q, k, v, z):
    b, s, _ = q.shape
    nb = s // MOBA_BLOCK
    assert s % MOBA_BLOCK == 0 and nb < ATTN_HEAD_DIM
    pairs = ATTN_HEADS // 2
    blk = lambda hp, bi, i: (bi, i, hp)
    whole = lambda hp, bi, i: (bi, 0, hp)
    return pl.pallas_call(
        functools.partial(_moba_kernel, nb=nb),
        grid=(pairs, b, nb),
        in_specs=[
            pl.BlockSpec(memory_space=pltpu.SMEM),
            pl.BlockSpec((1, MOBA_BLOCK, 128), blk),
            pl.BlockSpec((1, MOBA_BLOCK, 128), lambda hp, bi, i: (bi, jnp.minimum(i + 1, nb - 1), hp)),
            pl.BlockSpec((1, s, 128), whole),
            pl.BlockSpec((1, s, 128), whole),
            pl.BlockSpec((1, MOBA_BLOCK, 128), blk),
        ],
        out_specs=pl.BlockSpec((1, MOBA_BLOCK, 128), blk),
        out_shape=jax.ShapeDtypeStruct((b, s, ATTN_WIDTH), BF16),
        scratch_shapes=[
            pltpu.VMEM((2, nb + 1, MOBA_BLOCK, 128), BF16),
            pltpu.VMEM((nb + 1, 2, VT_ROWS, MOBA_BLOCK), BF16),
            pltpu.VMEM((2, ATTN_HEAD_DIM, 128), F32),
            pltpu.VMEM((2, 2, MOBA_BLOCK, MOBA_BLOCK), F32),
            pltpu.VMEM((2, VT_ROWS, MOBA_BLOCK), F32),
            pltpu.VMEM((2, 2, 128, MOBA_BLOCK), BF16),
            pltpu.VMEM((4, 2, MOBA_BLOCK, MOBA_BLOCK), F32),
        ],
        compiler_params=pltpu.CompilerParams(
            dimension_semantics=("arbitrary", "arbitrary", "arbitrary"), vmem_limit_bytes=VMEM_LIMIT),
        name="moba",
    )(rel_bias, q, q, k, v, z)


DN_ROWS = DN_HEADS * DN_CHUNK


def _split2(a):
    a0 = a.astype(BF16)
    return a0, (a - a0.astype(F32)).astype(BF16)


def _split3(a):
    a0 = a.astype(BF16)
    r1 = a - a0.astype(F32)
    a1 = r1.astype(BF16)
    return a0, a1, (r1 - a1.astype(F32)).astype(BF16)


def _dot(a, b, dims=(((1,), (0,)), ((), ()))):
    return lax.dot_general(a.astype(BF16), b.astype(BF16), dims, preferred_element_type=F32)


_NT = (((1,), (1,)), ((), ()))
_TN = (((0,), (0,)), ((), ()))


def _dot_sel(sel, b):
    return sum(_nn(sel, p) for p in _split3(b))


def _deltanet_kernel(alog_ref, dtb_ref, x_ref, z_ref, g_ref, cw_ref, nw_ref, res_ref, ya_ref, wa_ref, wd_ref,
                     o_ref, state_ref, pad_ref, act_ref, rep_ref, yd_ref, *, tb):
    t = pl.program_id(1)
    C, R, W, H, D = DN_CHUNK, DN_ROWS, DN_WIDTH, DN_HEADS, DN_HEAD
    hist = DN_CONV_WIDTH - 1

    @pl.when(t == 0)
    def _reset():
        state_ref[...] = jnp.zeros_like(state_ref)
        pad_ref[0:8, :] = jnp.zeros((8, 3 * W), F32)

    pad_ref[8:8 + tb, :] = x_ref[0]
    for gcol in range(3 * H):
        cols = slice(gcol * D, (gcol + 1) * D)
        acc = pad_ref[pl.ds(8 - hist, tb), cols] * cw_ref[0:1, cols]
        for w in range(1, DN_CONV_WIDTH):
            acc = acc + pad_ref[pl.ds(8 - hist + w, tb), cols] * cw_ref[w:w + 1, cols]
        a = _silu(acc)
        if gcol < 2 * H:
            a = a * lax.rsqrt(jnp.sum(a * a, axis=-1, keepdims=True) + EPS)
            if gcol < H:
                a = a * (D ** -0.5)
        act_ref[:, cols] = a
    pad_ref[0:8, :] = x_ref[0, tb - 8:tb, :]

    lane1 = lax.broadcasted_iota(jnp.int32, (1, GATE_PAD), 1)
    alog_row = jnp.zeros((1, GATE_PAD), F32)
    dtb_row = jnp.zeros((1, GATE_PAD), F32)
    for h in range(H):
        alog_row = jnp.where(lane1 == H + h, alog_ref[h], alog_row)
        dtb_row = jnp.where(lane1 == H + h, dtb_ref[h], dtb_row)
    gates = g_ref[0]
    xs = gates + dtb_row
    softplus = jnp.maximum(xs, 0.0) + jnp.log(1.0 + jnp.exp(-jnp.abs(xs)))
    gval = jnp.where(lane1 < H, 1.0 / (1.0 + jnp.exp(-gates)), -jnp.exp(alog_row) * softplus)
    tr = lax.broadcasted_iota(jnp.int32, (tb, tb), 0)
    tc = lax.broadcasted_iota(jnp.int32, (tb, tb), 1)
    gcum = _dot_sel(jnp.where((tr // C == tc // C) & (tr >= tc), 1.0, 0.0).astype(BF16), gval)
    er = lax.broadcasted_iota(jnp.int32, (GATE_PAD, 2 * H * D), 0)
    ec = lax.broadcasted_iota(jnp.int32, (GATE_PAD, 2 * H * D), 1)
    spread = jnp.where(er == ec // D, 1.0, 0.0).astype(BF16)
    rep_ref[...] = sum(_nn(p, spread) for p in _split3(jnp.where(lane1 < H, gval, gcum)))

    r = lax.broadcasted_iota(jnp.int32, (R, R), 0)
    c = lax.broadcasted_iota(jnp.int32, (R, R), 1)
    same = (r // C) == (c // C)
    incl = same & (r >= c)
    strict = same & (r > c)
    eye = jnp.where(r == c, 1.0, 0.0)
    lane_r = lax.broadcasted_iota(jnp.int32, (R, D), 1)

    chunks = range(tb // C)
    each = lambda f, *seqs: [f(*a) for a in zip(*seqs)]
    rows = [slice(ci * C, (ci + 1) * C) for ci in chunks]
    stack = lambda ref, c0: [jnp.concatenate([ref[rw, c0 + D * h:c0 + D * (h + 1)] for h in range(H)], axis=0)
                             for rw in rows]
    qc, kc, vc = stack(act_ref, 0), stack(act_ref, W), stack(act_ref, 2 * W)
    beta = stack(rep_ref, 0)
    gc = stack(rep_ref, H * D)

    def pair_diff(g):
        p0, p1, p2 = (p.astype(F32) for p in _split3(g))
        u_m = jnp.where(lane_r == 0, p0, jnp.where(lane_r == 1, p1, jnp.where(lane_r == 2, p2,
                        jnp.where(lane_r < 6, 1.0, 0.0))))
        w_m = jnp.where(lane_r < 3, 1.0, jnp.where(lane_r == 3, -p0, jnp.where(lane_r == 4, -p1,
                        jnp.where(lane_r == 5, -p2, 0.0))))
        return _nt(u_m.astype(BF16), w_m.astype(BF16))

    decay = each(lambda g: jnp.where(incl, jnp.exp(jnp.where(incl, pair_diff(g), 0.0)), 0.0), gc)
    kb = each(lambda k, b: k * b, kc, beta)
    x = each(lambda a, k, d: -jnp.where(strict, _dot(a, k, _NT) * d, 0.0), kb, kc, decay)
    sq = lambda m: each(lambda a: _dot(a, a), m)
    mul = lambda ma, mb: each(lambda a, b: _dot(a, b), ma, mb)
    add = lambda ma, mb: each(lambda a, b: a + b, ma, mb)
    plus_eye = lambda m: each(lambda a: eye + a, m)
    x2 = sq(x)
    x4 = sq(x2)
    f1 = plus_eye(x)
    m1 = add(f1, mul(f1, x2))
    x8 = sq(x4)
    f4 = plus_eye(x4)
    m2 = add(f4, mul(f4, x8))
    x16 = sq(x8)
    m12 = mul(m1, m2)
    x32 = sq(x16)
    f16 = plus_eye(x16)
    m3 = add(f16, mul(f16, x32))
    tinv = mul(m12, m3)
    eg = each(jnp.exp, gc)
    uw = each(lambda t, v, b, k, e: _dot(t, jnp.concatenate([v * b, k * e], axis=1)),
              tinv, vc, beta, kb, eg)
    a_intra = each(lambda q, k, d: jnp.where(incl, _dot(q, k, _NT) * d, 0.0), qc, kc, decay)
    glast = [[g[h * C + C - 1:h * C + C, :] for h in range(H)] for g in gc]
    q_dec = each(lambda q, e: q * e, qc, eg)
    k_dec = each(lambda k, g, gl: k * jnp.exp(
        jnp.concatenate([jnp.broadcast_to(t, (C, D)) for t in gl], axis=0) - g), kc, gc, glast)

    hs = [slice(h * C, (h + 1) * C) for h in range(H)]
    for ci in chunks:
        state = [state_ref[h] for h in range(H)]
        wq = [_dot(jnp.concatenate([uw[ci][hs[h], D:2 * D], q_dec[ci][hs[h]]], axis=0), state[h])
              for h in range(H)]
        vn = [uw[ci][hs[h], 0:D] - wq[h][0:C] for h in range(H)]
        for h in range(H):
            state_ref[h] = state[h] * jnp.exp(glast[ci][h]) + _dot(k_dec[ci][hs[h]], vn[h], _TN)
        o = jnp.concatenate([wq[h][C:2 * C] for h in range(H)], axis=0) + _dot(
            a_intra[ci], jnp.concatenate(vn, axis=0))
        on = o * lax.rsqrt(jnp.mean(o * o, axis=-1, keepdims=True) + EPS) * nw_ref[...]
        for h in range(H):
            cols = slice(h * D, (h + 1) * D)
            yd_ref[rows[ci], cols] = (on[hs[h]] * _silu(z_ref[0, rows[ci], cols])).astype(yd_ref.dtype)

    o_ref[0] = res_ref[0] + _nn(ya_ref[0], wa_ref[...]) + _nn(yd_ref[...], wd_ref[...])


def _deltanet(a_log, dt_bias, qkv, z, gates, conv_w, dn_norm_w, x, ya, wa, wd, tb):
    b, s, _ = qkv.shape
    assert s % tb == 0 and tb % DN_CHUNK == 0
    blk = lambda bi, t: (bi, t, 0)
    fixed = lambda bi, t: (0, 0)
    return pl.pallas_call(
        functools.partial(_deltanet_kernel, tb=tb),
        grid=(b, s // tb),
        in_specs=[
            pl.BlockSpec(memory_space=pltpu.SMEM),
            pl.BlockSpec(memory_space=pltpu.SMEM),
            pl.BlockSpec((1, tb, 3 * DN_WIDTH), blk),
            pl.BlockSpec((1, tb, DN_WIDTH), blk),
            pl.BlockSpec((1, tb, GATE_PAD), blk),
            pl.BlockSpec((DN_CONV_WIDTH, 3 * DN_WIDTH), fixed),
            pl.BlockSpec((1, DN_HEAD), fixed),
            pl.BlockSpec((1, tb, D_MODEL), blk),
            pl.BlockSpec((1, tb, ATTN_WIDTH), blk),
            pl.BlockSpec((ATTN_WIDTH, D_MODEL), fixed),
            pl.BlockSpec((DN_WIDTH, D_MODEL), fixed),
        ],
        out_specs=pl.BlockSpec((1, tb, D_MODEL), blk),
        out_shape=jax.ShapeDtypeStruct((b, s, D_MODEL), F32),
        scratch_shapes=[
            pltpu.VMEM((DN_HEADS, DN_HEAD, DN_HEAD), F32),
            pltpu.VMEM((tb + 8, 3 * DN_WIDTH), F32),
            pltpu.VMEM((tb, 3 * DN_WIDTH), F32),
            pltpu.VMEM((tb, 2 * DN_HEADS * DN_HEAD), F32),
            pltpu.VMEM((tb, DN_WIDTH), BF16),
        ],
        compiler_params=pltpu.CompilerParams(
            dimension_semantics=("arbitrary", "arbitrary"), vmem_limit_bytes=VMEM_LIMIT),
        name="deltanet",
    )(a_log, dt_bias, qkv, z, gates, conv_w, dn_norm_w, x, ya, wa, wd)


def _row_tile(n):
    for tm in (512, 256, 128, 64, 32, 16, 8):
        if n % tm == 0:
            return tm
    raise ValueError(f"row count {n} is not a multiple of 8")


def kernel(x, rel_bias, norm_w, w_in, q_norm_w, k_norm_w, conv_w, a_log, dt_bias, dn_norm_w, w_out):
    b, s, d = x.shape
    assert d == D_MODEL and norm_w.shape[0] == 1, "single-layer kernel"
    n = b * s
    x2 = x.reshape(n, d)
    tm = _row_tile(n)

    w_main = w_in[0][:, :MAIN_COLS].astype(BF16)
    w_gate = jnp.pad(w_in[0][:, MAIN_COLS:], ((0, 0), (0, GATE_PAD - 2 * DN_HEADS))).astype(BF16)
    qnw = jnp.tile(q_norm_w[0], ATTN_HEADS)[None, :]
    knw = jnp.tile(k_norm_w[0], ATTN_HEADS)[None, :]

    q, k, v, za, qkv, zd, gates = _inproj(x2, norm_w, w_main, w_gate, qnw, knw, tm)

    r3 = lambda t: t.reshape(b, s, t.shape[-1])
    ya = _moba(rel_bias, r3(q), r3(k), r3(v), r3(za))
    tb = 256 if s % 256 == 0 else DN_CHUNK
    w_o = w_out[0].astype(BF16)
    return _deltanet(a_log[0], dt_bias[0], r3(qkv), r3(zd), r3(gates), conv_w[0], dn_norm_w,
                     x, ya, w_o[:ATTN_WIDTH], w_o[ATTN_WIDTH:], tb)
```

```python
import functools
import math

import jax
import jax.numpy as jnp
from jax import lax
from jax.experimental import pallas as pl
from jax.experimental.pallas import tpu as pltpu

F32 = jnp.float32
BF16 = jnp.bfloat16

D_MODEL = 1024
ATTN_HEADS = 8
ATTN_HEAD_DIM = 64
ATTN_WIDTH = ATTN_HEADS * ATTN_HEAD_DIM
MOBA_BLOCK = 256
MOBA_TOPK = 3
REL_BUCKETS = 32
REL_MAX_DISTANCE = 128
DN_HEADS = 4
DN_HEAD = 128
DN_WIDTH = DN_HEADS * DN_HEAD
DN_CONV_WIDTH = 4
DN_CHUNK = 64
MAIN_COLS = 4 * ATTN_WIDTH + 3 * DN_WIDTH + DN_WIDTH
GATE_PAD = 128
EPS = 1e-6
LOG2E = math.log2(math.e)
NEG_BIG = -32768.0
VT_ROWS = ATTN_HEAD_DIM + 16
VMEM_LIMIT = 56 * 1024 * 1024


def _nt(a, b):
    return lax.dot_general(a, b, (((1,), (1,)), ((), ())), preferred_element_type=F32)


def _nn(a, b):
    return lax.dot_general(a, b, (((1,), (0,)), ((), ())), preferred_element_type=F32)


def _silu(x):
    half = 0.5 * x
    return half + half * jnp.tanh(half)


def _inproj_kernel(x_ref, nw_ref, w_ref, wg_ref, qnw_ref, knw_ref,
                   q_ref, k_ref, v_ref, za_ref, qkv_ref, zd_ref, g_ref):
    x = x_ref[...]
    ms = jnp.mean(x * x, axis=-1, keepdims=True)
    h = (x * lax.rsqrt(ms + EPS) * nw_ref[...]).astype(BF16)

    r_i = lax.broadcasted_iota(jnp.int32, (ATTN_WIDTH, 128), 0)
    c_i = lax.broadcasted_iota(jnp.int32, (ATTN_WIDTH, 128), 1)
    ind = jnp.where(r_i // ATTN_HEAD_DIM == c_i, 1.0, 0.0).astype(BF16)
    r_e = lax.broadcasted_iota(jnp.int32, (128, ATTN_WIDTH), 0)
    c_e = lax.broadcasted_iota(jnp.int32, (128, ATTN_WIDTH), 1)
    expand = jnp.where(c_e // ATTN_HEAD_DIM == r_e, 1.0, 0.0).astype(BF16)

    def head_rms(t, w):
        ss = _nn((t * t).astype(BF16), ind)
        r = lax.rsqrt(ss * (1.0 / ATTN_HEAD_DIM) + EPS)
        r0, r1 = _split2(r)
        rf = _nn(r0, expand) + _nn(r1, expand)
        return t * rf * w

    def proj(c0, width):
        return _nn(h, w_ref[:, c0:c0 + width])

    q_ref[...] = head_rms(proj(0, ATTN_WIDTH), qnw_ref[...]).astype(q_ref.dtype)
    k_ref[...] = head_rms(proj(ATTN_WIDTH, ATTN_WIDTH), knw_ref[...]).astype(k_ref.dtype)
    v_ref[...] = proj(2 * ATTN_WIDTH, ATTN_WIDTH).astype(v_ref.dtype)
    za_ref[...] = proj(3 * ATTN_WIDTH, ATTN_WIDTH).astype(za_ref.dtype)
    for c in range(3):
        qkv_ref[:, c * DN_WIDTH:(c + 1) * DN_WIDTH] = proj(4 * ATTN_WIDTH + c * DN_WIDTH, DN_WIDTH).astype(qkv_ref.dtype)
    zd_ref[...] = proj(4 * ATTN_WIDTH + 3 * DN_WIDTH, DN_WIDTH).astype(zd_ref.dtype)
    g_ref[...] = _nn(h, wg_ref[...])


def _inproj(x2, norm_w, w_main, w_gate, qnw, knw, tm):
    n = x2.shape[0]
    row = lambda i: (i, 0)
    fixed = lambda i: (0, 0)
    outs = [
        jax.ShapeDtypeStruct((n, ATTN_WIDTH), F32),
        jax.ShapeDtypeStruct((n, ATTN_WIDTH), F32),
        jax.ShapeDtypeStruct((n, ATTN_WIDTH), BF16),
        jax.ShapeDtypeStruct((n, ATTN_WIDTH), F32),
        jax.ShapeDtypeStruct((n, 3 * DN_WIDTH), F32),
        jax.ShapeDtypeStruct((n, DN_WIDTH), F32),
        jax.ShapeDtypeStruct((n, GATE_PAD), F32),
    ]
    return pl.pallas_call(
        _inproj_kernel,
        grid=(n // tm,),
        in_specs=[
            pl.BlockSpec((tm, D_MODEL), row),
            pl.BlockSpec((1, D_MODEL), fixed),
            pl.BlockSpec((D_MODEL, MAIN_COLS), fixed),
            pl.BlockSpec((D_MODEL, GATE_PAD), fixed),
            pl.BlockSpec((1, ATTN_WIDTH), fixed),
            pl.BlockSpec((1, ATTN_WIDTH), fixed),
        ],
        out_specs=[pl.BlockSpec((tm, o.shape[1]), row) for o in outs],
        out_shape=outs,
        compiler_params=pltpu.CompilerParams(dimension_semantics=("arbitrary",), vmem_limit_bytes=VMEM_LIMIT),
        name="inproj",
    )(x2, norm_w, w_main, w_gate, qnw, knw)


def _t5_bias_tile_t(relb_ref, head, offset):
    c = lax.broadcasted_iota(jnp.int32, (MOBA_BLOCK, MOBA_BLOCK), 0)
    r = lax.broadcasted_iota(jnp.int32, (MOBA_BLOCK, MOBA_BLOCK), 1)
    dist = r - c + offset
    n = jnp.maximum(dist, 0)
    max_exact = REL_BUCKETS // 2
    nf = jnp.maximum(n, 1).astype(F32)
    large = max_exact + (jnp.log(nf / max_exact) / math.log(REL_MAX_DISTANCE / max_exact)
                         * (REL_BUCKETS - max_exact)).astype(jnp.int32)
    large = jnp.minimum(large, REL_BUCKETS - 1)
    bucket = jnp.where(n < max_exact, n, large)
    far = relb_ref[REL_BUCKETS - 1, head]
    bias = jnp.zeros((MOBA_BLOCK, MOBA_BLOCK), F32)
    for t in range(REL_BUCKETS):
        bias = jnp.where(bucket == t, relb_ref[t, head] - far, bias)
    return jnp.where(dist >= 0, bias * LOG2E, NEG_BIG)


def _moba_kernel(relb_ref, q_ref, qn_ref, k_ref, v_ref, z_ref, o_ref,
                 kaug_ref, vt_ref, kmean_ref, bias_ref, acc_ref, qa_ref, s_ref, *, nb):
    hp = pl.program_id(0)
    bi = pl.program_id(1)
    i = pl.program_id(2)
    par = i % 2
    HD = ATTN_HEAD_DIM
    lane = lax.broadcasted_iota(jnp.int32, (MOBA_BLOCK, 128), 1)
    lo_half = lane < HD
    er = lax.broadcasted_iota(jnp.int32, (128, 128), 0)
    ec = lax.broadcasted_iota(jnp.int32, (128, 128), 1)
    eye = jnp.where(er == ec, 1.0, 0.0).astype(BF16)
    blk = lax.broadcasted_iota(jnp.int32, (HD, MOBA_BLOCK), 0)

    def gate_scores(q):
        qt = _nt(eye, (q * (HD ** -0.5 * LOG2E)).astype(BF16))
        q0, q1 = _split2(q)
        gates = []
        for hh in range(2):
            k0, k1 = _split2(kmean_ref[hh])
            gates.append(_nt(k0, q0) + (_nt(k0, q1) + _nt(k1, q0)))
        return qt, gates

    def store_operands(slot, qt, gates, tile):
        for hh in range(2):
            g = jnp.where(blk < tile, gates[hh], -jnp.inf)
            sel = blk == tile
            for _ in range(MOBA_TOPK):
                mx = jnp.max(g, axis=0, keepdims=True)
                first = jnp.min(jnp.where(g == mx, blk, HD), axis=0, keepdims=True)
                hit = (blk == first) & (mx > -jnp.inf)
                sel = sel | hit
                g = jnp.where(hit, -jnp.inf, g)
            mask = jnp.where(sel, 0.0, NEG_BIG)
            rows = [qt[0:HD], mask] if hh == 0 else [mask, qt[HD:2 * HD]]
            qa_ref[slot, hh] = jnp.concatenate(rows, axis=0).astype(BF16)

    @pl.when(i == 0)
    def _prepare():
        kmean_ref[...] = jnp.zeros_like(kmean_ref)
        lane1 = lax.broadcasted_iota(jnp.int32, (1, 128), 1)
        ones_rows = jnp.ones((VT_ROWS - HD, MOBA_BLOCK), BF16)

        def prep(j, carry):
            rows = pl.ds(pl.multiple_of(j * MOBA_BLOCK, MOBA_BLOCK), MOBA_BLOCK)
            kb = k_ref[0, rows, :]
            km = jnp.mean(kb, axis=0, keepdims=True)
            kmean_ref[0, pl.ds(j, 1), :] = jnp.where(lane1 < HD, km, 0.0)
            kmean_ref[1, pl.ds(j, 1), :] = jnp.where(lane1 < HD, 0.0, km)
            kaug_ref[0, j] = jnp.where(lo_half, kb, jnp.where(lane == HD + j, 1.0, 0.0)).astype(BF16)
            kaug_ref[1, j] = jnp.where(lo_half, jnp.where(lane == j, 1.0, 0.0), kb).astype(BF16)
            vt = _nt(eye, v_ref[0, rows, :]).astype(BF16)
            for hh in range(2):
                vt_ref[j, hh] = jnp.concatenate([vt[hh * HD:(hh + 1) * HD], ones_rows], axis=0)
            return carry

        lax.fori_loop(0, nb, prep, 0, unroll=4)
        kaug_ref[0, nb] = jnp.where(lane == 2 * HD - 1, 1.0, 0.0).astype(BF16)
        kaug_ref[1, nb] = jnp.where(lane == HD - 1, 1.0, 0.0).astype(BF16)
        vt_ref[nb] = jnp.zeros((2, VT_ROWS, MOBA_BLOCK), BF16)
        store_operands(0, *gate_scores(q_ref[0]), 0)

    @pl.when((i == 0) & (bi == 0))
    def _bias_tiles():
        for hh in range(2):
            bias_ref[hh, 0] = _t5_bias_tile_t(relb_ref, 2 * hp + hh, 0)
            bias_ref[hh, 1] = _t5_bias_tile_t(relb_ref, 2 * hp + hh, MOBA_BLOCK)

    def issue(j, buf_ref):
        for hh in range(2):
            buf_ref[hh] = _nn(kaug_ref[hh, j], qa_ref[par, hh])

    def consume(j, buf_ref, st, bias_idx=None):
        out = []
        for hh in range(2):
            s = buf_ref[hh]
            if bias_idx is not None:
                s = s + bias_ref[hh, bias_idx]
            m_new = jnp.maximum(st[hh], jnp.max(s, axis=0, keepdims=True))
            p = jnp.exp2((s - m_new).astype(BF16))
            pv = _nn(vt_ref[j, hh], p)
            acc_ref[hh] = jnp.exp2(st[hh] - m_new) * acc_ref[hh] + pv
            out.append(m_new)
        return tuple(out)

    nfar = jnp.maximum(i - 1, 0)
    far = lambda j: jnp.where(j < nfar, j, nb)
    prev = jnp.where(i >= 1, i - 1, nb)
    a0_ref, b0_ref, a1_ref, b1_ref = (s_ref.at[n] for n in range(4))
    qt_next, gates_next = gate_scores(qn_ref[0])
    issue(i, a0_ref)
    issue(prev, b0_ref)
    issue(far(0), a1_ref)
    issue(far(1), b1_ref)
    store_operands(1 - par, qt_next, gates_next, i + 1)
    acc_ref[...] = jnp.zeros_like(acc_ref)
    st = (jnp.full((1, MOBA_BLOCK), 2 * NEG_BIG, F32),) * 2
    st = consume(i, a0_ref, st, bias_idx=0)
    issue(far(2), a0_ref)
    st = consume(prev, b0_ref, st, bias_idx=1)
    issue(far(3), b0_ref)
    st = consume(far(0), a1_ref, st)
    st = consume(far(1), b1_ref, st)

    def far_quad(j, st):
        issue(far(j + 2), a1_ref)
        issue(far(j + 3), b1_ref)
        st = consume(far(j), a0_ref, st)
        issue(far(j + 4), a0_ref)
        st = consume(far(j + 1), b0_ref, st)
        issue(far(j + 5), b0_ref)
        st = consume(far(j + 2), a1_ref, st)
        return consume(far(j + 3), b1_ref, st)

    nrest = jnp.maximum(nfar - 2, 0)
    n8 = nrest // 8
    st = lax.fori_loop(0, n8, lambda u, st: far_quad(8 * u + 6, far_quad(8 * u + 2, st)), st)
    n4 = (nrest - 8 * n8) // 4
    st = lax.fori_loop(0, n4, lambda u, st: far_quad(8 * n8 + 4 * u + 2, st), st)
    base = 8 * n8 + 4 * n4 + 2
    rem = nfar - base

    def rem_first(st):
        issue(far(base + 2), a1_ref)
        st = consume(far(base), a0_ref, st)
        return consume(far(base + 1), b0_ref, st)

    st = lax.cond(rem >= 1, rem_first, lambda st: st, st)
    lax.cond(rem >= 3, lambda st: consume(far(base + 2), a1_ref, st), lambda st: st, st)

    ot = jnp.concatenate([acc_ref[hh, 0:HD, :] / acc_ref[hh, HD:HD + 1, :] for hh in range(2)], axis=0)
    o_ref[0] = (ot.T * _silu(z_ref[0])).astype(o_ref.dtype)


def _moba(rel_bias, q, k, v, z):
    b, s, _ = q.shape
    nb = s // MOBA_BLOCK
    assert s % MOBA_BLOCK == 0 and nb < ATTN_HEAD_DIM
    pairs = ATTN_HEADS // 2
    blk = lambda hp, bi, i: (bi, i, hp)
    whole = lambda hp, bi, i: (bi, 0, hp)
    return pl.pallas_call(
        functools.partial(_moba_kernel, nb=nb),
        grid=(pairs, b, nb),
        in_specs=[
            pl.BlockSpec(memory_space=pltpu.SMEM),
            pl.BlockSpec((1, MOBA_BLOCK, 128), blk),
            pl.BlockSpec((1, MOBA_BLOCK, 128), lambda hp, bi, i: (bi, jnp.minimum(i + 1, nb - 1), hp)),
            pl.BlockSpec((1, s, 128), whole),
            pl.BlockSpec((1, s, 128), whole),
            pl.BlockSpec((1, MOBA_BLOCK, 128), blk),
        ],
        out_specs=pl.BlockSpec((1, MOBA_BLOCK, 128), blk),
        out_shape=jax.ShapeDtypeStruct((b, s, ATTN_WIDTH), BF16),
        scratch_shapes=[
            pltpu.VMEM((2, nb + 1, MOBA_BLOCK, 128), BF16),
            pltpu.VMEM((nb + 1, 2, VT_ROWS, MOBA_BLOCK), BF16),
            pltpu.VMEM((2, ATTN_HEAD_DIM, 128), F32),
            pltpu.VMEM((2, 2, MOBA_BLOCK, MOBA_BLOCK), F32),
            pltpu.VMEM((2, VT_ROWS, MOBA_BLOCK), F32),
            pltpu.VMEM((2, 2, 128, MOBA_BLOCK), BF16),
            pltpu.VMEM((4, 2, MOBA_BLOCK, MOBA_BLOCK), F32),
        ],
        compiler_params=pltpu.CompilerParams(
            dimension_semantics=("arbitrary", "arbitrary", "arbitrary"), vmem_limit_bytes=VMEM_LIMIT),
        name="moba",
    )(rel_bias, q, q, k, v, z)


DN_ROWS = DN_HEADS * DN_CHUNK


def _split2(a):
    a0 = a.astype(BF16)
    return a0, (a - a0.astype(F32)).astype(BF16)


def _split3(a):
    a0 = a.astype(BF16)
    r1 = a - a0.astype(F32)
    a1 = r1.astype(BF16)
    return a0, a1, (r1 - a1.astype(F32)).astype(BF16)


def _dot(a, b, dims=(((1,), (0,)), ((), ()))):
    return lax.dot_general(a.astype(BF16), b.astype(BF16), dims, preferred_element_type=F32)


_NT = (((1,), (1,)), ((), ()))
_TN = (((0,), (0,)), ((), ()))


def _dot_sel(sel, b):
    return sum(_nn(sel, p) for p in _split2(b))


def _deltanet_kernel(alog_ref, dtb_ref, x_ref, z_ref, g_ref, cw_ref, nw_ref, res_ref, ya_ref, wa_ref, wd_ref,
                     o_ref, state_ref, pad_ref, act_ref, rep_ref, yd_ref, *, tb):
    t = pl.program_id(1)
    C, R, W, H, D = DN_CHUNK, DN_ROWS, DN_WIDTH, DN_HEADS, DN_HEAD

    @pl.when(t == 0)
    def _reset():
        state_ref[...] = jnp.zeros_like(state_ref)
        pad_ref[0:8, :] = jnp.zeros((8, 3 * W), F32)

    pad_ref[8:8 + tb, :] = x_ref[0]
    for gcol in range(3 * H):
        cols = slice(gcol * D, (gcol + 1) * D)
        xp = pad_ref[:, cols]
        acc = xp * cw_ref[0:1, cols]
        for w in range(1, DN_CONV_WIDTH):
            acc = xp * cw_ref[w:w + 1, cols] + pltpu.roll(acc, 1, 0)
        a = _silu(acc[8:8 + tb])
        if gcol < 2 * H:
            a = a * lax.rsqrt(jnp.sum(a * a, axis=-1, keepdims=True) + EPS)
            if gcol < H:
                a = a * (D ** -0.5)
        act_ref[:, cols] = a
    pad_ref[0:8, :] = x_ref[0, tb - 8:tb, :]

    lane1 = lax.broadcasted_iota(jnp.int32, (1, GATE_PAD), 1)
    alog_row = jnp.zeros((1, GATE_PAD), F32)
    dtb_row = jnp.zeros((1, GATE_PAD), F32)
    for h in range(H):
        alog_row = jnp.where(lane1 == H + h, alog_ref[h], alog_row)
        dtb_row = jnp.where(lane1 == H + h, dtb_ref[h], dtb_row)
    gates = g_ref[0]
    xs = gates + dtb_row
    softplus = jnp.maximum(xs, 0.0) + jnp.log(1.0 + jnp.exp(-jnp.abs(xs)))
    gval = jnp.where(lane1 < H, 1.0 / (1.0 + jnp.exp(-gates)), -jnp.exp(alog_row) * softplus)
    tr = lax.broadcasted_iota(jnp.int32, (tb, tb), 0)
    tc = lax.broadcasted_iota(jnp.int32, (tb, tb), 1)
    gcum = _dot_sel(jnp.where((tr // C == tc // C) & (tr >= tc), 1.0, 0.0).astype(BF16), gval)
    er = lax.broadcasted_iota(jnp.int32, (GATE_PAD, 2 * H * D), 0)
    ec = lax.broadcasted_iota(jnp.int32, (GATE_PAD, 2 * H * D), 1)
    spread = jnp.where(er == ec // D, 1.0, 0.0).astype(BF16)
    rep_ref[...] = sum(_nn(p, spread) for p in _split2(jnp.where(lane1 < H, gval, gcum)))

    r = lax.broadcasted_iota(jnp.int32, (R, R), 0)
    c = lax.broadcasted_iota(jnp.int32, (R, R), 1)
    same = (r // C) == (c // C)
    incl = same & (r >= c)
    strict = same & (r > c)
    eye = jnp.where(r == c, 1.0, 0.0)
    lane_r = lax.broadcasted_iota(jnp.int32, (R, D), 1)

    chunks = range(tb // C)
    each = lambda f, *seqs: [f(*a) for a in zip(*seqs)]
    rows = [slice(ci * C, (ci + 1) * C) for ci in chunks]
    stack = lambda ref, c0: [jnp.concatenate([ref[rw, c0 + D * h:c0 + D * (h + 1)] for h in range(H)], axis=0)
                             for rw in rows]
    qc, kc, vc = stack(act_ref, 0), stack(act_ref, W), stack(act_ref, 2 * W)
    beta = stack(rep_ref, 0)
    gc = stack(rep_ref, H * D)

    def pair_diff(g):
        p0, p1, p2 = (p.astype(F32) for p in _split3(g))
        u_m = jnp.where(lane_r == 0, p0, jnp.where(lane_r == 1, p1, jnp.where(lane_r == 2, p2,
                        jnp.where(lane_r < 6, 1.0, 0.0))))
        w_m = jnp.where(lane_r < 3, 1.0, jnp.where(lane_r == 3, -p0, jnp.where(lane_r == 4, -p1,
                        jnp.where(lane_r == 5, -p2, 0.0))))
        return _nt(u_m.astype(BF16), w_m.astype(BF16))

    decay = each(lambda g: jnp.where(incl, jnp.exp(jnp.where(incl, pair_diff(g), 0.0)), 0.0), gc)
    kb = each(lambda k, b: k * b, kc, beta)
    x = each(lambda a, k, d: -jnp.where(strict, _dot(a, k, _NT) * d, 0.0), kb, kc, decay)
    sq = lambda m: each(lambda a: _dot(a, a), m)
    mul = lambda ma, mb: each(lambda a, b: _dot(a, b), ma, mb)
    add = lambda ma, mb: each(lambda a, b: a + b, ma, mb)
    plus_eye = lambda m: each(lambda a: eye + a, m)
    x2 = sq(x)
    x4 = sq(x2)
    f1 = plus_eye(x)
    m1 = add(f1, mul(f1, x2))
    x8 = sq(x4)
    f4 = plus_eye(x4)
    m2 = add(f4, mul(f4, x8))
    x16 = sq(x8)
    m12 = mul(m1, m2)
    x32 = sq(x16)
    f16 = plus_eye(x16)
    m3 = add(f16, mul(f16, x32))
    tinv = mul(m12, m3)
    eg = each(jnp.exp, gc)
    uw = each(lambda t, v, b, k, e: _dot(t, jnp.concatenate([v * b, k * e], axis=1)),
              tinv, vc, beta, kb, eg)
    a_intra = each(lambda q, k, d: jnp.where(incl, _dot(q, k, _NT) * d, 0.0), qc, kc, decay)
    glast = [[g[h * C + C - 1:h * C + C, :] for h in range(H)] for g in gc]
    q_dec = each(lambda q, e: q * e, qc, eg)
    k_dec = each(lambda k, g, gl: k * jnp.exp(
        jnp.concatenate([jnp.broadcast_to(t, (C, D)) for t in gl], axis=0) - g), kc, gc, glast)

    hs = [slice(h * C, (h + 1) * C) for h in range(H)]
    for ci in chunks:
        state = [state_ref[h] for h in range(H)]
        wq = [_dot(jnp.concatenate([uw[ci][hs[h], D:2 * D], q_dec[ci][hs[h]]], axis=0), state[h])
              for h in range(H)]
        vn = [uw[ci][hs[h], 0:D] - wq[h][0:C] for h in range(H)]
        for h in range(H):
            state_ref[h] = state[h] * jnp.exp(glast[ci][h]) + _dot(k_dec[ci][hs[h]], vn[h], _TN)
        o = jnp.concatenate([wq[h][C:2 * C] for h in range(H)], axis=0) + _dot(
            a_intra[ci], jnp.concatenate(vn, axis=0))
        on = o * lax.rsqrt(jnp.mean(o * o, axis=-1, keepdims=True) + EPS) * nw_ref[...]
        for h in range(H):
            cols = slice(h * D, (h + 1) * D)
            yd_ref[rows[ci], cols] = (on[hs[h]] * _silu(z_ref[0, rows[ci], cols])).astype(yd_ref.dtype)

    o_ref[0] = res_ref[0] + _nn(ya_ref[0], wa_ref[...]) + _nn(yd_ref[...], wd_ref[...])


def _deltanet(a_log, dt_bias, qkv, z, gates, conv_w, dn_norm_w, x, ya, wa, wd, tb):
    b, s, _ = qkv.shape
    assert s % tb == 0 and tb % DN_CHUNK == 0
    blk = lambda bi, t: (bi, t, 0)
    fixed = lambda bi, t: (0, 0)
    return pl.pallas_call(
        functools.partial(_deltanet_kernel, tb=tb),
        grid=(b, s // tb),
        in_specs=[
            pl.BlockSpec(memory_space=pltpu.SMEM),
            pl.BlockSpec(memory_space=pltpu.SMEM),
            pl.BlockSpec((1, tb, 3 * DN_WIDTH), blk),
            pl.BlockSpec((1, tb, DN_WIDTH), blk),
            pl.BlockSpec((1, tb, GATE_PAD), blk),
            pl.BlockSpec((DN_CONV_WIDTH, 3 * DN_WIDTH), fixed),
            pl.BlockSpec((1, DN_HEAD), fixed),
            pl.BlockSpec((1, tb, D_MODEL), blk),
            pl.BlockSpec((1, tb, ATTN_WIDTH), blk),
            pl.BlockSpec((ATTN_WIDTH, D_MODEL), fixed),
            pl.BlockSpec((DN_WIDTH, D_MODEL), fixed),
        ],
        out_specs=pl.BlockSpec((1, tb, D_MODEL), blk),
        out_shape=jax.ShapeDtypeStruct((b, s, D_MODEL), F32),
        scratch_shapes=[
            pltpu.VMEM((DN_HEADS, DN_HEAD, DN_HEAD), F32),
            pltpu.VMEM((tb + 8, 3 * DN_WIDTH), F32),
            pltpu.VMEM((tb, 3 * DN_WIDTH), F32),
            pltpu.VMEM((tb, 2 * DN_HEADS * DN_HEAD), F32),
            pltpu.VMEM((tb, DN_WIDTH), BF16),
        ],
        compiler_params=pltpu.CompilerParams(
            dimension_semantics=("arbitrary", "arbitrary"), vmem_limit_bytes=VMEM_LIMIT),
        name="deltanet",
    )(a_log, dt_bias, qkv, z, gates, conv_w, dn_norm_w, x, ya, wa, wd)


def _row_tile(n):
    for tm in (512, 256, 128, 64, 32, 16, 8):
        if n % tm == 0:
            return tm
    raise ValueError(f"row count {n} is not a multiple of 8")


def kernel(x, rel_bias, norm_w, w_in, q_norm_w, k_norm_w, conv_w, a_log, dt_bias, dn_norm_w, w_out):
    b, s, d = x.shape
    assert d == D_MODEL and norm_w.shape[0] == 1, "single-layer kernel"
    n = b * s
    x2 = x.reshape(n, d)
    tm = _row_tile(n)

    w_main = w_in[0][:, :MAIN_COLS].astype(BF16)
    w_gate = jnp.pad(w_in[0][:, MAIN_COLS:], ((0, 0), (0, GATE_PAD - 2 * DN_HEADS))).astype(BF16)
    qnw = jnp.tile(q_norm_w[0], ATTN_HEADS)[None, :]
    knw = jnp.tile(k_norm_w[0], ATTN_HEADS)[None, :]

    q, k, v, za, qkv, zd, gates = _inproj(x2, norm_w, w_main, w_gate, qnw, knw, tm)

    r3 = lambda t: t.reshape(b, s, t.shape[-1])
    ya = _moba(rel_bias, r3(q), r3(k), r3(v), r3(za))
    tb = 256 if s % 256 == 0 else DN_CHUNK
    w_o = w_out[0].astype(BF16)
    return _deltanet(a_log[0], dt_bias[0], r3(qkv), r3(zd), r3(gates), conv_w[0], dn_norm_w,
                     x, ya, w_o[:ATTN_WIDTH], w_o[ATTN_WIDTH:], tb)
```

```python
import functools
import math

import jax
import jax.numpy as jnp
from jax import lax
from jax.experimental import pallas as pl
from jax.experimental.pallas import tpu as pltpu

F32 = jnp.float32
BF16 = jnp.bfloat16

D_MODEL = 1024
ATTN_HEADS = 8
ATTN_HEAD_DIM = 64
ATTN_WIDTH = ATTN_HEADS * ATTN_HEAD_DIM
MOBA_BLOCK = 256
MOBA_TOPK = 3
REL_BUCKETS = 32
REL_MAX_DISTANCE = 128
DN_HEADS = 4
DN_HEAD = 128
DN_WIDTH = DN_HEADS * DN_HEAD
DN_CONV_WIDTH = 4
DN_CHUNK = 64
MAIN_COLS = 4 * ATTN_WIDTH + 3 * DN_WIDTH + DN_WIDTH
GATE_PAD = 128
EPS = 1e-6
LOG2E = math.log2(math.e)
NEG_BIG = -32768.0
VT_ROWS = ATTN_HEAD_DIM + 16
VMEM_LIMIT = 56 * 1024 * 1024


def _nt(a, b):
    return lax.dot_general(a, b, (((1,), (1,)), ((), ())), preferred_element_type=F32)


def _nn(a, b):
    return lax.dot_general(a, b, (((1,), (0,)), ((), ())), preferred_element_type=F32)


def _silu(x):
    half = 0.5 * x
    return half + half * jnp.tanh(half)


def _inproj_kernel(x_ref, nw_ref, w_ref, wg_ref, qnw_ref, knw_ref,
                   q_ref, k_ref, v_ref, za_ref, qkv_ref, zd_ref, g_ref):
    x = x_ref[...]
    ms = jnp.mean(x * x, axis=-1, keepdims=True)
    h = (x * lax.rsqrt(ms + EPS) * nw_ref[...]).astype(BF16)

    r_i = lax.broadcasted_iota(jnp.int32, (ATTN_WIDTH, ATTN_WIDTH), 0)
    c_i = lax.broadcasted_iota(jnp.int32, (ATTN_WIDTH, ATTN_WIDTH), 1)
    same_head = jnp.where(r_i // ATTN_HEAD_DIM == c_i // ATTN_HEAD_DIM, 1.0, 0.0).astype(BF16)

    def head_rms(t, w):
        ss = _nn((t * t).astype(BF16), same_head)
        return t * lax.rsqrt(ss * (1.0 / ATTN_HEAD_DIM) + EPS) * w

    def proj(c0, width):
        return _nn(h, w_ref[:, c0:c0 + width])

    q_ref[...] = head_rms(proj(0, ATTN_WIDTH), qnw_ref[...]).astype(q_ref.dtype)
    k_ref[...] = head_rms(proj(ATTN_WIDTH, ATTN_WIDTH), knw_ref[...]).astype(k_ref.dtype)
    v_ref[...] = proj(2 * ATTN_WIDTH, ATTN_WIDTH).astype(v_ref.dtype)
    za_ref[...] = proj(3 * ATTN_WIDTH, ATTN_WIDTH).astype(za_ref.dtype)
    for c in range(3):
        qkv_ref[:, c * DN_WIDTH:(c + 1) * DN_WIDTH] = proj(4 * ATTN_WIDTH + c * DN_WIDTH, DN_WIDTH).astype(qkv_ref.dtype)
    zd_ref[...] = proj(4 * ATTN_WIDTH + 3 * DN_WIDTH, DN_WIDTH).astype(zd_ref.dtype)
    g_ref[...] = _nn(h, wg_ref[...])


def _inproj(x2, norm_w, w_main, w_gate, qnw, knw, tm):
    n = x2.shape[0]
    row = lambda i: (i, 0)
    fixed = lambda i: (0, 0)
    outs = [
        jax.ShapeDtypeStruct((n, ATTN_WIDTH), F32),
        jax.ShapeDtypeStruct((n, ATTN_WIDTH), F32),
        jax.ShapeDtypeStruct((n, ATTN_WIDTH), BF16),
        jax.ShapeDtypeStruct((n, ATTN_WIDTH), F32),
        jax.ShapeDtypeStruct((n, 3 * DN_WIDTH), F32),
        jax.ShapeDtypeStruct((n, DN_WIDTH), F32),
        jax.ShapeDtypeStruct((n, GATE_PAD), F32),
    ]
    return pl.pallas_call(
        _inproj_kernel,
        grid=(n // tm,),
        in_specs=[
            pl.BlockSpec((tm, D_MODEL), row),
            pl.BlockSpec((1, D_MODEL), fixed),
            pl.BlockSpec((D_MODEL, MAIN_COLS), fixed),
            pl.BlockSpec((D_MODEL, GATE_PAD), fixed),
            pl.BlockSpec((1, ATTN_WIDTH), fixed),
            pl.BlockSpec((1, ATTN_WIDTH), fixed),
        ],
        out_specs=[pl.BlockSpec((tm, o.shape[1]), row) for o in outs],
        out_shape=outs,
        compiler_params=pltpu.CompilerParams(dimension_semantics=("arbitrary",), vmem_limit_bytes=VMEM_LIMIT),
        name="inproj",
    )(x2, norm_w, w_main, w_gate, qnw, knw)


def _t5_bias_tile_t(relb_ref, head, offset):
    c = lax.broadcasted_iota(jnp.int32, (MOBA_BLOCK, MOBA_BLOCK), 0)
    r = lax.broadcasted_iota(jnp.int32, (MOBA_BLOCK, MOBA_BLOCK), 1)
    dist = r - c + offset
    n = jnp.maximum(dist, 0)
    max_exact = REL_BUCKETS // 2
    nf = jnp.maximum(n, 1).astype(F32)
    large = max_exact + (jnp.log(nf / max_exact) / math.log(REL_MAX_DISTANCE / max_exact)
                         * (REL_BUCKETS - max_exact)).astype(jnp.int32)
    large = jnp.minimum(large, REL_BUCKETS - 1)
    bucket = jnp.where(n < max_exact, n, large)
    far = relb_ref[REL_BUCKETS - 1, head]
    bias = jnp.zeros((MOBA_BLOCK, MOBA_BLOCK), F32)
    for t in range(REL_BUCKETS):
        bias = jnp.where(bucket == t, relb_ref[t, head] - far, bias)
    return jnp.where(dist >= 0, bias * LOG2E, NEG_BIG)


def _moba_kernel(relb_ref, q_ref, qn_ref, k_ref, v_ref, z_ref, o_ref,
                 kaug_ref, vt_ref, kmean_ref, bias_ref, acc_ref, qa_ref, s_ref, *, nb):
    hp = pl.program_id(0)
    bi = pl.program_id(1)
    i = pl.program_id(2)
    par = i % 2
    HD = ATTN_HEAD_DIM
    lane = lax.broadcasted_iota(jnp.int32, (MOBA_BLOCK, 128), 1)
    lo_half = lane < HD
    er = lax.broadcasted_iota(jnp.int32, (128, 128), 0)
    ec = lax.broadcasted_iota(jnp.int32, (128, 128), 1)
    eye = jnp.where(er == ec, 1.0, 0.0).astype(BF16)
    blk = lax.broadcasted_iota(jnp.int32, (HD, MOBA_BLOCK), 0)

    def gate_scores(q):
        qt = _nt(eye, (q * (HD ** -0.5 * LOG2E)).astype(BF16))
        q0, q1 = _split2(q)
        gates = []
        for hh in range(2):
            k0, k1 = _split2(kmean_ref[hh])
            gates.append(_nt(k0, q0) + (_nt(k0, q1) + _nt(k1, q0)))
        return qt, gates

    def store_operands(slot, qt, gates, tile):
        for hh in range(2):
            g = jnp.where(blk < tile, gates[hh], -jnp.inf)
            sel = blk == tile
            for _ in range(MOBA_TOPK):
                mx = jnp.max(g, axis=0, keepdims=True)
                first = jnp.min(jnp.where(g == mx, blk, HD), axis=0, keepdims=True)
                hit = (blk == first) & (mx > -jnp.inf)
                sel = sel | hit
                g = jnp.where(hit, -jnp.inf, g)
            mask = jnp.where(sel, 0.0, NEG_BIG)
            rows = [qt[0:HD], mask] if hh == 0 else [mask, qt[HD:2 * HD]]
            qa_ref[slot, hh] = jnp.concatenate(rows, axis=0).astype(BF16)

    @pl.when(i == 0)
    def _prepare():
        kmean_ref[...] = jnp.zeros_like(kmean_ref)
        lane1 = lax.broadcasted_iota(jnp.int32, (1, 128), 1)
        ones_rows = jnp.ones((VT_ROWS - HD, MOBA_BLOCK), BF16)

        def prep(j, carry):
            rows = pl.ds(pl.multiple_of(j * MOBA_BLOCK, MOBA_BLOCK), MOBA_BLOCK)
            kb = k_ref[0, rows, :]
            km = jnp.mean(kb, axis=0, keepdims=True)
            kmean_ref[0, pl.ds(j, 1), :] = jnp.where(lane1 < HD, km, 0.0)
            kmean_ref[1, pl.ds(j, 1), :] = jnp.where(lane1 < HD, 0.0, km)
            kaug_ref[0, j] = jnp.where(lo_half, kb, jnp.where(lane == HD + j, 1.0, 0.0)).astype(BF16)
            kaug_ref[1, j] = jnp.where(lo_half, jnp.where(lane == j, 1.0, 0.0), kb).astype(BF16)
            vt = _nt(eye, v_ref[0, rows, :]).astype(BF16)
            for hh in range(2):
                vt_ref[j, hh] = jnp.concatenate([vt[hh * HD:(hh + 1) * HD], ones_rows], axis=0)
            return carry

        lax.fori_loop(0, nb, prep, 0, unroll=4)
        kaug_ref[0, nb] = jnp.where(lane == 2 * HD - 1, 1.0, 0.0).astype(BF16)
        kaug_ref[1, nb] = jnp.where(lane == HD - 1, 1.0, 0.0).astype(BF16)
        vt_ref[nb] = jnp.zeros((2, VT_ROWS, MOBA_BLOCK), BF16)
        store_operands(0, *gate_scores(q_ref[0]), 0)

    @pl.when((i == 0) & (bi == 0))
    def _bias_tiles():
        for hh in range(2):
            bias_ref[hh, 0] = _t5_bias_tile_t(relb_ref, 2 * hp + hh, 0)
            bias_ref[hh, 1] = _t5_bias_tile_t(relb_ref, 2 * hp + hh, MOBA_BLOCK)

    def issue(j, buf_ref):
        for hh in range(2):
            buf_ref[hh] = _nn(kaug_ref[hh, j], qa_ref[par, hh])

    def consume(j, buf_ref, st, bias_idx=None):
        out = []
        for hh in range(2):
            s = buf_ref[hh]
            if bias_idx is not None:
                s = s + bias_ref[hh, bias_idx]
            m_new = jnp.maximum(st[hh], jnp.max(s, axis=0, keepdims=True))
            p = jnp.exp2((s - m_new).astype(BF16))
            pv = _nn(vt_ref[j, hh], p)
            acc_ref[hh] = jnp.exp2(st[hh] - m_new) * acc_ref[hh] + pv
            out.append(m_new)
        return tuple(out)

    nfar = jnp.maximum(i - 1, 0)
    far = lambda j: jnp.where(j < nfar, j, nb)
    prev = jnp.where(i >= 1, i - 1, nb)
    a0_ref, b0_ref, a1_ref, b1_ref = (s_ref.at[n] for n in range(4))
    qt_next, gates_next = gate_scores(qn_ref[0])
    issue(i, a0_ref)
    issue(prev, b0_ref)
    issue(far(0), a1_ref)
    issue(far(1), b1_ref)
    store_operands(1 - par, qt_next, gates_next, i + 1)
    acc_ref[...] = jnp.zeros_like(acc_ref)
    st = (jnp.full((1, MOBA_BLOCK), 2 * NEG_BIG, F32),) * 2
    st = consume(i, a0_ref, st, bias_idx=0)
    issue(far(2), a0_ref)
    st = consume(prev, b0_ref, st, bias_idx=1)
    issue(far(3), b0_ref)
    st = consume(far(0), a1_ref, st)
    st = consume(far(1), b1_ref, st)

    def far_quad(j, st):
        issue(far(j + 2), a1_ref)
        issue(far(j + 3), b1_ref)
        st = consume(far(j), a0_ref, st)
        issue(far(j + 4), a0_ref)
        st = consume(far(j + 1), b0_ref, st)
        issue(far(j + 5), b0_ref)
        st = consume(far(j + 2), a1_ref, st)
        return consume(far(j + 3), b1_ref, st)

    nrest = jnp.maximum(nfar - 2, 0)
    n8 = nrest // 8
    st = lax.fori_loop(0, n8, lambda u, st: far_quad(8 * u + 6, far_quad(8 * u + 2, st)), st)
    n4 = (nrest - 8 * n8) // 4
    st = lax.fori_loop(0, n4, lambda u, st: far_quad(8 * n8 + 4 * u + 2, st), st)
    base = 8 * n8 + 4 * n4 + 2
    rem = nfar - base

    def rem_first(st):
        issue(far(base + 2), a1_ref)
        st = consume(far(base), a0_ref, st)
        return consume(far(base + 1), b0_ref, st)

    st = lax.cond(rem >= 1, rem_first, lambda st: st, st)
    lax.cond(rem >= 3, lambda st: consume(far(base + 2), a1_ref, st), lambda st: st, st)

    ot = jnp.concatenate([acc_ref[hh, 0:HD, :] / acc_ref[hh, HD:HD + 1, :] for hh in range(2)], axis=0)
    o_ref[0] = (ot.T * _silu(z_ref[0])).astype(o_ref.dtype)


def _moba(rel_bias, q, k, v, z):
    b, s, _ = q.shape
    nb = s // MOBA_BLOCK
    assert s % MOBA_BLOCK == 0 and nb < ATTN_HEAD_DIM
    pairs = ATTN_HEADS // 2
    blk = lambda hp, bi, i: (bi, i, hp)
    whole = lambda hp, bi, i: (bi, 0, hp)
    return pl.pallas_call(
        functools.partial(_moba_kernel, nb=nb),
        grid=(pairs, b, nb),
        in_specs=[
            pl.BlockSpec(memory_space=pltpu.SMEM),
            pl.BlockSpec((1, MOBA_BLOCK, 128), blk),
            pl.BlockSpec((1, MOBA_BLOCK, 128), lambda hp, bi, i: (bi, jnp.minimum(i + 1, nb - 1), hp)),
            pl.BlockSpec((1, s, 128), whole),
            pl.BlockSpec((1, s, 128), whole),
            pl.BlockSpec((1, MOBA_BLOCK, 128), blk),
        ],
        out_specs=pl.BlockSpec((1, MOBA_BLOCK, 128), blk),
        out_shape=jax.ShapeDtypeStruct((b, s, ATTN_WIDTH), BF16),
        scratch_shapes=[
            pltpu.VMEM((2, nb + 1, MOBA_BLOCK, 128), BF16),
            pltpu.VMEM((nb + 1, 2, VT_ROWS, MOBA_BLOCK), BF16),
            pltpu.VMEM((2, ATTN_HEAD_DIM, 128), F32),
            pltpu.VMEM((2, 2, MOBA_BLOCK, MOBA_BLOCK), F32),
            pltpu.VMEM((2, VT_ROWS, MOBA_BLOCK), F32),
            pltpu.VMEM((2, 2, 128, MOBA_BLOCK), BF16),
            pltpu.VMEM((4, 2, MOBA_BLOCK, MOBA_BLOCK), F32),
        ],
        compiler_params=pltpu.CompilerParams(
            dimension_semantics=("arbitrary", "arbitrary", "arbitrary"), vmem_limit_bytes=VMEM_LIMIT),
        name="moba",
    )(rel_bias, q, q, k, v, z)


DN_ROWS = DN_HEADS * DN_CHUNK


def _split2(a):
    a0 = a.astype(BF16)
    return a0, (a - a0.astype(F32)).astype(BF16)


def _split3(a):
    a0 = a.astype(BF16)
    r1 = a - a0.astype(F32)
    a1 = r1.astype(BF16)
    return a0, a1, (r1 - a1.astype(F32)).astype(BF16)


def _dot(a, b, dims=(((1,), (0,)), ((), ()))):
    return lax.dot_general(a.astype(BF16), b.astype(BF16), dims, preferred_element_type=F32)


_NT = (((1,), (1,)), ((), ()))
_TN = (((0,), (0,)), ((), ()))


def _dot_sel(sel, b):
    return sum(_nn(sel, p) for p in _split2(b))


def _deltanet_kernel(alog_ref, dtb_ref, x_ref, z_ref, g_ref, cw_ref, nw_ref, res_ref, ya_ref, wa_ref, wd_ref,
                     o_ref, state_ref, pad_ref, act_ref, rep_ref, yd_ref, *, tb):
    t = pl.program_id(1)
    C, R, W, H, D = DN_CHUNK, DN_ROWS, DN_WIDTH, DN_HEADS, DN_HEAD

    @pl.when(t == 0)
    def _reset():
        state_ref[...] = jnp.zeros_like(state_ref)
        pad_ref[0:8, :] = jnp.zeros((8, 3 * W), F32)

    pad_ref[8:8 + tb, :] = x_ref[0]
    for gcol in range(3 * H):
        cols = slice(gcol * D, (gcol + 1) * D)
        xp = pad_ref[:, cols]
        acc = xp * cw_ref[0:1, cols]
        for w in range(1, DN_CONV_WIDTH):
            acc = xp * cw_ref[w:w + 1, cols] + pltpu.roll(acc, 1, 0)
        a = _silu(acc[8:8 + tb])
        if gcol < 2 * H:
            a = a * lax.rsqrt(jnp.sum(a * a, axis=-1, keepdims=True) + EPS)
            if gcol < H:
                a = a * (D ** -0.5)
        act_ref[:, cols] = a
    pad_ref[0:8, :] = x_ref[0, tb - 8:tb, :]

    lane1 = lax.broadcasted_iota(jnp.int32, (1, GATE_PAD), 1)
    alog_row = jnp.zeros((1, GATE_PAD), F32)
    dtb_row = jnp.zeros((1, GATE_PAD), F32)
    for h in range(H):
        alog_row = jnp.where(lane1 == H + h, alog_ref[h], alog_row)
        dtb_row = jnp.where(lane1 == H + h, dtb_ref[h], dtb_row)
    gates = g_ref[0]
    xs = gates + dtb_row
    softplus = jnp.maximum(xs, 0.0) + jnp.log(1.0 + jnp.exp(-jnp.abs(xs)))
    gval = jnp.where(lane1 < H, 1.0 / (1.0 + jnp.exp(-gates)), -jnp.exp(alog_row) * softplus)
    tr = lax.broadcasted_iota(jnp.int32, (tb, tb), 0)
    tc = lax.broadcasted_iota(jnp.int32, (tb, tb), 1)
    gcum = _dot_sel(jnp.where((tr // C == tc // C) & (tr >= tc), 1.0, 0.0).astype(BF16), gval)
    er = lax.broadcasted_iota(jnp.int32, (GATE_PAD, 2 * H * D), 0)
    ec = lax.broadcasted_iota(jnp.int32, (GATE_PAD, 2 * H * D), 1)
    spread = jnp.where(er == ec // D, 1.0, 0.0).astype(BF16)
    rep_ref[...] = sum(_nn(p, spread) for p in _split2(jnp.where(lane1 < H, gval, gcum)))

    r = lax.broadcasted_iota(jnp.int32, (R, R), 0)
    c = lax.broadcasted_iota(jnp.int32, (R, R), 1)
    same = (r // C) == (c // C)
    incl = same & (r >= c)
    strict = same & (r > c)
    eye = jnp.where(r == c, 1.0, 0.0)
    lane_r = lax.broadcasted_iota(jnp.int32, (R, D), 1)

    chunks = range(tb // C)
    each = lambda f, *seqs: [f(*a) for a in zip(*seqs)]
    rows = [slice(ci * C, (ci + 1) * C) for ci in chunks]
    stack = lambda ref, c0: [jnp.concatenate([ref[rw, c0 + D * h:c0 + D * (h + 1)] for h in range(H)], axis=0)
                             for rw in rows]
    qc, kc, vc = stack(act_ref, 0), stack(act_ref, W), stack(act_ref, 2 * W)
    beta = stack(rep_ref, 0)
    gc = stack(rep_ref, H * D)

    pick = jnp.where((lax.broadcasted_iota(jnp.int32, (16, D), 0) == 0)
                     & (lax.broadcasted_iota(jnp.int32, (16, D), 1) < 3), 1.0, 0.0).astype(BF16)

    def pair_diff(g):
        p0, p1, p2 = (p.astype(F32) for p in _split3(g))
        parts = jnp.where(lane_r == 0, p0, jnp.where(lane_r == 1, p1, jnp.where(lane_r == 2, p2, 0.0)))
        g_row = _nt(pick, parts.astype(BF16))[0:1, :]
        return jnp.concatenate([g, g], axis=1) - g_row

    decay = each(lambda g: jnp.where(incl, jnp.exp(jnp.where(incl, pair_diff(g), 0.0)), 0.0), gc)
    kb = each(lambda k, b: k * b, kc, beta)
    x = each(lambda a, k, d: -jnp.where(strict, _dot(a, k, _NT) * d, 0.0), kb, kc, decay)
    sq = lambda m: each(lambda a: _dot(a, a), m)
    mul = lambda ma, mb: each(lambda a, b: _dot(a, b), ma, mb)
    add = lambda ma, mb: each(lambda a, b: a + b, ma, mb)
    plus_eye = lambda m: each(lambda a: eye + a, m)
    x2 = sq(x)
    x4 = sq(x2)
    f1 = plus_eye(x)
    m1 = add(f1, mul(f1, x2))
    x8 = sq(x4)
    f4 = plus_eye(x4)
    m2 = add(f4, mul(f4, x8))
    x16 = sq(x8)
    m12 = mul(m1, m2)
    x32 = sq(x16)
    f16 = plus_eye(x16)
    m3 = add(f16, mul(f16, x32))
    tinv = mul(m12, m3)
    eg = each(jnp.exp, gc)
    uw = each(lambda t, v, b, k, e: _dot(t, jnp.concatenate([v * b, k * e], axis=1)),
              tinv, vc, beta, kb, eg)
    a_intra = each(lambda q, k, d: jnp.where(incl, _dot(q, k, _NT) * d, 0.0), qc, kc, decay)
    glast = [[g[h * C + C - 1:h * C + C, :] for h in range(H)] for g in gc]
    q_dec = each(lambda q, e: q * e, qc, eg)
    k_dec = each(lambda k, g, gl: k * jnp.exp(
        jnp.concatenate([jnp.broadcast_to(t, (C, D)) for t in gl], axis=0) - g), kc, gc, glast)

    hs = [slice(h * C, (h + 1) * C) for h in range(H)]
    for ci in chunks:
        state = [state_ref[h] for h in range(H)]
        wq = [_dot(jnp.concatenate([uw[ci][hs[h], D:2 * D], q_dec[ci][hs[h]]], axis=0), state[h])
              for h in range(H)]
        vn = [uw[ci][hs[h], 0:D] - wq[h][0:C] for h in range(H)]
        for h in range(H):
            state_ref[h] = state[h] * jnp.exp(glast[ci][h]) + _dot(k_dec[ci][hs[h]], vn[h], _TN)
        o = jnp.concatenate([wq[h][C:2 * C] for h in range(H)], axis=0) + _dot(
            a_intra[ci], jnp.concatenate(vn, axis=0))
        on = o * lax.rsqrt(jnp.mean(o * o, axis=-1, keepdims=True) + EPS) * nw_ref[...]
        for h in range(H):
            cols = slice(h * D, (h + 1) * D)
            yd_ref[rows[ci], cols] = (on[hs[h]] * _silu(z_ref[0, rows[ci], cols])).astype(yd_ref.dtype)

    o_ref[0] = res_ref[0] + _nn(ya_ref[0], wa_ref[...]) + _nn(yd_ref[...], wd_ref[...])


def _deltanet(a_log, dt_bias, qkv, z, gates, conv_w, dn_norm_w, x, ya, wa, wd, tb):
    b, s, _ = qkv.shape
    assert s % tb == 0 and tb % DN_CHUNK == 0
    blk = lambda bi, t: (bi, t, 0)
    fixed = lambda bi, t: (0, 0)
    return pl.pallas_call(
        functools.partial(_deltanet_kernel, tb=tb),
        grid=(b, s // tb),
        in_specs=[
            pl.BlockSpec(memory_space=pltpu.SMEM),
            pl.BlockSpec(memory_space=pltpu.SMEM),
            pl.BlockSpec((1, tb, 3 * DN_WIDTH), blk),
            pl.BlockSpec((1, tb, DN_WIDTH), blk),
            pl.BlockSpec((1, tb, GATE_PAD), blk),
            pl.BlockSpec((DN_CONV_WIDTH, 3 * DN_WIDTH), fixed),
            pl.BlockSpec((1, DN_HEAD), fixed),
            pl.BlockSpec((1, tb, D_MODEL), blk),
            pl.BlockSpec((1, tb, ATTN_WIDTH), blk),
            pl.BlockSpec((ATTN_WIDTH, D_MODEL), fixed),
            pl.BlockSpec((DN_WIDTH, D_MODEL), fixed),
        ],
        out_specs=pl.BlockSpec((1, tb, D_MODEL), blk),
        out_shape=jax.ShapeDtypeStruct((b, s, D_MODEL), F32),
        scratch_shapes=[
            pltpu.VMEM((DN_HEADS, DN_HEAD, DN_HEAD), F32),
            pltpu.VMEM((tb + 8, 3 * DN_WIDTH), F32),
            pltpu.VMEM((tb, 3 * DN_WIDTH), F32),
            pltpu.VMEM((tb, 2 * DN_HEADS * DN_HEAD), F32),
            pltpu.VMEM((tb, DN_WIDTH), BF16),
        ],
        compiler_params=pltpu.CompilerParams(
            dimension_semantics=("arbitrary", "arbitrary"), vmem_limit_bytes=VMEM_LIMIT),
        name="deltanet",
    )(a_log, dt_bias, qkv, z, gates, conv_w, dn_norm_w, x, ya, wa, wd)


def _row_tile(n):
    for tm in (512, 256, 128, 64, 32, 16, 8):
        if n % tm == 0:
            return tm
    raise ValueError(f"row count {n} is not a multiple of 8")


def kernel(x, rel_bias, norm_w, w_in, q_norm_w, k_norm_w, conv_w, a_log, dt_bias, dn_norm_w, w_out):
    b, s, d = x.shape
    assert d == D_MODEL and norm_w.shape[0] == 1, "single-layer kernel"
    n = b * s
    x2 = x.reshape(n, d)
    tm = _row_tile(n)

    w_main = w_in[0][:, :MAIN_COLS].astype(BF16)
    w_gate = jnp.pad(w_in[0][:, MAIN_COLS:], ((0, 0), (0, GATE_PAD - 2 * DN_HEADS))).astype(BF16)
    qnw = jnp.tile(q_norm_w[0], ATTN_HEADS)[None, :]
    knw = jnp.tile(k_norm_w[0], ATTN_HEADS)[None, :]

    q, k, v, za, qkv, zd, gates = _inproj(x2, norm_w, w_main, w_gate, qnw, knw, tm)

    r3 = lambda t: t.reshape(b, s, t.shape[-1])
    ya = _moba(rel_bias, r3(q), r3(k), r3(v), r3(za))
    tb = 256 if s % 256 == 0 else DN_CHUNK
    w_o = w_out[0].astype(BF16)
    return _deltanet(a_log[0], dt_bias[0], r3(qkv), r3(zd), r3(gates), conv_w[0], dn_norm_w,
                     x, ya, w_o[:ATTN_WIDTH], w_o[ATTN_WIDTH:], tb)
```

```python
import functools
import math

import jax
import jax.numpy as jnp
from jax import lax
from jax.experimental import pallas as pl
from jax.experimental.pallas import tpu as pltpu

F32 = jnp.float32
BF16 = jnp.bfloat16

D_MODEL = 1024
ATTN_HEADS = 8
ATTN_HEAD_DIM = 64
ATTN_WIDTH = ATTN_HEADS * ATTN_HEAD_DIM
MOBA_BLOCK = 256
MOBA_TOPK = 3
REL_BUCKETS = 32
REL_MAX_DISTANCE = 128
DN_HEADS = 4
DN_HEAD = 128
DN_WIDTH = DN_HEADS * DN_HEAD
DN_CONV_WIDTH = 4
DN_CHUNK = 64
MAIN_COLS = 4 * ATTN_WIDTH + 3 * DN_WIDTH + DN_WIDTH
GATE_PAD = 128
EPS = 1e-6
LOG2E = math.log2(math.e)
NEG_BIG = -32768.0
VT_ROWS = ATTN_HEAD_DIM + 16
VMEM_LIMIT = 56 * 1024 * 1024


def _nt(a, b):
    return lax.dot_general(a, b, (((1,), (1,)), ((), ())), preferred_element_type=F32)


def _nn(a, b):
    return lax.dot_general(a, b, (((1,), (0,)), ((), ())), preferred_element_type=F32)


def _silu(x):
    half = 0.5 * x
    return half + half * jnp.tanh(half)


def _inproj_kernel(x_ref, nw_ref, w_ref, wg_ref, qnw_ref, knw_ref,
                   q_ref, k_ref, v_ref, za_ref, qkv_ref, zd_ref, g_ref):
    x = x_ref[...]
    ms = jnp.mean(x * x, axis=-1, keepdims=True)
    h = (x * lax.rsqrt(ms + EPS) * nw_ref[...]).astype(BF16)

    r_i = lax.broadcasted_iota(jnp.int32, (256, 256), 0)
    c_i = lax.broadcasted_iota(jnp.int32, (256, 256), 1)
    same_head = jnp.where(r_i // ATTN_HEAD_DIM == c_i // ATTN_HEAD_DIM, 1.0, 0.0).astype(BF16)

    def head_rms(t, w):
        t2 = (t * t).astype(BF16)
        ss = jnp.concatenate([_nn(t2[:, c0:c0 + 256], same_head) for c0 in range(0, ATTN_WIDTH, 256)], axis=1)
        return t * lax.rsqrt(ss * (1.0 / ATTN_HEAD_DIM) + EPS) * w

    def proj(c0, width):
        return _nn(h, w_ref[:, c0:c0 + width])

    q_ref[...] = head_rms(proj(0, ATTN_WIDTH), qnw_ref[...]).astype(q_ref.dtype)
    k_ref[...] = head_rms(proj(ATTN_WIDTH, ATTN_WIDTH), knw_ref[...]).astype(k_ref.dtype)
    v_ref[...] = proj(2 * ATTN_WIDTH, ATTN_WIDTH).astype(v_ref.dtype)
    za_ref[...] = proj(3 * ATTN_WIDTH, ATTN_WIDTH).astype(za_ref.dtype)
    for c in range(3):
        qkv_ref[:, c * DN_WIDTH:(c + 1) * DN_WIDTH] = proj(4 * ATTN_WIDTH + c * DN_WIDTH, DN_WIDTH).astype(qkv_ref.dtype)
    zd_ref[...] = proj(4 * ATTN_WIDTH + 3 * DN_WIDTH, DN_WIDTH).astype(zd_ref.dtype)
    g_ref[...] = _nn(h, wg_ref[...])


def _inproj(x2, norm_w, w_main, w_gate, qnw, knw, tm):
    n = x2.shape[0]
    row = lambda i: (i, 0)
    fixed = lambda i: (0, 0)
    outs = [
        jax.ShapeDtypeStruct((n, ATTN_WIDTH), F32),
        jax.ShapeDtypeStruct((n, ATTN_WIDTH), F32),
        jax.ShapeDtypeStruct((n, ATTN_WIDTH), BF16),
        jax.ShapeDtypeStruct((n, ATTN_WIDTH), F32),
        jax.ShapeDtypeStruct((n, 3 * DN_WIDTH), F32),
        jax.ShapeDtypeStruct((n, DN_WIDTH), F32),
        jax.ShapeDtypeStruct((n, GATE_PAD), F32),
    ]
    return pl.pallas_call(
        _inproj_kernel,
        grid=(n // tm,),
        in_specs=[
            pl.BlockSpec((tm, D_MODEL), row),
            pl.BlockSpec((1, D_MODEL), fixed),
            pl.BlockSpec((D_MODEL, MAIN_COLS), fixed),
            pl.BlockSpec((D_MODEL, GATE_PAD), fixed),
            pl.BlockSpec((1, ATTN_WIDTH), fixed),
            pl.BlockSpec((1, ATTN_WIDTH), fixed),
        ],
        out_specs=[pl.BlockSpec((tm, o.shape[1]), row) for o in outs],
        out_shape=outs,
        compiler_params=pltpu.CompilerParams(dimension_semantics=("arbitrary",), vmem_limit_bytes=VMEM_LIMIT),
        name="inproj",
    )(x2, norm_w, w_main, w_gate, qnw, knw)


def _t5_bias_tile_t(relb_ref, head, offset):
    c = lax.broadcasted_iota(jnp.int32, (MOBA_BLOCK, MOBA_BLOCK), 0)
    r = lax.broadcasted_iota(jnp.int32, (MOBA_BLOCK, MOBA_BLOCK), 1)
    dist = r - c + offset
    n = jnp.maximum(dist, 0)
    max_exact = REL_BUCKETS // 2
    nf = jnp.maximum(n, 1).astype(F32)
    large = max_exact + (jnp.log(nf / max_exact) / math.log(REL_MAX_DISTANCE / max_exact)
                         * (REL_BUCKETS - max_exact)).astype(jnp.int32)
    large = jnp.minimum(large, REL_BUCKETS - 1)
    bucket = jnp.where(n < max_exact, n, large)
    far = relb_ref[REL_BUCKETS - 1, head]
    bias = jnp.zeros((MOBA_BLOCK, MOBA_BLOCK), F32)
    for t in range(REL_BUCKETS):
        bias = jnp.where(bucket == t, relb_ref[t, head] - far, bias)
    return jnp.where(dist >= 0, bias * LOG2E, NEG_BIG)


def _moba_kernel(relb_ref, q_ref, qn_ref, k_ref, v_ref, z_ref, o_ref,
                 kaug_ref, vt_ref, kmean_ref, bias_ref, acc_ref, qa_ref, s_ref, *, nb):
    hp = pl.program_id(0)
    bi = pl.program_id(1)
    i = pl.program_id(2)
    par = i % 2
    HD = ATTN_HEAD_DIM
    lane = lax.broadcasted_iota(jnp.int32, (MOBA_BLOCK, 128), 1)
    lo_half = lane < HD
    er = lax.broadcasted_iota(jnp.int32, (128, 128), 0)
    ec = lax.broadcasted_iota(jnp.int32, (128, 128), 1)
    eye = jnp.where(er == ec, 1.0, 0.0).astype(BF16)
    blk = lax.broadcasted_iota(jnp.int32, (HD, MOBA_BLOCK), 0)

    def gate_scores(q):
        qt = _nt(eye, (q * (HD ** -0.5 * LOG2E)).astype(BF16))
        q0, q1 = _split2(q)
        gates = []
        for hh in range(2):
            k0, k1 = _split2(kmean_ref[hh])
            gates.append(_nt(k0, q0) + (_nt(k0, q1) + _nt(k1, q0)))
        return qt, gates

    def store_operands(slot, qt, gates, tile):
        for hh in range(2):
            g = jnp.where(blk < tile, gates[hh], -jnp.inf)
            sel = blk == tile
            for _ in range(MOBA_TOPK):
                mx = jnp.max(g, axis=0, keepdims=True)
                first = jnp.min(jnp.where(g == mx, blk, HD), axis=0, keepdims=True)
                hit = (blk == first) & (mx > -jnp.inf)
                sel = sel | hit
                g = jnp.where(hit, -jnp.inf, g)
            mask = jnp.where(sel, 0.0, NEG_BIG)
            rows = [qt[0:HD], mask] if hh == 0 else [mask, qt[HD:2 * HD]]
            qa_ref[slot, hh] = jnp.concatenate(rows, axis=0).astype(BF16)

    @pl.when(i == 0)
    def _prepare():
        kmean_ref[...] = jnp.zeros_like(kmean_ref)
        lane1 = lax.broadcasted_iota(jnp.int32, (1, 128), 1)
        ones_rows = jnp.ones((VT_ROWS - HD, MOBA_BLOCK), BF16)

        def prep(j, carry):
            rows = pl.ds(pl.multiple_of(j * MOBA_BLOCK, MOBA_BLOCK), MOBA_BLOCK)
            kb = k_ref[0, rows, :]
            km = jnp.mean(kb, axis=0, keepdims=True)
            kmean_ref[0, pl.ds(j, 1), :] = jnp.where(lane1 < HD, km, 0.0)
            kmean_ref[1, pl.ds(j, 1), :] = jnp.where(lane1 < HD, 0.0, km)
            kaug_ref[0, j] = jnp.where(lo_half, kb, jnp.where(lane == HD + j, 1.0, 0.0)).astype(BF16)
            kaug_ref[1, j] = jnp.where(lo_half, jnp.where(lane == j, 1.0, 0.0), kb).astype(BF16)
            vt = _nt(eye, v_ref[0, rows, :]).astype(BF16)
            for hh in range(2):
                vt_ref[j, hh] = jnp.concatenate([vt[hh * HD:(hh + 1) * HD], ones_rows], axis=0)
            return carry

        lax.fori_loop(0, nb, prep, 0, unroll=4)
        kaug_ref[0, nb] = jnp.where(lane == 2 * HD - 1, 1.0, 0.0).astype(BF16)
        kaug_ref[1, nb] = jnp.where(lane == HD - 1, 1.0, 0.0).astype(BF16)
        vt_ref[nb] = jnp.zeros((2, VT_ROWS, MOBA_BLOCK), BF16)
        store_operands(0, *gate_scores(q_ref[0]), 0)

    @pl.when((i == 0) & (bi == 0))
    def _bias_tiles():
        for hh in range(2):
            bias_ref[hh, 0] = _t5_bias_tile_t(relb_ref, 2 * hp + hh, 0)
            bias_ref[hh, 1] = _t5_bias_tile_t(relb_ref, 2 * hp + hh, MOBA_BLOCK)

    def issue(j, buf_ref):
        for hh in range(2):
            buf_ref[hh] = _nn(kaug_ref[hh, j], qa_ref[par, hh])

    def consume(j, buf_ref, st, bias_idx=None):
        out = []
        for hh in range(2):
            s = buf_ref[hh]
            if bias_idx is not None:
                s = s + bias_ref[hh, bias_idx]
            m_new = jnp.maximum(st[hh], jnp.max(s, axis=0, keepdims=True))
            p = jnp.exp2((s - m_new).astype(BF16))
            pv = _nn(vt_ref[j, hh], p)
            acc_ref[hh] = jnp.exp2(st[hh] - m_new) * acc_ref[hh] + pv
            out.append(m_new)
        return tuple(out)

    nfar = jnp.maximum(i - 1, 0)
    far = lambda j: jnp.where(j < nfar, j, nb)
    prev = jnp.where(i >= 1, i - 1, nb)
    a0_ref, b0_ref, a1_ref, b1_ref = (s_ref.at[n] for n in range(4))
    qt_next, gates_next = gate_scores(qn_ref[0])
    issue(i, a0_ref)
    issue(prev, b0_ref)
    issue(far(0), a1_ref)
    issue(far(1), b1_ref)
    store_operands(1 - par, qt_next, gates_next, i + 1)
    acc_ref[...] = jnp.zeros_like(acc_ref)
    st = (jnp.full((1, MOBA_BLOCK), 2 * NEG_BIG, F32),) * 2
    st = consume(i, a0_ref, st, bias_idx=0)
    issue(far(2), a0_ref)
    st = consume(prev, b0_ref, st, bias_idx=1)
    issue(far(3), b0_ref)
    st = consume(far(0), a1_ref, st)
    st = consume(far(1), b1_ref, st)

    def far_quad(j, st):
        issue(far(j + 2), a1_ref)
        issue(far(j + 3), b1_ref)
        st = consume(far(j), a0_ref, st)
        issue(far(j + 4), a0_ref)
        st = consume(far(j + 1), b0_ref, st)
        issue(far(j + 5), b0_ref)
        st = consume(far(j + 2), a1_ref, st)
        return consume(far(j + 3), b1_ref, st)

    nrest = jnp.maximum(nfar - 2, 0)
    n8 = nrest // 8
    st = lax.fori_loop(0, n8, lambda u, st: far_quad(8 * u + 6, far_quad(8 * u + 2, st)), st)
    n4 = (nrest - 8 * n8) // 4
    st = lax.fori_loop(0, n4, lambda u, st: far_quad(8 * n8 + 4 * u + 2, st), st)
    base = 8 * n8 + 4 * n4 + 2
    rem = nfar - base

    def rem_first(st):
        issue(far(base + 2), a1_ref)
        st = consume(far(base), a0_ref, st)
        return consume(far(base + 1), b0_ref, st)

    st = lax.cond(rem >= 1, rem_first, lambda st: st, st)
    lax.cond(rem >= 3, lambda st: consume(far(base + 2), a1_ref, st), lambda st: st, st)

    ot = jnp.concatenate([acc_ref[hh, 0:HD, :] / acc_ref[hh, HD:HD + 1, :] for hh in range(2)], axis=0)
    o_ref[0] = (ot.T * _silu(z_ref[0])).astype(o_ref.dtype)


def _moba(rel_bias, q, k, v, z):
    b, s, _ = q.shape
    nb = s // MOBA_BLOCK
    assert s % MOBA_BLOCK == 0 and nb < ATTN_HEAD_DIM
    pairs = ATTN_HEADS // 2
    blk = lambda hp, bi, i: (bi, i, hp)
    whole = lambda hp, bi, i: (bi, 0, hp)
    return pl.pallas_call(
        functools.partial(_moba_kernel, nb=nb),
        grid=(pairs, b, nb),
        in_specs=[
            pl.BlockSpec(memory_space=pltpu.SMEM),
            pl.BlockSpec((1, MOBA_BLOCK, 128), blk),
            pl.BlockSpec((1, MOBA_BLOCK, 128), lambda hp, bi, i: (bi, jnp.minimum(i + 1, nb - 1), hp)),
            pl.BlockSpec((1, s, 128), whole),
            pl.BlockSpec((1, s, 128), whole),
            pl.BlockSpec((1, MOBA_BLOCK, 128), blk),
        ],
        out_specs=pl.BlockSpec((1, MOBA_BLOCK, 128), blk),
        out_shape=jax.ShapeDtypeStruct((b, s, ATTN_WIDTH), BF16),
        scratch_shapes=[
            pltpu.VMEM((2, nb + 1, MOBA_BLOCK, 128), BF16),
            pltpu.VMEM((nb + 1, 2, VT_ROWS, MOBA_BLOCK), BF16),
            pltpu.VMEM((2, ATTN_HEAD_DIM, 128), F32),
            pltpu.VMEM((2, 2, MOBA_BLOCK, MOBA_BLOCK), F32),
            pltpu.VMEM((2, VT_ROWS, MOBA_BLOCK), F32),
            pltpu.VMEM((2, 2, 128, MOBA_BLOCK), BF16),
            pltpu.VMEM((4, 2, MOBA_BLOCK, MOBA_BLOCK), F32),
        ],
        compiler_params=pltpu.CompilerParams(
            dimension_semantics=("arbitrary", "arbitrary", "arbitrary"), vmem_limit_bytes=VMEM_LIMIT),
        name="moba",
    )(rel_bias, q, q, k, v, z)


DN_ROWS = DN_HEADS * DN_CHUNK


def _split2(a):
    a0 = a.astype(BF16)
    return a0, (a - a0.astype(F32)).astype(BF16)


def _split3(a):
    a0 = a.astype(BF16)
    r1 = a - a0.astype(F32)
    a1 = r1.astype(BF16)
    return a0, a1, (r1 - a1.astype(F32)).astype(BF16)


def _dot(a, b, dims=(((1,), (0,)), ((), ()))):
    return lax.dot_general(a.astype(BF16), b.astype(BF16), dims, preferred_element_type=F32)


_NT = (((1,), (1,)), ((), ()))
_TN = (((0,), (0,)), ((), ()))


def _dot_sel(sel, b):
    return sum(_nn(sel, p) for p in _split2(b))


def _deltanet_kernel(alog_ref, dtb_ref, x_ref, z_ref, g_ref, cw_ref, nw_ref, res_ref, ya_ref, wa_ref, wd_ref,
                     o_ref, state_ref, pad_ref, act_ref, rep_ref, yd_ref, *, tb):
    t = pl.program_id(1)
    C, R, W, H, D = DN_CHUNK, DN_ROWS, DN_WIDTH, DN_HEADS, DN_HEAD

    @pl.when(t == 0)
    def _reset():
        state_ref[...] = jnp.zeros_like(state_ref)
        pad_ref[0:8, :] = jnp.zeros((8, 3 * W), F32)

    pad_ref[8:8 + tb, :] = x_ref[0]
    for gcol in range(3 * H):
        cols = slice(gcol * D, (gcol + 1) * D)
        xp = pad_ref[:, cols]
        acc = xp * cw_ref[0:1, cols]
        for w in range(1, DN_CONV_WIDTH):
            acc = xp * cw_ref[w:w + 1, cols] + pltpu.roll(acc, 1, 0)
        a = _silu(acc[8:8 + tb])
        if gcol < 2 * H:
            a = a * lax.rsqrt(jnp.sum(a * a, axis=-1, keepdims=True) + EPS)
            if gcol < H:
                a = a * (D ** -0.5)
        act_ref[:, cols] = a
    pad_ref[0:8, :] = x_ref[0, tb - 8:tb, :]

    lane1 = lax.broadcasted_iota(jnp.int32, (1, GATE_PAD), 1)
    alog_row = jnp.zeros((1, GATE_PAD), F32)
    dtb_row = jnp.zeros((1, GATE_PAD), F32)
    for h in range(H):
        alog_row = jnp.where(lane1 == H + h, alog_ref[h], alog_row)
        dtb_row = jnp.where(lane1 == H + h, dtb_ref[h], dtb_row)
    gates = g_ref[0]
    xs = gates + dtb_row
    softplus = jnp.maximum(xs, 0.0) + jnp.log(1.0 + jnp.exp(-jnp.abs(xs)))
    gval = jnp.where(lane1 < H, 1.0 / (1.0 + jnp.exp(-gates)), -jnp.exp(alog_row) * softplus)
    tr = lax.broadcasted_iota(jnp.int32, (tb, tb), 0)
    tc = lax.broadcasted_iota(jnp.int32, (tb, tb), 1)
    gcum = _dot_sel(jnp.where((tr // C == tc // C) & (tr >= tc), 1.0, 0.0).astype(BF16), gval)
    er = lax.broadcasted_iota(jnp.int32, (GATE_PAD, 2 * H * D), 0)
    ec = lax.broadcasted_iota(jnp.int32, (GATE_PAD, 2 * H * D), 1)
    spread = jnp.where(er == ec // D, 1.0, 0.0).astype(BF16)
    rep_ref[...] = sum(_nn(p, spread) for p in _split2(jnp.where(lane1 < H, gval, gcum)))

    r = lax.broadcasted_iota(jnp.int32, (R, R), 0)
    c = lax.broadcasted_iota(jnp.int32, (R, R), 1)
    same = (r // C) == (c // C)
    incl = same & (r >= c)
    strict = same & (r > c)
    eye = jnp.where(r == c, 1.0, 0.0)
    lane_r = lax.broadcasted_iota(jnp.int32, (R, D), 1)

    chunks = range(tb // C)
    each = lambda f, *seqs: [f(*a) for a in zip(*seqs)]
    rows = [slice(ci * C, (ci + 1) * C) for ci in chunks]
    stack = lambda ref, c0: [jnp.concatenate([ref[rw, c0 + D * h:c0 + D * (h + 1)] for h in range(H)], axis=0)
                             for rw in rows]
    qc, kc, vc = stack(act_ref, 0), stack(act_ref, W), stack(act_ref, 2 * W)
    beta = stack(rep_ref, 0)
    gc = stack(rep_ref, H * D)

    pick = jnp.where((lax.broadcasted_iota(jnp.int32, (16, D), 0) == 0)
                     & (lax.broadcasted_iota(jnp.int32, (16, D), 1) < 3), 1.0, 0.0).astype(BF16)

    def pair_diff(g):
        p0, p1, p2 = (p.astype(F32) for p in _split3(g))
        parts = jnp.where(lane_r == 0, p0, jnp.where(lane_r == 1, p1, jnp.where(lane_r == 2, p2, 0.0)))
        g_row = _nt(pick, parts.astype(BF16))[0:1, :]
        return jnp.concatenate([g, g], axis=1) - g_row

    decay = each(lambda g: jnp.where(incl, jnp.exp(jnp.where(incl, pair_diff(g), 0.0)), 0.0), gc)
    kb = each(lambda k, b: k * b, kc, beta)
    x = each(lambda a, k, d: -jnp.where(strict, _dot(a, k, _NT) * d, 0.0), kb, kc, decay)
    sq = lambda m: each(lambda a: _dot(a, a), m)
    mul = lambda ma, mb: each(lambda a, b: _dot(a, b), ma, mb)
    add = lambda ma, mb: each(lambda a, b: a + b, ma, mb)
    plus_eye = lambda m: each(lambda a: eye + a, m)
    x2 = sq(x)
    x4 = sq(x2)
    f1 = plus_eye(x)
    m1 = add(f1, mul(f1, x2))
    x8 = sq(x4)
    f4 = plus_eye(x4)
    m2 = add(f4, mul(f4, x8))
    x16 = sq(x8)
    m12 = mul(m1, m2)
    x32 = sq(x16)
    f16 = plus_eye(x16)
    m3 = add(f16, mul(f16, x32))
    tinv = mul(m12, m3)
    eg = each(jnp.exp, gc)
    uw = each(lambda t, v, b, k, e: _dot(t, jnp.concatenate([v * b, k * e], axis=1)),
              tinv, vc, beta, kb, eg)
    a_intra = each(lambda q, k, d: jnp.where(incl, _dot(q, k, _NT) * d, 0.0), qc, kc, decay)
    glast = [[g[h * C + C - 1:h * C + C, :] for h in range(H)] for g in gc]
    q_dec = each(lambda q, e: q * e, qc, eg)
    k_dec = each(lambda k, g, gl: k * jnp.exp(
        jnp.concatenate([jnp.broadcast_to(t, (C, D)) for t in gl], axis=0) - g), kc, gc, glast)

    hs = [slice(h * C, (h + 1) * C) for h in range(H)]
    for ci in chunks:
        state = [state_ref[h] for h in range(H)]
        wq = [_dot(jnp.concatenate([uw[ci][hs[h], D:2 * D], q_dec[ci][hs[h]]], axis=0), state[h])
              for h in range(H)]
        vn = [uw[ci][hs[h], 0:D] - wq[h][0:C] for h in range(H)]
        for h in range(H):
            state_ref[h] = state[h] * jnp.exp(glast[ci][h]) + _dot(k_dec[ci][hs[h]], vn[h], _TN)
        o = jnp.concatenate([wq[h][C:2 * C] for h in range(H)], axis=0) + _dot(
            a_intra[ci], jnp.concatenate(vn, axis=0))
        on = o * lax.rsqrt(jnp.mean(o * o, axis=-1, keepdims=True) + EPS) * nw_ref[...]
        for h in range(H):
            cols = slice(h * D, (h + 1) * D)
            yd_ref[rows[ci], cols] = (on[hs[h]] * _silu(z_ref[0, rows[ci], cols])).astype(yd_ref.dtype)

    o_ref[0] = res_ref[0] + _nn(ya_ref[0], wa_ref[...]) + _nn(yd_ref[...], wd_ref[...])


def _deltanet(a_log, dt_bias, qkv, z, gates, conv_w, dn_norm_w, x, ya, wa, wd, tb):
    b, s, _ = qkv.shape
    assert s % tb == 0 and tb % DN_CHUNK == 0
    blk = lambda bi, t: (bi, t, 0)
    fixed = lambda bi, t: (0, 0)
    return pl.pallas_call(
        functools.partial(_deltanet_kernel, tb=tb),
        grid=(b, s // tb),
        in_specs=[
            pl.BlockSpec(memory_space=pltpu.SMEM),
            pl.BlockSpec(memory_space=pltpu.SMEM),
            pl.BlockSpec((1, tb, 3 * DN_WIDTH), blk),
            pl.BlockSpec((1, tb, DN_WIDTH), blk),
            pl.BlockSpec((1, tb, GATE_PAD), blk),
            pl.BlockSpec((DN_CONV_WIDTH, 3 * DN_WIDTH), fixed),
            pl.BlockSpec((1, DN_HEAD), fixed),
            pl.BlockSpec((1, tb, D_MODEL), blk),
            pl.BlockSpec((1, tb, ATTN_WIDTH), blk),
            pl.BlockSpec((ATTN_WIDTH, D_MODEL), fixed),
            pl.BlockSpec((DN_WIDTH, D_MODEL), fixed),
        ],
        out_specs=pl.BlockSpec((1, tb, D_MODEL), blk),
        out_shape=jax.ShapeDtypeStruct((b, s, D_MODEL), F32),
        scratch_shapes=[
            pltpu.VMEM((DN_HEADS, DN_HEAD, DN_HEAD), F32),
            pltpu.VMEM((tb + 8, 3 * DN_WIDTH), F32),
            pltpu.VMEM((tb, 3 * DN_WIDTH), F32),
            pltpu.VMEM((tb, 2 * DN_HEADS * DN_HEAD), F32),
            pltpu.VMEM((tb, DN_WIDTH), BF16),
        ],
        compiler_params=pltpu.CompilerParams(
            dimension_semantics=("arbitrary", "arbitrary"), vmem_limit_bytes=VMEM_LIMIT),
        name="deltanet",
    )(a_log, dt_bias, qkv, z, gates, conv_w, dn_norm_w, x, ya, wa, wd)


def _row_tile(n):
    for tm in (512, 256, 128, 64, 32, 16, 8):
        if n % tm == 0:
            return tm
    raise ValueError(f"row count {n} is not a multiple of 8")


def kernel(x, rel_bias, norm_w, w_in, q_norm_w, k_norm_w, conv_w, a_log, dt_bias, dn_norm_w, w_out):
    b, s, d = x.shape
    assert d == D_MODEL and norm_w.shape[0] == 1, "single-layer kernel"
    n = b * s
    x2 = x.reshape(n, d)
    tm = _row_tile(n)

    w_main = w_in[0][:, :MAIN_COLS].astype(BF16)
    w_gate = jnp.pad(w_in[0][:, MAIN_COLS:], ((0, 0), (0, GATE_PAD - 2 * DN_HEADS))).astype(BF16)
    qnw = jnp.tile(q_norm_w[0], ATTN_HEADS)[None, :]
    knw = jnp.tile(k_norm_w[0], ATTN_HEADS)[None, :]

    q, k, v, za, qkv, zd, gates = _inproj(x2, norm_w, w_main, w_gate, qnw, knw, tm)

    r3 = lambda t: t.reshape(b, s, t.shape[-1])
    ya = _moba(rel_bias, r3(q), r3(k), r3(v), r3(za))
    tb = 256 if s % 256 == 0 else DN_CHUNK
    w_o = w_out[0].astype(BF16)
    return _deltanet(a_log[0], dt_bias[0], r3(qkv), r3(zd), r3(gates), conv_w[0], dn_norm_w,
                     x, ya, w_o[:ATTN_WIDTH], w_o[ATTN_WIDTH:], tb)
```

```python
import functools
import math

import jax
import jax.numpy as jnp
from jax import lax
from jax.experimental import pallas as pl
from jax.experimental.pallas import tpu as pltpu

F32 = jnp.float32
BF16 = jnp.bfloat16

D_MODEL = 1024
ATTN_HEADS = 8
ATTN_HEAD_DIM = 64
ATTN_WIDTH = ATTN_HEADS * ATTN_HEAD_DIM
MOBA_BLOCK = 256
MOBA_TOPK = 3
REL_BUCKETS = 32
REL_MAX_DISTANCE = 128
DN_HEADS = 4
DN_HEAD = 128
DN_WIDTH = DN_HEADS * DN_HEAD
DN_CONV_WIDTH = 4
DN_CHUNK = 64
MAIN_COLS = 4 * ATTN_WIDTH + 3 * DN_WIDTH + DN_WIDTH
GATE_PAD = 128
EPS = 1e-6
LOG2E = math.log2(math.e)
NEG_BIG = -(2.0 ** 30)
VT_ROWS = ATTN_HEAD_DIM + 16
VMEM_LIMIT = 56 * 1024 * 1024


def _nt(a, b):
    return lax.dot_general(a, b, (((1,), (1,)), ((), ())), preferred_element_type=F32)


def _nn(a, b):
    return lax.dot_general(a, b, (((1,), (0,)), ((), ())), preferred_element_type=F32)


def _silu(x):
    half = 0.5 * x
    return half + half * jnp.tanh(half)


def _inproj_kernel(x_ref, nw_ref, w_ref, wg_ref, qnw_ref, knw_ref,
                   q_ref, k_ref, v_ref, za_ref, qkv_ref, zd_ref, g_ref):
    x = x_ref[...]
    ms = jnp.mean(x * x, axis=-1, keepdims=True)
    h = (x * lax.rsqrt(ms + EPS) * nw_ref[...]).astype(BF16)

    r_i = lax.broadcasted_iota(jnp.int32, (256, 256), 0)
    c_i = lax.broadcasted_iota(jnp.int32, (256, 256), 1)
    same_head = jnp.where(r_i // ATTN_HEAD_DIM == c_i // ATTN_HEAD_DIM, 1.0, 0.0).astype(BF16)

    def head_rms(t, w):
        t2 = (t * t).astype(BF16)
        ss = jnp.concatenate([_nn(t2[:, c0:c0 + 256], same_head) for c0 in range(0, ATTN_WIDTH, 256)], axis=1)
        return t * lax.rsqrt(ss * (1.0 / ATTN_HEAD_DIM) + EPS) * w

    def proj(c0, width):
        return _nn(h, w_ref[:, c0:c0 + width])

    q_ref[...] = head_rms(proj(0, ATTN_WIDTH), qnw_ref[...]).astype(q_ref.dtype)
    k_ref[...] = head_rms(proj(ATTN_WIDTH, ATTN_WIDTH), knw_ref[...]).astype(k_ref.dtype)
    v_ref[...] = proj(2 * ATTN_WIDTH, ATTN_WIDTH).astype(v_ref.dtype)
    za_ref[...] = proj(3 * ATTN_WIDTH, ATTN_WIDTH).astype(za_ref.dtype)
    for c in range(3):
        qkv_ref[:, c * DN_WIDTH:(c + 1) * DN_WIDTH] = proj(4 * ATTN_WIDTH + c * DN_WIDTH, DN_WIDTH).astype(qkv_ref.dtype)
    zd_ref[...] = proj(4 * ATTN_WIDTH + 3 * DN_WIDTH, DN_WIDTH).astype(zd_ref.dtype)
    g_ref[...] = _nn(h, wg_ref[...])


def _inproj(x2, norm_w, w_main, w_gate, qnw, knw, tm):
    n = x2.shape[0]
    row = lambda i: (i, 0)
    fixed = lambda i: (0, 0)
    outs = [
        jax.ShapeDtypeStruct((n, ATTN_WIDTH), F32),
        jax.ShapeDtypeStruct((n, ATTN_WIDTH), F32),
        jax.ShapeDtypeStruct((n, ATTN_WIDTH), BF16),
        jax.ShapeDtypeStruct((n, ATTN_WIDTH), F32),
        jax.ShapeDtypeStruct((n, 3 * DN_WIDTH), F32),
        jax.ShapeDtypeStruct((n, DN_WIDTH), F32),
        jax.ShapeDtypeStruct((n, GATE_PAD), F32),
    ]
    return pl.pallas_call(
        _inproj_kernel,
        grid=(n // tm,),
        in_specs=[
            pl.BlockSpec((tm, D_MODEL), row),
            pl.BlockSpec((1, D_MODEL), fixed),
            pl.BlockSpec((D_MODEL, MAIN_COLS), fixed),
            pl.BlockSpec((D_MODEL, GATE_PAD), fixed),
            pl.BlockSpec((1, ATTN_WIDTH), fixed),
            pl.BlockSpec((1, ATTN_WIDTH), fixed),
        ],
        out_specs=[pl.BlockSpec((tm, o.shape[1]), row) for o in outs],
        out_shape=outs,
        compiler_params=pltpu.CompilerParams(dimension_semantics=("arbitrary",), vmem_limit_bytes=VMEM_LIMIT),
        name="inproj",
    )(x2, norm_w, w_main, w_gate, qnw, knw)


def _t5_bias_tile_t(relb_ref, head, offset):
    c = lax.broadcasted_iota(jnp.int32, (MOBA_BLOCK, MOBA_BLOCK), 0)
    r = lax.broadcasted_iota(jnp.int32, (MOBA_BLOCK, MOBA_BLOCK), 1)
    dist = r - c + offset
    n = jnp.maximum(dist, 0)
    max_exact = REL_BUCKETS // 2
    nf = jnp.maximum(n, 1).astype(F32)
    large = max_exact + (jnp.log(nf / max_exact) / math.log(REL_MAX_DISTANCE / max_exact)
                         * (REL_BUCKETS - max_exact)).astype(jnp.int32)
    large = jnp.minimum(large, REL_BUCKETS - 1)
    bucket = jnp.where(n < max_exact, n, large)
    far = relb_ref[REL_BUCKETS - 1, head]
    bias = jnp.zeros((MOBA_BLOCK, MOBA_BLOCK), F32)
    for t in range(REL_BUCKETS):
        bias = jnp.where(bucket == t, relb_ref[t, head] - far, bias)
    return jnp.where(dist >= 0, bias * LOG2E, NEG_BIG)


def _moba_kernel(relb_ref, q_ref, qn_ref, k_ref, v_ref, z_ref, o_ref,
                 kaug_ref, vt_ref, kmean_ref, bias_ref, acc_ref, qa_ref, s_ref, *, nb):
    hp = pl.program_id(0)
    bi = pl.program_id(1)
    i = pl.program_id(2)
    par = i % 2
    HD = ATTN_HEAD_DIM
    lane = lax.broadcasted_iota(jnp.int32, (MOBA_BLOCK, 128), 1)
    lo_half = lane < HD
    er = lax.broadcasted_iota(jnp.int32, (128, 128), 0)
    ec = lax.broadcasted_iota(jnp.int32, (128, 128), 1)
    eye = jnp.where(er == ec, 1.0, 0.0).astype(BF16)
    blk = lax.broadcasted_iota(jnp.int32, (HD, MOBA_BLOCK), 0)

    def gate_scores(q):
        qt = _nt(eye, (q * (HD ** -0.5 * LOG2E)).astype(BF16))
        q0, q1 = _split2(q)
        gates = []
        for hh in range(2):
            k0, k1 = _split2(kmean_ref[hh])
            gates.append(_nt(k0, q0) + (_nt(k0, q1) + _nt(k1, q0)))
        return qt, gates

    def store_operands(slot, qt, gates, tile):
        for hh in range(2):
            g = jnp.where(blk < tile, gates[hh], -jnp.inf)
            sel = blk == tile
            for _ in range(MOBA_TOPK):
                mx = jnp.max(g, axis=0, keepdims=True)
                first = jnp.min(jnp.where(g == mx, blk, HD), axis=0, keepdims=True)
                hit = (blk == first) & (mx > -jnp.inf)
                sel = sel | hit
                g = jnp.where(hit, -jnp.inf, g)
            mask = jnp.where(sel, 0.0, NEG_BIG)
            rows = [qt[0:HD], mask] if hh == 0 else [mask, qt[HD:2 * HD]]
            qa_ref[slot, hh] = jnp.concatenate(rows, axis=0).astype(BF16)

    @pl.when(i == 0)
    def _prepare():
        kmean_ref[...] = jnp.zeros_like(kmean_ref)
        lane1 = lax.broadcasted_iota(jnp.int32, (1, 128), 1)
        ones_rows = jnp.ones((VT_ROWS - HD, MOBA_BLOCK), BF16)

        def prep(j, carry):
            rows = pl.ds(pl.multiple_of(j * MOBA_BLOCK, MOBA_BLOCK), MOBA_BLOCK)
            kb = k_ref[0, rows, :]
            km = jnp.mean(kb, axis=0, keepdims=True)
            kmean_ref[0, pl.ds(j, 1), :] = jnp.where(lane1 < HD, km, 0.0)
            kmean_ref[1, pl.ds(j, 1), :] = jnp.where(lane1 < HD, 0.0, km)
            kaug_ref[0, j] = jnp.where(lo_half, kb, jnp.where(lane == HD + j, 1.0, 0.0)).astype(BF16)
            kaug_ref[1, j] = jnp.where(lo_half, jnp.where(lane == j, 1.0, 0.0), kb).astype(BF16)
            vt = _nt(eye, v_ref[0, rows, :]).astype(BF16)
            for hh in range(2):
                vt_ref[j, hh] = jnp.concatenate([vt[hh * HD:(hh + 1) * HD], ones_rows], axis=0)
            return carry

        lax.fori_loop(0, nb, prep, 0, unroll=4)
        kaug_ref[0, nb] = jnp.where(lane == 2 * HD - 1, 1.0, 0.0).astype(BF16)
        kaug_ref[1, nb] = jnp.where(lane == HD - 1, 1.0, 0.0).astype(BF16)
        vt_ref[nb] = jnp.zeros((2, VT_ROWS, MOBA_BLOCK), BF16)
        store_operands(0, *gate_scores(q_ref[0]), 0)

    @pl.when((i == 0) & (bi == 0))
    def _bias_tiles():
        for hh in range(2):
            bias_ref[hh, 0] = _t5_bias_tile_t(relb_ref, 2 * hp + hh, 0)
            bias_ref[hh, 1] = _t5_bias_tile_t(relb_ref, 2 * hp + hh, MOBA_BLOCK)

    def issue(j, buf_ref):
        for hh in range(2):
            buf_ref[hh] = _nn(kaug_ref[hh, j], qa_ref[par, hh])

    def consume(j, buf_ref, st, bias_idx=None):
        out = []
        for hh in range(2):
            s = buf_ref[hh]
            if bias_idx is not None:
                s = s + bias_ref[hh, bias_idx]
            m_new = jnp.maximum(st[hh], jnp.max(s, axis=0, keepdims=True))
            p = jnp.exp2((s - m_new).astype(BF16))
            pv = _nn(vt_ref[j, hh], p)
            acc_ref[hh] = jnp.exp2(st[hh] - m_new) * acc_ref[hh] + pv
            out.append(m_new)
        return tuple(out)

    nfar = jnp.maximum(i - 1, 0)
    far = lambda j: jnp.where(j < nfar, j, nb)
    prev = jnp.where(i >= 1, i - 1, nb)
    a0_ref, b0_ref, a1_ref, b1_ref = (s_ref.at[n] for n in range(4))
    qt_next, gates_next = gate_scores(qn_ref[0])
    issue(i, a0_ref)
    issue(prev, b0_ref)
    issue(far(0), a1_ref)
    issue(far(1), b1_ref)
    store_operands(1 - par, qt_next, gates_next, i + 1)
    acc_ref[...] = jnp.zeros_like(acc_ref)
    st = (jnp.full((1, MOBA_BLOCK), 2 * NEG_BIG, F32),) * 2
    st = consume(i, a0_ref, st, bias_idx=0)
    issue(far(2), a0_ref)
    st = consume(prev, b0_ref, st, bias_idx=1)
    issue(far(3), b0_ref)
    st = consume(far(0), a1_ref, st)
    st = consume(far(1), b1_ref, st)

    def far_quad(j, st):
        issue(far(j + 2), a1_ref)
        issue(far(j + 3), b1_ref)
        st = consume(far(j), a0_ref, st)
        issue(far(j + 4), a0_ref)
        st = consume(far(j + 1), b0_ref, st)
        issue(far(j + 5), b0_ref)
        st = consume(far(j + 2), a1_ref, st)
        return consume(far(j + 3), b1_ref, st)

    nrest = jnp.maximum(nfar - 2, 0)
    n8 = nrest // 8
    st = lax.fori_loop(0, n8, lambda u, st: far_quad(8 * u + 6, far_quad(8 * u + 2, st)), st)
    n4 = (nrest - 8 * n8) // 4
    st = lax.fori_loop(0, n4, lambda u, st: far_quad(8 * n8 + 4 * u + 2, st), st)
    base = 8 * n8 + 4 * n4 + 2
    rem = nfar - base

    def rem_first(st):
        issue(far(base + 2), a1_ref)
        st = consume(far(base), a0_ref, st)
        return consume(far(base + 1), b0_ref, st)

    st = lax.cond(rem >= 1, rem_first, lambda st: st, st)
    lax.cond(rem >= 3, lambda st: consume(far(base + 2), a1_ref, st), lambda st: st, st)

    ot = jnp.concatenate([acc_ref[hh, 0:HD, :] / acc_ref[hh, HD:HD + 1, :] for hh in range(2)], axis=0)
    o_ref[0] = (ot.T * _silu(z_ref[0])).astype(o_ref.dtype)


def _moba(rel_bias, q, k, v, z):
    b, s, _ = q.shape
    nb = s // MOBA_BLOCK
    assert s % MOBA_BLOCK == 0 and nb < ATTN_HEAD_DIM
    pairs = ATTN_HEADS // 2
    blk = lambda hp, bi, i: (bi, i, hp)
    whole = lambda hp, bi, i: (bi, 0, hp)
    return pl.pallas_call(
        functools.partial(_moba_kernel, nb=nb),
        grid=(pairs, b, nb),
        in_specs=[
            pl.BlockSpec(memory_space=pltpu.SMEM),
            pl.BlockSpec((1, MOBA_BLOCK, 128), blk),
            pl.BlockSpec((1, MOBA_BLOCK, 128), lambda hp, bi, i: (bi, jnp.minimum(i + 1, nb - 1), hp)),
            pl.BlockSpec((1, s, 128), whole),
            pl.BlockSpec((1, s, 128), whole),
            pl.BlockSpec((1, MOBA_BLOCK, 128), blk),
        ],
        out_specs=pl.BlockSpec((1, MOBA_BLOCK, 128), blk),
        out_shape=jax.ShapeDtypeStruct((b, s, ATTN_WIDTH), BF16),
        scratch_shapes=[
            pltpu.VMEM((2, nb + 1, MOBA_BLOCK, 128), BF16),
            pltpu.VMEM((nb + 1, 2, VT_ROWS, MOBA_BLOCK), BF16),
            pltpu.VMEM((2, ATTN_HEAD_DIM, 128), F32),
            pltpu.VMEM((2, 2, MOBA_BLOCK, MOBA_BLOCK), F32),
            pltpu.VMEM((2, VT_ROWS, MOBA_BLOCK), F32),
            pltpu.VMEM((2, 2, 128, MOBA_BLOCK), BF16),
            pltpu.VMEM((4, 2, MOBA_BLOCK, MOBA_BLOCK), F32),
        ],
        compiler_params=pltpu.CompilerParams(
            dimension_semantics=("arbitrary", "arbitrary", "arbitrary"), vmem_limit_bytes=VMEM_LIMIT),
        name="moba",
    )(rel_bias, q, q, k, v, z)


DN_ROWS = DN_HEADS * DN_CHUNK


def _split2(a):
    a0 = a.astype(BF16)
    return a0, (a - a0.astype(F32)).astype(BF16)


def _split3(a):
    a0 = a.astype(BF16)
    r1 = a - a0.astype(F32)
    a1 = r1.astype(BF16)
    return a0, a1, (r1 - a1.astype(F32)).astype(BF16)


def _dot(a, b, dims=(((1,), (0,)), ((), ()))):
    return lax.dot_general(a.astype(BF16), b.astype(BF16), dims, preferred_element_type=F32)


_NT = (((1,), (1,)), ((), ()))
_TN = (((0,), (0,)), ((), ()))


def _dot_sel(sel, b):
    return sum(_nn(sel, p) for p in _split2(b))


def _deltanet_kernel(alog_ref, dtb_ref, x_ref, z_ref, g_ref, cw_ref, nw_ref, res_ref, ya_ref, wa_ref, wd_ref,
                     o_ref, state_ref, pad_ref, act_ref, rep_ref, yd_ref, *, tb):
    t = pl.program_id(1)
    C, R, W, H, D = DN_CHUNK, DN_ROWS, DN_WIDTH, DN_HEADS, DN_HEAD

    @pl.when(t == 0)
    def _reset():
        state_ref[...] = jnp.zeros_like(state_ref)
        pad_ref[0:8, :] = jnp.zeros((8, 3 * W), F32)

    pad_ref[8:8 + tb, :] = x_ref[0]
    for gcol in range(3 * H):
        cols = slice(gcol * D, (gcol + 1) * D)
        xp = pad_ref[:, cols]
        acc = xp * cw_ref[0:1, cols]
        for w in range(1, DN_CONV_WIDTH):
            acc = xp * cw_ref[w:w + 1, cols] + pltpu.roll(acc, 1, 0)
        a = _silu(acc[8:8 + tb])
        if gcol < 2 * H:
            a = a * lax.rsqrt(jnp.sum(a * a, axis=-1, keepdims=True) + EPS)
            if gcol < H:
                a = a * (D ** -0.5)
        act_ref[:, cols] = a
    pad_ref[0:8, :] = x_ref[0, tb - 8:tb, :]

    lane1 = lax.broadcasted_iota(jnp.int32, (1, GATE_PAD), 1)
    alog_row = jnp.zeros((1, GATE_PAD), F32)
    dtb_row = jnp.zeros((1, GATE_PAD), F32)
    for h in range(H):
        alog_row = jnp.where(lane1 == H + h, alog_ref[h], alog_row)
        dtb_row = jnp.where(lane1 == H + h, dtb_ref[h], dtb_row)
    gates = g_ref[0]
    xs = gates + dtb_row
    softplus = jnp.maximum(xs, 0.0) + jnp.log(1.0 + jnp.exp(-jnp.abs(xs)))
    gval = jnp.where(lane1 < H, 1.0 / (1.0 + jnp.exp(-gates)), -jnp.exp(alog_row) * softplus)
    tr = lax.broadcasted_iota(jnp.int32, (tb, tb), 0)
    tc = lax.broadcasted_iota(jnp.int32, (tb, tb), 1)
    gcum = _dot_sel(jnp.where((tr // C == tc // C) & (tr >= tc), 1.0, 0.0).astype(BF16), gval)
    er = lax.broadcasted_iota(jnp.int32, (GATE_PAD, 2 * H * D), 0)
    ec = lax.broadcasted_iota(jnp.int32, (GATE_PAD, 2 * H * D), 1)
    spread = jnp.where(er == ec // D, 1.0, 0.0).astype(BF16)
    rep_ref[...] = sum(_nn(p, spread) for p in _split2(jnp.where(lane1 < H, gval, gcum)))

    r = lax.broadcasted_iota(jnp.int32, (R, R), 0)
    c = lax.broadcasted_iota(jnp.int32, (R, R), 1)
    same = (r // C) == (c // C)
    incl = same & (r >= c)
    strict = same & (r > c)
    eye = jnp.where(r == c, 1.0, 0.0)
    lane_r = lax.broadcasted_iota(jnp.int32, (R, D), 1)

    chunks = range(tb // C)
    each = lambda f, *seqs: [f(*a) for a in zip(*seqs)]
    rows = [slice(ci * C, (ci + 1) * C) for ci in chunks]
    stack = lambda ref, c0: [jnp.concatenate([ref[rw, c0 + D * h:c0 + D * (h + 1)] for h in range(H)], axis=0)
                             for rw in rows]
    qc, kc, vc = stack(act_ref, 0), stack(act_ref, W), stack(act_ref, 2 * W)
    beta = stack(rep_ref, 0)
    gc = stack(rep_ref, H * D)

    pick = jnp.where((lax.broadcasted_iota(jnp.int32, (16, D), 0) == 0)
                     & (lax.broadcasted_iota(jnp.int32, (16, D), 1) < 3), 1.0, 0.0).astype(BF16)

    def pair_diff(g):
        p0, p1, p2 = (p.astype(F32) for p in _split3(g))
        parts = jnp.where(lane_r == 0, p0, jnp.where(lane_r == 1, p1, jnp.where(lane_r == 2, p2, 0.0)))
        g_row = _nt(pick, parts.astype(BF16))[0:1, :]
        return jnp.concatenate([g, g], axis=1) - g_row

    decay = each(lambda g: jnp.where(incl, jnp.exp(jnp.where(incl, pair_diff(g), 0.0)), 0.0), gc)
    kb = each(lambda k, b: k * b, kc, beta)
    x = each(lambda a, k, d: -jnp.where(strict, _dot(a, k, _NT) * d, 0.0), kb, kc, decay)
    sq = lambda m: each(lambda a: _dot(a, a), m)
    mul = lambda ma, mb: each(lambda a, b: _dot(a, b), ma, mb)
    add = lambda ma, mb: each(lambda a, b: a + b, ma, mb)
    plus_eye = lambda m: each(lambda a: eye + a, m)
    x2 = sq(x)
    x4 = sq(x2)
    f1 = plus_eye(x)
    m1 = add(f1, mul(f1, x2))
    x8 = sq(x4)
    f4 = plus_eye(x4)
    m2 = add(f4, mul(f4, x8))
    x16 = sq(x8)
    m12 = mul(m1, m2)
    x32 = sq(x16)
    f16 = plus_eye(x16)
    m3 = add(f16, mul(f16, x32))
    tinv = mul(m12, m3)
    eg = each(jnp.exp, gc)
    uw = each(lambda t, v, b, k, e: _dot(t, jnp.concatenate([v * b, k * e], axis=1)),
              tinv, vc, beta, kb, eg)
    a_intra = each(lambda q, k, d: jnp.where(incl, _dot(q, k, _NT) * d, 0.0), qc, kc, decay)
    glast = [[g[h * C + C - 1:h * C + C, :] for h in range(H)] for g in gc]
    q_dec = each(lambda q, e: q * e, qc, eg)
    k_dec = each(lambda k, g, gl: k * jnp.exp(
        jnp.concatenate([jnp.broadcast_to(t, (C, D)) for t in gl], axis=0) - g), kc, gc, glast)

    hs = [slice(h * C, (h + 1) * C) for h in range(H)]
    for ci in chunks:
        state = [state_ref[h] for h in range(H)]
        wq = [_dot(jnp.concatenate([uw[ci][hs[h], D:2 * D], q_dec[ci][hs[h]]], axis=0), state[h])
              for h in range(H)]
        vn = [uw[ci][hs[h], 0:D] - wq[h][0:C] for h in range(H)]
        for h in range(H):
            state_ref[h] = state[h] * jnp.exp(glast[ci][h]) + _dot(k_dec[ci][hs[h]], vn[h], _TN)
        o = jnp.concatenate([wq[h][C:2 * C] for h in range(H)], axis=0) + _dot(
            a_intra[ci], jnp.concatenate(vn, axis=0))
        on = o * lax.rsqrt(jnp.mean(o * o, axis=-1, keepdims=True) + EPS) * nw_ref[...]
        for h in range(H):
            cols = slice(h * D, (h + 1) * D)
            yd_ref[rows[ci], cols] = (on[hs[h]] * _silu(z_ref[0, rows[ci], cols])).astype(yd_ref.dtype)

    o_ref[0] = res_ref[0] + _nn(ya_ref[0], wa_ref[...]) + _nn(yd_ref[...], wd_ref[...])


def _deltanet(a_log, dt_bias, qkv, z, gates, conv_w, dn_norm_w, x, ya, wa, wd, tb):
    b, s, _ = qkv.shape
    assert s % tb == 0 and tb % DN_CHUNK == 0
    blk = lambda bi, t: (bi, t, 0)
    fixed = lambda bi, t: (0, 0)
    return pl.pallas_call(
        functools.partial(_deltanet_kernel, tb=tb),
        grid=(b, s // tb),
        in_specs=[
            pl.BlockSpec(memory_space=pltpu.SMEM),
            pl.BlockSpec(memory_space=pltpu.SMEM),
            pl.BlockSpec((1, tb, 3 * DN_WIDTH), blk),
            pl.BlockSpec((1, tb, DN_WIDTH), blk),
            pl.BlockSpec((1, tb, GATE_PAD), blk),
            pl.BlockSpec((DN_CONV_WIDTH, 3 * DN_WIDTH), fixed),
            pl.BlockSpec((1, DN_HEAD), fixed),
            pl.BlockSpec((1, tb, D_MODEL), blk),
            pl.BlockSpec((1, tb, ATTN_WIDTH), blk),
            pl.BlockSpec((ATTN_WIDTH, D_MODEL), fixed),
            pl.BlockSpec((DN_WIDTH, D_MODEL), fixed),
        ],
        out_specs=pl.BlockSpec((1, tb, D_MODEL), blk),
        out_shape=jax.ShapeDtypeStruct((b, s, D_MODEL), F32),
        scratch_shapes=[
            pltpu.VMEM((DN_HEADS, DN_HEAD, DN_HEAD), F32),
            pltpu.VMEM((tb + 8, 3 * DN_WIDTH), F32),
            pltpu.VMEM((tb, 3 * DN_WIDTH), F32),
            pltpu.VMEM((tb, 2 * DN_HEADS * DN_HEAD), F32),
            pltpu.VMEM((tb, DN_WIDTH), BF16),
        ],
        compiler_params=pltpu.CompilerParams(
            dimension_semantics=("arbitrary", "arbitrary"), vmem_limit_bytes=VMEM_LIMIT),
        name="deltanet",
    )(a_log, dt_bias, qkv, z, gates, conv_w, dn_norm_w, x, ya, wa, wd)


def _row_tile(n):
    for tm in (512, 256, 128, 64, 32, 16, 8):
        if n % tm == 0:
            return tm
    raise ValueError(f"row count {n} is not a multiple of 8")


def kernel(x, rel_bias, norm_w, w_in, q_norm_w, k_norm_w, conv_w, a_log, dt_bias, dn_norm_w, w_out):
    b, s, d = x.shape
    assert d == D_MODEL and norm_w.shape[0] == 1, "single-layer kernel"
    n = b * s
    x2 = x.reshape(n, d)
    tm = _row_tile(n)

    w_main = w_in[0][:, :MAIN_COLS].astype(BF16)
    w_gate = jnp.pad(w_in[0][:, MAIN_COLS:], ((0, 0), (0, GATE_PAD - 2 * DN_HEADS))).astype(BF16)
    qnw = jnp.tile(q_norm_w[0], ATTN_HEADS)[None, :]
    knw = jnp.tile(k_norm_w[0], ATTN_HEADS)[None, :]

    q, k, v, za, qkv, zd, gates = _inproj(x2, norm_w, w_main, w_gate, qnw, knw, tm)

    r3 = lambda t: t.reshape(b, s, t.shape[-1])
    ya = _moba(rel_bias, r3(q), r3(k), r3(v), r3(za))
    tb = 256 if s % 256 == 0 else DN_CHUNK
    w_o = w_out[0].astype(BF16)
    return _deltanet(a_log[0], dt_bias[0], r3(qkv), r3(zd), r3(gates), conv_w[0], dn_norm_w,
                     x, ya, w_o[:ATTN_WIDTH], w_o[ATTN_WIDTH:], tb)
```

```python
import functools
import math

import jax
import jax.numpy as jnp
from jax import lax
from jax.experimental import pallas as pl
from jax.experimental.pallas import tpu as pltpu

F32 = jnp.float32
BF16 = jnp.bfloat16

D_MODEL = 1024
ATTN_HEADS = 8
ATTN_HEAD_DIM = 64
ATTN_WIDTH = ATTN_HEADS * ATTN_HEAD_DIM
MOBA_BLOCK = 256
MOBA_TOPK = 3
REL_BUCKETS = 32
REL_MAX_DISTANCE = 128
DN_HEADS = 4
DN_HEAD = 128
DN_WIDTH = DN_HEADS * DN_HEAD
DN_CONV_WIDTH = 4
DN_CHUNK = 64
MAIN_COLS = 4 * ATTN_WIDTH + 3 * DN_WIDTH + DN_WIDTH
GATE_PAD = 128
EPS = 1e-6
LOG2E = math.log2(math.e)
NEG_BIG = -(2.0 ** 30)
VT_ROWS = ATTN_HEAD_DIM + 16
VMEM_LIMIT = 56 * 1024 * 1024


def _nt(a, b):
    return lax.dot_general(a, b, (((1,), (1,)), ((), ())), preferred_element_type=F32)


def _nn(a, b):
    return lax.dot_general(a, b, (((1,), (0,)), ((), ())), preferred_element_type=F32)


def _silu(x):
    half = 0.5 * x
    return half + half * jnp.tanh(half)


def _inproj_kernel(x_ref, nw_ref, w_ref, wg_ref, qnw_ref, knw_ref,
                   q_ref, k_ref, v_ref, za_ref, qkv_ref, zd_ref, g_ref):
    x = x_ref[...]
    ms = jnp.mean(x * x, axis=-1, keepdims=True)
    h = (x * lax.rsqrt(ms + EPS) * nw_ref[...]).astype(BF16)

    r_i = lax.broadcasted_iota(jnp.int32, (256, 256), 0)
    c_i = lax.broadcasted_iota(jnp.int32, (256, 256), 1)
    same_head = jnp.where(r_i // ATTN_HEAD_DIM == c_i // ATTN_HEAD_DIM, 1.0, 0.0).astype(BF16)

    def head_rms(t, w):
        t2 = (t * t).astype(BF16)
        ss = jnp.concatenate([_nn(t2[:, c0:c0 + 256], same_head) for c0 in range(0, ATTN_WIDTH, 256)], axis=1)
        return t * lax.rsqrt(ss * (1.0 / ATTN_HEAD_DIM) + EPS) * w

    def proj(c0, width):
        return _nn(h, w_ref[:, c0:c0 + width])

    q_ref[...] = head_rms(proj(0, ATTN_WIDTH), qnw_ref[...]).astype(q_ref.dtype)
    k_ref[...] = head_rms(proj(ATTN_WIDTH, ATTN_WIDTH), knw_ref[...]).astype(k_ref.dtype)
    v_ref[...] = proj(2 * ATTN_WIDTH, ATTN_WIDTH).astype(v_ref.dtype)
    za_ref[...] = proj(3 * ATTN_WIDTH, ATTN_WIDTH).astype(za_ref.dtype)
    for c in range(3):
        qkv_ref[:, c * DN_WIDTH:(c + 1) * DN_WIDTH] = proj(4 * ATTN_WIDTH + c * DN_WIDTH, DN_WIDTH).astype(qkv_ref.dtype)
    zd_ref[...] = proj(4 * ATTN_WIDTH + 3 * DN_WIDTH, DN_WIDTH).astype(zd_ref.dtype)
    g_ref[...] = _nn(h, wg_ref[...])


def _inproj(x2, norm_w, w_main, w_gate, qnw, knw, tm):
    n = x2.shape[0]
    row = lambda i: (i, 0)
    fixed = lambda i: (0, 0)
    outs = [
        jax.ShapeDtypeStruct((n, ATTN_WIDTH), F32),
        jax.ShapeDtypeStruct((n, ATTN_WIDTH), F32),
        jax.ShapeDtypeStruct((n, ATTN_WIDTH), BF16),
        jax.ShapeDtypeStruct((n, ATTN_WIDTH), F32),
        jax.ShapeDtypeStruct((n, 3 * DN_WIDTH), F32),
        jax.ShapeDtypeStruct((n, DN_WIDTH), F32),
        jax.ShapeDtypeStruct((n, GATE_PAD), F32),
    ]
    return pl.pallas_call(
        _inproj_kernel,
        grid=(n // tm,),
        in_specs=[
            pl.BlockSpec((tm, D_MODEL), row),
            pl.BlockSpec((1, D_MODEL), fixed),
            pl.BlockSpec((D_MODEL, MAIN_COLS), fixed),
            pl.BlockSpec((D_MODEL, GATE_PAD), fixed),
            pl.BlockSpec((1, ATTN_WIDTH), fixed),
            pl.BlockSpec((1, ATTN_WIDTH), fixed),
        ],
        out_specs=[pl.BlockSpec((tm, o.shape[1]), row) for o in outs],
        out_shape=outs,
        compiler_params=pltpu.CompilerParams(dimension_semantics=("arbitrary",), vmem_limit_bytes=VMEM_LIMIT),
        name="inproj",
    )(x2, norm_w, w_main, w_gate, qnw, knw)


def _t5_bias_tile_t(relb_ref, head, offset):
    c = lax.broadcasted_iota(jnp.int32, (MOBA_BLOCK, MOBA_BLOCK), 0)
    r = lax.broadcasted_iota(jnp.int32, (MOBA_BLOCK, MOBA_BLOCK), 1)
    dist = r - c + offset
    n = jnp.maximum(dist, 0)
    max_exact = REL_BUCKETS // 2
    nf = jnp.maximum(n, 1).astype(F32)
    large = max_exact + (jnp.log(nf / max_exact) / math.log(REL_MAX_DISTANCE / max_exact)
                         * (REL_BUCKETS - max_exact)).astype(jnp.int32)
    large = jnp.minimum(large, REL_BUCKETS - 1)
    bucket = jnp.where(n < max_exact, n, large)
    far = relb_ref[REL_BUCKETS - 1, head]
    bias = jnp.zeros((MOBA_BLOCK, MOBA_BLOCK), F32)
    for t in range(REL_BUCKETS):
        bias = jnp.where(bucket == t, relb_ref[t, head] - far, bias)
    return jnp.where(dist >= 0, bias * LOG2E, NEG_BIG)


def _moba_kernel(relb_ref, q_ref, qn_ref, k_ref, v_ref, z_ref, o_ref,
                 kaug_ref, vt_ref, kmean_ref, bias_ref, acc_ref, qa_ref, s_ref, *, nb):
    hp = pl.program_id(0)
    bi = pl.program_id(1)
    i = pl.program_id(2)
    par = i % 2
    HD = ATTN_HEAD_DIM
    lane = lax.broadcasted_iota(jnp.int32, (MOBA_BLOCK, 128), 1)
    lo_half = lane < HD
    er = lax.broadcasted_iota(jnp.int32, (128, 128), 0)
    ec = lax.broadcasted_iota(jnp.int32, (128, 128), 1)
    eye = jnp.where(er == ec, 1.0, 0.0).astype(BF16)
    blk = lax.broadcasted_iota(jnp.int32, (HD, MOBA_BLOCK), 0)

    def gate_scores(q):
        qt = _nt(eye, (q * (HD ** -0.5 * LOG2E)).astype(BF16))
        q0, q1 = _split2(q)
        gates = []
        for hh in range(2):
            k0, k1 = _split2(kmean_ref[hh])
            gates.append(_nt(k0, q0) + (_nt(k0, q1) + _nt(k1, q0)))
        return qt, gates

    def store_operands(slot, qt, gates, tile):
        for hh in range(2):
            g = jnp.where(blk < tile, gates[hh], -jnp.inf)
            sel = blk == tile
            for _ in range(MOBA_TOPK):
                mx = jnp.max(g, axis=0, keepdims=True)
                first = jnp.min(jnp.where(g == mx, blk, HD), axis=0, keepdims=True)
                hit = (blk == first) & (mx > -jnp.inf)
                sel = sel | hit
                g = jnp.where(hit, -jnp.inf, g)
            mask = jnp.where(sel, 0.0, NEG_BIG)
            rows = [qt[0:HD], mask] if hh == 0 else [mask, qt[HD:2 * HD]]
            qa_ref[slot, hh] = jnp.concatenate(rows, axis=0).astype(BF16)

    @pl.when(i == 0)
    def _prepare():
        kmean_ref[...] = jnp.zeros_like(kmean_ref)
        lane1 = lax.broadcasted_iota(jnp.int32, (1, 128), 1)
        ones_rows = jnp.ones((VT_ROWS - HD, MOBA_BLOCK), BF16)

        def prep(j, carry):
            rows = pl.ds(pl.multiple_of(j * MOBA_BLOCK, MOBA_BLOCK), MOBA_BLOCK)
            kb = k_ref[0, rows, :]
            km = jnp.mean(kb, axis=0, keepdims=True)
            kmean_ref[0, pl.ds(j, 1), :] = jnp.where(lane1 < HD, km, 0.0)
            kmean_ref[1, pl.ds(j, 1), :] = jnp.where(lane1 < HD, 0.0, km)
            kaug_ref[0, j] = jnp.where(lo_half, kb, jnp.where(lane == HD + j, 1.0, 0.0)).astype(BF16)
            kaug_ref[1, j] = jnp.where(lo_half, jnp.where(lane == j, 1.0, 0.0), kb).astype(BF16)
            vt = _nt(eye, v_ref[0, rows, :]).astype(BF16)
            for hh in range(2):
                vt_ref[j, hh] = jnp.concatenate([vt[hh * HD:(hh + 1) * HD], ones_rows], axis=0)
            return carry

        lax.fori_loop(0, nb, prep, 0, unroll=8)
        kaug_ref[0, nb] = jnp.where(lane == 2 * HD - 1, 1.0, 0.0).astype(BF16)
        kaug_ref[1, nb] = jnp.where(lane == HD - 1, 1.0, 0.0).astype(BF16)
        vt_ref[nb] = jnp.zeros((2, VT_ROWS, MOBA_BLOCK), BF16)
        store_operands(0, *gate_scores(q_ref[0]), 0)

    @pl.when((i == 0) & (bi == 0))
    def _bias_tiles():
        for hh in range(2):
            bias_ref[hh, 0] = _t5_bias_tile_t(relb_ref, 2 * hp + hh, 0)
            bias_ref[hh, 1] = _t5_bias_tile_t(relb_ref, 2 * hp + hh, MOBA_BLOCK)

    def issue(j, buf_ref):
        for hh in range(2):
            buf_ref[hh] = _nn(kaug_ref[hh, j], qa_ref[par, hh])

    def consume(j, buf_ref, st, bias_idx=None):
        out = []
        for hh in range(2):
            s = buf_ref[hh]
            if bias_idx is not None:
                s = s + bias_ref[hh, bias_idx]
            m_new = jnp.maximum(st[hh], jnp.max(s, axis=0, keepdims=True))
            p = jnp.exp2((s - m_new).astype(BF16))
            pv = _nn(vt_ref[j, hh], p)
            acc_ref[hh] = jnp.exp2(st[hh] - m_new) * acc_ref[hh] + pv
            out.append(m_new)
        return tuple(out)

    nfar = jnp.maximum(i - 1, 0)
    far = lambda j: jnp.where(j < nfar, j, nb)
    prev = jnp.where(i >= 1, i - 1, nb)
    a0_ref, b0_ref, a1_ref, b1_ref = (s_ref.at[n] for n in range(4))
    qt_next, gates_next = gate_scores(qn_ref[0])
    issue(i, a0_ref)
    issue(prev, b0_ref)
    issue(far(0), a1_ref)
    issue(far(1), b1_ref)
    store_operands(1 - par, qt_next, gates_next, i + 1)
    acc_ref[...] = jnp.zeros_like(acc_ref)
    st = (jnp.full((1, MOBA_BLOCK), 2 * NEG_BIG, F32),) * 2
    st = consume(i, a0_ref, st, bias_idx=0)
    issue(far(2), a0_ref)
    st = consume(prev, b0_ref, st, bias_idx=1)
    issue(far(3), b0_ref)
    st = consume(far(0), a1_ref, st)
    st = consume(far(1), b1_ref, st)

    def far_quad(j, st):
        issue(far(j + 2), a1_ref)
        issue(far(j + 3), b1_ref)
        st = consume(far(j), a0_ref, st)
        issue(far(j + 4), a0_ref)
        st = consume(far(j + 1), b0_ref, st)
        issue(far(j + 5), b0_ref)
        st = consume(far(j + 2), a1_ref, st)
        return consume(far(j + 3), b1_ref, st)

    nrest = jnp.maximum(nfar - 2, 0)
    n16 = nrest // 16

    def far_16(u, st):
        for q4 in range(4):
            st = far_quad(16 * u + 4 * q4 + 2, st)
        return st

    st = lax.fori_loop(0, n16, far_16, st)
    d16 = 16 * n16 + 2
    n8 = (nrest - 16 * n16) // 8
    st = lax.fori_loop(0, n8, lambda u, st: far_quad(8 * u + 4 + d16, far_quad(8 * u + d16, st)), st)
    n4 = (nrest - 16 * n16 - 8 * n8) // 4
    st = lax.fori_loop(0, n4, lambda u, st: far_quad(8 * n8 + 4 * u + d16, st), st)
    base = 8 * n8 + 4 * n4 + d16
    rem = nfar - base

    def rem_first(st):
        issue(far(base + 2), a1_ref)
        st = consume(far(base), a0_ref, st)
        return consume(far(base + 1), b0_ref, st)

    st = lax.cond(rem >= 1, rem_first, lambda st: st, st)
    lax.cond(rem >= 3, lambda st: consume(far(base + 2), a1_ref, st), lambda st: st, st)

    ot = jnp.concatenate([acc_ref[hh, 0:HD, :] / acc_ref[hh, HD:HD + 1, :] for hh in range(2)], axis=0)
    o_ref[0] = (ot.T * _silu(z_ref[0])).astype(o_ref.dtype)


def _moba(rel_bias, q, k, v, z):
    b, s, _ = q.shape
    nb = s // MOBA_BLOCK
    assert s % MOBA_BLOCK == 0 and nb < ATTN_HEAD_DIM
    pairs = ATTN_HEADS // 2
    blk = lambda hp, bi, i: (bi, i, hp)
    whole = lambda hp, bi, i: (bi, 0, hp)
    return pl.pallas_call(
        functools.partial(_moba_kernel, nb=nb),
        grid=(pairs, b, nb),
        in_specs=[
            pl.BlockSpec(memory_space=pltpu.SMEM),
            pl.BlockSpec((1, MOBA_BLOCK, 128), blk),
            pl.BlockSpec((1, MOBA_BLOCK, 128), lambda hp, bi, i: (bi, jnp.minimum(i + 1, nb - 1), hp)),
            pl.BlockSpec((1, s, 128), whole),
            pl.BlockSpec((1, s, 128), whole),
            pl.BlockSpec((1, MOBA_BLOCK, 128), blk),
        ],
        out_specs=pl.BlockSpec((1, MOBA_BLOCK, 128), blk),
        out_shape=jax.ShapeDtypeStruct((b, s, ATTN_WIDTH), BF16),
        scratch_shapes=[
            pltpu.VMEM((2, nb + 1, MOBA_BLOCK, 128), BF16),
            pltpu.VMEM((nb + 1, 2, VT_ROWS, MOBA_BLOCK), BF16),
            pltpu.VMEM((2, ATTN_HEAD_DIM, 128), F32),
            pltpu.VMEM((2, 2, MOBA_BLOCK, MOBA_BLOCK), F32),
            pltpu.VMEM((2, VT_ROWS, MOBA_BLOCK), F32),
            pltpu.VMEM((2, 2, 128, MOBA_BLOCK), BF16),
            pltpu.VMEM((4, 2, MOBA_BLOCK, MOBA_BLOCK), F32),
        ],
        compiler_params=pltpu.CompilerParams(
            dimension_semantics=("arbitrary", "arbitrary", "arbitrary"), vmem_limit_bytes=VMEM_LIMIT),
        name="moba",
    )(rel_bias, q, q, k, v, z)


DN_ROWS = DN_HEADS * DN_CHUNK


def _split2(a):
    a0 = a.astype(BF16)
    return a0, (a - a0.astype(F32)).astype(BF16)


def _split3(a):
    a0 = a.astype(BF16)
    r1 = a - a0.astype(F32)
    a1 = r1.astype(BF16)
    return a0, a1, (r1 - a1.astype(F32)).astype(BF16)


def _dot(a, b, dims=(((1,), (0,)), ((), ()))):
    return lax.dot_general(a.astype(BF16), b.astype(BF16), dims, preferred_element_type=F32)


_NT = (((1,), (1,)), ((), ()))
_TN = (((0,), (0,)), ((), ()))


def _dot_sel(sel, b):
    return sum(_nn(sel, p) for p in _split2(b))


def _deltanet_kernel(alog_ref, dtb_ref, x_ref, z_ref, g_ref, cw_ref, nw_ref, res_ref, ya_ref, wa_ref, wd_ref,
                     o_ref, state_ref, pad_ref, act_ref, rep_ref, yd_ref, *, tb):
    t = pl.program_id(1)
    C, R, W, H, D = DN_CHUNK, DN_ROWS, DN_WIDTH, DN_HEADS, DN_HEAD

    @pl.when(t == 0)
    def _reset():
        state_ref[...] = jnp.zeros_like(state_ref)
        pad_ref[0:8, :] = jnp.zeros((8, 3 * W), F32)

    pad_ref[8:8 + tb, :] = x_ref[0]
    for gcol in range(3 * H):
        cols = slice(gcol * D, (gcol + 1) * D)
        xp = pad_ref[:, cols]
        acc = xp * cw_ref[0:1, cols]
        for w in range(1, DN_CONV_WIDTH):
            acc = xp * cw_ref[w:w + 1, cols] + pltpu.roll(acc, 1, 0)
        a = _silu(acc[8:8 + tb])
        if gcol < 2 * H:
            a = a * lax.rsqrt(jnp.sum(a * a, axis=-1, keepdims=True) + EPS)
            if gcol < H:
                a = a * (D ** -0.5)
        act_ref[:, cols] = a
    pad_ref[0:8, :] = x_ref[0, tb - 8:tb, :]

    lane1 = lax.broadcasted_iota(jnp.int32, (1, GATE_PAD), 1)
    alog_row = jnp.zeros((1, GATE_PAD), F32)
    dtb_row = jnp.zeros((1, GATE_PAD), F32)
    for h in range(H):
        alog_row = jnp.where(lane1 == H + h, alog_ref[h], alog_row)
        dtb_row = jnp.where(lane1 == H + h, dtb_ref[h], dtb_row)
    gates = g_ref[0]
    xs = gates + dtb_row
    softplus = jnp.maximum(xs, 0.0) + jnp.log(1.0 + jnp.exp(-jnp.abs(xs)))
    gval = jnp.where(lane1 < H, 1.0 / (1.0 + jnp.exp(-gates)), -jnp.exp(alog_row) * softplus)
    tr = lax.broadcasted_iota(jnp.int32, (tb, tb), 0)
    tc = lax.broadcasted_iota(jnp.int32, (tb, tb), 1)
    gcum = _dot_sel(jnp.where((tr // C == tc // C) & (tr >= tc), 1.0, 0.0).astype(BF16), gval)
    er = lax.broadcasted_iota(jnp.int32, (GATE_PAD, 2 * H * D), 0)
    ec = lax.broadcasted_iota(jnp.int32, (GATE_PAD, 2 * H * D), 1)
    spread = jnp.where(er == ec // D, 1.0, 0.0).astype(BF16)
    rep_ref[...] = sum(_nn(p, spread) for p in _split2(jnp.where(lane1 < H, gval, gcum)))

    r = lax.broadcasted_iota(jnp.int32, (R, R), 0)
    c = lax.broadcasted_iota(jnp.int32, (R, R), 1)
    same = (r // C) == (c // C)
    incl = same & (r >= c)
    strict = same & (r > c)
    eye = jnp.where(r == c, 1.0, 0.0)
    lane_r = lax.broadcasted_iota(jnp.int32, (R, D), 1)

    chunks = range(tb // C)
    each = lambda f, *seqs: [f(*a) for a in zip(*seqs)]
    rows = [slice(ci * C, (ci + 1) * C) for ci in chunks]
    stack = lambda ref, c0: [jnp.concatenate([ref[rw, c0 + D * h:c0 + D * (h + 1)] for h in range(H)], axis=0)
                             for rw in rows]
    qc, kc, vc = stack(act_ref, 0), stack(act_ref, W), stack(act_ref, 2 * W)
    beta = stack(rep_ref, 0)
    gc = stack(rep_ref, H * D)

    pick = jnp.where((lax.broadcasted_iota(jnp.int32, (16, D), 0) == 0)
                     & (lax.broadcasted_iota(jnp.int32, (16, D), 1) < 3), 1.0, 0.0).astype(BF16)

    def pair_diff(g):
        p0, p1, p2 = (p.astype(F32) for p in _split3(g))
        parts = jnp.where(lane_r == 0, p0, jnp.where(lane_r == 1, p1, jnp.where(lane_r == 2, p2, 0.0)))
        g_row = _nt(pick, parts.astype(BF16))[0:1, :]
        return jnp.concatenate([g, g], axis=1) - g_row

    decay = each(lambda g: jnp.where(incl, jnp.exp(jnp.where(incl, pair_diff(g), 0.0)), 0.0), gc)
    kb = each(lambda k, b: k * b, kc, beta)
    x = each(lambda a, k, d: -jnp.where(strict, _dot(a, k, _NT) * d, 0.0), kb, kc, decay)
    sq = lambda m: each(lambda a: _dot(a, a), m)
    mul = lambda ma, mb: each(lambda a, b: _dot(a, b), ma, mb)
    add = lambda ma, mb: each(lambda a, b: a + b, ma, mb)
    plus_eye = lambda m: each(lambda a: eye + a, m)
    x2 = sq(x)
    x4 = sq(x2)
    f1 = plus_eye(x)
    m1 = add(f1, mul(f1, x2))
    x8 = sq(x4)
    f4 = plus_eye(x4)
    m2 = add(f4, mul(f4, x8))
    x16 = sq(x8)
    m12 = mul(m1, m2)
    x32 = sq(x16)
    f16 = plus_eye(x16)
    m3 = add(f16, mul(f16, x32))
    tinv = mul(m12, m3)
    eg = each(jnp.exp, gc)
    uw = each(lambda t, v, b, k, e: _dot(t, jnp.concatenate([v * b, k * e], axis=1)),
              tinv, vc, beta, kb, eg)
    a_intra = each(lambda q, k, d: jnp.where(incl, _dot(q, k, _NT) * d, 0.0), qc, kc, decay)
    glast = [[g[h * C + C - 1:h * C + C, :] for h in range(H)] for g in gc]
    q_dec = each(lambda q, e: q * e, qc, eg)
    k_dec = each(lambda k, g, gl: k * jnp.exp(
        jnp.concatenate([jnp.broadcast_to(t, (C, D)) for t in gl], axis=0) - g), kc, gc, glast)

    hs = [slice(h * C, (h + 1) * C) for h in range(H)]
    for ci in chunks:
        state = [state_ref[h] for h in range(H)]
        wq = [_dot(jnp.concatenate([uw[ci][hs[h], D:2 * D], q_dec[ci][hs[h]]], axis=0), state[h])
              for h in range(H)]
        vn = [uw[ci][hs[h], 0:D] - wq[h][0:C] for h in range(H)]
        for h in range(H):
            state_ref[h] = state[h] * jnp.exp(glast[ci][h]) + _dot(k_dec[ci][hs[h]], vn[h], _TN)
        o = jnp.concatenate([wq[h][C:2 * C] for h in range(H)], axis=0) + _dot(
            a_intra[ci], jnp.concatenate(vn, axis=0))
        on = o * lax.rsqrt(jnp.mean(o * o, axis=-1, keepdims=True) + EPS) * nw_ref[...]
        for h in range(H):
            cols = slice(h * D, (h + 1) * D)
            yd_ref[rows[ci], cols] = (on[hs[h]] * _silu(z_ref[0, rows[ci], cols])).astype(yd_ref.dtype)

    o_ref[0] = res_ref[0] + _nn(ya_ref[0], wa_ref[...]) + _nn(yd_ref[...], wd_ref[...])


def _deltanet(a_log, dt_bias, qkv, z, gates, conv_w, dn_norm_w, x, ya, wa, wd, tb):
    b, s, _ = qkv.shape
    assert s % tb == 0 and tb % DN_CHUNK == 0
    blk = lambda bi, t: (bi, t, 0)
    fixed = lambda bi, t: (0, 0)
    return pl.pallas_call(
        functools.partial(_deltanet_kernel, tb=tb),
        grid=(b, s // tb),
        in_specs=[
            pl.BlockSpec(memory_space=pltpu.SMEM),
            pl.BlockSpec(memory_space=pltpu.SMEM),
            pl.BlockSpec((1, tb, 3 * DN_WIDTH), blk),
            pl.BlockSpec((1, tb, DN_WIDTH), blk),
            pl.BlockSpec((1, tb, GATE_PAD), blk),
            pl.BlockSpec((DN_CONV_WIDTH, 3 * DN_WIDTH), fixed),
            pl.BlockSpec((1, DN_HEAD), fixed),
            pl.BlockSpec((1, tb, D_MODEL), blk),
            pl.BlockSpec((1, tb, ATTN_WIDTH), blk),
            pl.BlockSpec((ATTN_WIDTH, D_MODEL), fixed),
            pl.BlockSpec((DN_WIDTH, D_MODEL), fixed),
        ],
        out_specs=pl.BlockSpec((1, tb, D_MODEL), blk),
        out_shape=jax.ShapeDtypeStruct((b, s, D_MODEL), F32),
        scratch_shapes=[
            pltpu.VMEM((DN_HEADS, DN_HEAD, DN_HEAD), F32),
            pltpu.VMEM((tb + 8, 3 * DN_WIDTH), F32),
            pltpu.VMEM((tb, 3 * DN_WIDTH), F32),
            pltpu.VMEM((tb, 2 * DN_HEADS * DN_HEAD), F32),
            pltpu.VMEM((tb, DN_WIDTH), BF16),
        ],
        compiler_params=pltpu.CompilerParams(
            dimension_semantics=("arbitrary", "arbitrary"), vmem_limit_bytes=VMEM_LIMIT),
        name="deltanet",
    )(a_log, dt_bias, qkv, z, gates, conv_w, dn_norm_w, x, ya, wa, wd)


def _row_tile(n):
    for tm in (512, 256, 128, 64, 32, 16, 8):
        if n % tm == 0:
            return tm
    raise ValueError(f"row count {n} is not a multiple of 8")


def kernel(x, rel_bias, norm_w, w_in, q_norm_w, k_norm_w, conv_w, a_log, dt_bias, dn_norm_w, w_out):
    b, s, d = x.shape
    assert d == D_MODEL and norm_w.shape[0] == 1, "single-layer kernel"
    n = b * s
    x2 = x.reshape(n, d)
    tm = _row_tile(n)

    w_main = w_in[0][:, :MAIN_COLS].astype(BF16)
    w_gate = jnp.pad(w_in[0][:, MAIN_COLS:], ((0, 0), (0, GATE_PAD - 2 * DN_HEADS))).astype(BF16)
    qnw = jnp.tile(q_norm_w[0], ATTN_HEADS)[None, :]
    knw = jnp.tile(k_norm_w[0], ATTN_HEADS)[None, :]

    q, k, v, za, qkv, zd, gates = _inproj(x2, norm_w, w_main, w_gate, qnw, knw, tm)

    r3 = lambda t: t.reshape(b, s, t.shape[-1])
    ya = _moba(rel_bias, r3(q), r3(k), r3(v), r3(za))
    tb = 256 if s % 256 == 0 else DN_CHUNK
    w_o = w_out[0].astype(BF16)
    return _deltanet(a_log[0], dt_bias[0], r3(qkv), r3(zd), r3(gates), conv_w[0], dn_norm_w,
                     x, ya, w_o[:ATTN_WIDTH], w_o[ATTN_WIDTH:], tb)
```

```python
import functools
import math

import jax
import jax.numpy as jnp
from jax import lax
from jax.experimental import pallas as pl
from jax.experimental.pallas import tpu as pltpu

F32 = jnp.float32
BF16 = jnp.bfloat16

D_MODEL = 1024
ATTN_HEADS = 8
ATTN_HEAD_DIM = 64
ATTN_WIDTH = ATTN_HEADS * ATTN_HEAD_DIM
MOBA_BLOCK = 256
MOBA_TOPK = 3
REL_BUCKETS = 32
REL_MAX_DISTANCE = 128
DN_HEADS = 4
DN_HEAD = 128
DN_WIDTH = DN_HEADS * DN_HEAD
DN_CONV_WIDTH = 4
DN_CHUNK = 64
MAIN_COLS = 4 * ATTN_WIDTH + 3 * DN_WIDTH + DN_WIDTH
GATE_PAD = 128
EPS = 1e-6
LOG2E = math.log2(math.e)
NEG_BIG = -(2.0 ** 30)
VT_ROWS = ATTN_HEAD_DIM + 16
VMEM_LIMIT = 56 * 1024 * 1024


def _nt(a, b):
    return lax.dot_general(a, b, (((1,), (1,)), ((), ())), preferred_element_type=F32)


def _nn(a, b):
    return lax.dot_general(a, b, (((1,), (0,)), ((), ())), preferred_element_type=F32)


def _silu(x):
    half = 0.5 * x
    return half + half * jnp.tanh(half)


def _inproj_kernel(x_ref, nw_ref, w_ref, wg_ref, qnw_ref, knw_ref,
                   q_ref, k_ref, v_ref, za_ref, qkv_ref, zd_ref, g_ref):
    x = x_ref[...]
    ms = jnp.mean(x * x, axis=-1, keepdims=True)
    h = (x * lax.rsqrt(ms + EPS) * nw_ref[...]).astype(BF16)

    r_i = lax.broadcasted_iota(jnp.int32, (256, 256), 0)
    c_i = lax.broadcasted_iota(jnp.int32, (256, 256), 1)
    same_head = jnp.where(r_i // ATTN_HEAD_DIM == c_i // ATTN_HEAD_DIM, 1.0, 0.0).astype(BF16)

    def head_rms(t, w):
        t2 = (t * t).astype(BF16)
        ss = jnp.concatenate([_nn(t2[:, c0:c0 + 256], same_head) for c0 in range(0, ATTN_WIDTH, 256)], axis=1)
        return t * lax.rsqrt(ss * (1.0 / ATTN_HEAD_DIM) + EPS) * w

    def proj(c0, width):
        return _nn(h, w_ref[:, c0:c0 + width])

    q_ref[...] = head_rms(proj(0, ATTN_WIDTH), qnw_ref[...]).astype(q_ref.dtype)
    k_ref[...] = head_rms(proj(ATTN_WIDTH, ATTN_WIDTH), knw_ref[...]).astype(k_ref.dtype)
    v_ref[...] = proj(2 * ATTN_WIDTH, ATTN_WIDTH).astype(v_ref.dtype)
    za_ref[...] = proj(3 * ATTN_WIDTH, ATTN_WIDTH).astype(za_ref.dtype)
    for c in range(3):
        qkv_ref[:, c * DN_WIDTH:(c + 1) * DN_WIDTH] = proj(4 * ATTN_WIDTH + c * DN_WIDTH, DN_WIDTH).astype(qkv_ref.dtype)
    zd_ref[...] = proj(4 * ATTN_WIDTH + 3 * DN_WIDTH, DN_WIDTH).astype(zd_ref.dtype)
    g_ref[...] = _nn(h, wg_ref[...])


def _inproj(x2, norm_w, w_main, w_gate, qnw, knw, tm):
    n = x2.shape[0]
    row = lambda i: (i, 0)
    fixed = lambda i: (0, 0)
    outs = [
        jax.ShapeDtypeStruct((n, ATTN_WIDTH), F32),
        jax.ShapeDtypeStruct((n, ATTN_WIDTH), F32),
        jax.ShapeDtypeStruct((n, ATTN_WIDTH), BF16),
        jax.ShapeDtypeStruct((n, ATTN_WIDTH), F32),
        jax.ShapeDtypeStruct((n, 3 * DN_WIDTH), F32),
        jax.ShapeDtypeStruct((n, DN_WIDTH), F32),
        jax.ShapeDtypeStruct((n, GATE_PAD), F32),
    ]
    return pl.pallas_call(
        _inproj_kernel,
        grid=(n // tm,),
        in_specs=[
            pl.BlockSpec((tm, D_MODEL), row),
            pl.BlockSpec((1, D_MODEL), fixed, pipeline_mode=pl.Buffered(1)),
            pl.BlockSpec((D_MODEL, MAIN_COLS), fixed, pipeline_mode=pl.Buffered(1)),
            pl.BlockSpec((D_MODEL, GATE_PAD), fixed, pipeline_mode=pl.Buffered(1)),
            pl.BlockSpec((1, ATTN_WIDTH), fixed, pipeline_mode=pl.Buffered(1)),
            pl.BlockSpec((1, ATTN_WIDTH), fixed, pipeline_mode=pl.Buffered(1)),
        ],
        out_specs=[pl.BlockSpec((tm, o.shape[1]), row) for o in outs],
        out_shape=outs,
        compiler_params=pltpu.CompilerParams(dimension_semantics=("arbitrary",), vmem_limit_bytes=VMEM_LIMIT),
        name="inproj",
    )(x2, norm_w, w_main, w_gate, qnw, knw)


def _t5_bias_tile_t(relb_ref, head, offset):
    c = lax.broadcasted_iota(jnp.int32, (MOBA_BLOCK, MOBA_BLOCK), 0)
    r = lax.broadcasted_iota(jnp.int32, (MOBA_BLOCK, MOBA_BLOCK), 1)
    dist = r - c + offset
    n = jnp.maximum(dist, 0)
    max_exact = REL_BUCKETS // 2
    nf = jnp.maximum(n, 1).astype(F32)
    large = max_exact + (jnp.log(nf / max_exact) / math.log(REL_MAX_DISTANCE / max_exact)
                         * (REL_BUCKETS - max_exact)).astype(jnp.int32)
    large = jnp.minimum(large, REL_BUCKETS - 1)
    bucket = jnp.where(n < max_exact, n, large)
    far = relb_ref[REL_BUCKETS - 1, head]
    bias = jnp.zeros((MOBA_BLOCK, MOBA_BLOCK), F32)
    for t in range(REL_BUCKETS):
        bias = jnp.where(bucket == t, relb_ref[t, head] - far, bias)
    return jnp.where(dist >= 0, bias * LOG2E, NEG_BIG)


def _moba_kernel(relb_ref, q_ref, qn_ref, k_ref, v_ref, z_ref, o_ref,
                 kaug_ref, vt_ref, kmean_ref, bias_ref, acc_ref, qa_ref, s_ref, *, nb):
    hp = pl.program_id(0)
    bi = pl.program_id(1)
    i = pl.program_id(2)
    par = i % 2
    HD = ATTN_HEAD_DIM
    lane = lax.broadcasted_iota(jnp.int32, (MOBA_BLOCK, 128), 1)
    lo_half = lane < HD
    er = lax.broadcasted_iota(jnp.int32, (128, 128), 0)
    ec = lax.broadcasted_iota(jnp.int32, (128, 128), 1)
    eye = jnp.where(er == ec, 1.0, 0.0).astype(BF16)
    blk = lax.broadcasted_iota(jnp.int32, (HD, MOBA_BLOCK), 0)

    def gate_scores(q):
        qt = _nt(eye, (q * (HD ** -0.5 * LOG2E)).astype(BF16))
        q0, q1 = _split2(q)
        gates = []
        for hh in range(2):
            k0, k1 = _split2(kmean_ref[hh])
            gates.append(_nt(k0, q0) + (_nt(k0, q1) + _nt(k1, q0)))
        return qt, gates

    def store_operands(slot, qt, gates, tile):
        for hh in range(2):
            g = jnp.where(blk < tile, gates[hh], -jnp.inf)
            sel = blk == tile
            for _ in range(MOBA_TOPK):
                mx = jnp.max(g, axis=0, keepdims=True)
                first = jnp.min(jnp.where(g == mx, blk, HD), axis=0, keepdims=True)
                hit = (blk == first) & (mx > -jnp.inf)
                sel = sel | hit
                g = jnp.where(hit, -jnp.inf, g)
            mask = jnp.where(sel, 0.0, NEG_BIG)
            rows = [qt[0:HD], mask] if hh == 0 else [mask, qt[HD:2 * HD]]
            qa_ref[slot, hh] = jnp.concatenate(rows, axis=0).astype(BF16)

    @pl.when(i == 0)
    def _prepare():
        kmean_ref[...] = jnp.zeros_like(kmean_ref)
        lane1 = lax.broadcasted_iota(jnp.int32, (1, 128), 1)
        ones_rows = jnp.ones((VT_ROWS - HD, MOBA_BLOCK), BF16)

        def prep(j, carry):
            rows = pl.ds(pl.multiple_of(j * MOBA_BLOCK, MOBA_BLOCK), MOBA_BLOCK)
            kb = k_ref[0, rows, :]
            km = jnp.mean(kb, axis=0, keepdims=True)
            kmean_ref[0, pl.ds(j, 1), :] = jnp.where(lane1 < HD, km, 0.0)
            kmean_ref[1, pl.ds(j, 1), :] = jnp.where(lane1 < HD, 0.0, km)
            kaug_ref[0, j] = jnp.where(lo_half, kb, jnp.where(lane == HD + j, 1.0, 0.0)).astype(BF16)
            kaug_ref[1, j] = jnp.where(lo_half, jnp.where(lane == j, 1.0, 0.0), kb).astype(BF16)
            vt = _nt(eye, v_ref[0, rows, :]).astype(BF16)
            for hh in range(2):
                vt_ref[j, hh] = jnp.concatenate([vt[hh * HD:(hh + 1) * HD], ones_rows], axis=0)
            return carry

        lax.fori_loop(0, nb, prep, 0, unroll=8)
        kaug_ref[0, nb] = jnp.where(lane == 2 * HD - 1, 1.0, 0.0).astype(BF16)
        kaug_ref[1, nb] = jnp.where(lane == HD - 1, 1.0, 0.0).astype(BF16)
        vt_ref[nb] = jnp.zeros((2, VT_ROWS, MOBA_BLOCK), BF16)
        store_operands(0, *gate_scores(q_ref[0]), 0)

    @pl.when((i == 0) & (bi == 0))
    def _bias_tiles():
        for hh in range(2):
            bias_ref[hh, 0] = _t5_bias_tile_t(relb_ref, 2 * hp + hh, 0)
            bias_ref[hh, 1] = _t5_bias_tile_t(relb_ref, 2 * hp + hh, MOBA_BLOCK)

    def issue(j, buf_ref):
        for hh in range(2):
            buf_ref[hh] = _nn(kaug_ref[hh, j], qa_ref[par, hh])

    def consume(j, buf_ref, st, bias_idx=None):
        out = []
        for hh in range(2):
            s = buf_ref[hh]
            if bias_idx is not None:
                s = s + bias_ref[hh, bias_idx]
            m_new = jnp.maximum(st[hh], jnp.max(s, axis=0, keepdims=True))
            p = jnp.exp2((s - m_new).astype(BF16))
            pv = _nn(vt_ref[j, hh], p)
            acc_ref[hh] = jnp.exp2(st[hh] - m_new) * acc_ref[hh] + pv
            out.append(m_new)
        return tuple(out)

    nfar = jnp.maximum(i - 1, 0)
    far = lambda j: jnp.where(j < nfar, j, nb)
    prev = jnp.where(i >= 1, i - 1, nb)
    a0_ref, b0_ref, a1_ref, b1_ref = (s_ref.at[n] for n in range(4))
    qt_next, gates_next = gate_scores(qn_ref[0])
    issue(i, a0_ref)
    issue(prev, b0_ref)
    issue(far(0), a1_ref)
    issue(far(1), b1_ref)
    store_operands(1 - par, qt_next, gates_next, i + 1)
    acc_ref[...] = jnp.zeros_like(acc_ref)
    st = (jnp.full((1, MOBA_BLOCK), 2 * NEG_BIG, F32),) * 2
    st = consume(i, a0_ref, st, bias_idx=0)
    issue(far(2), a0_ref)
    st = consume(prev, b0_ref, st, bias_idx=1)
    issue(far(3), b0_ref)
    st = consume(far(0), a1_ref, st)
    st = consume(far(1), b1_ref, st)

    def far_quad(j, st):
        issue(far(j + 2), a1_ref)
        issue(far(j + 3), b1_ref)
        st = consume(far(j), a0_ref, st)
        issue(far(j + 4), a0_ref)
        st = consume(far(j + 1), b0_ref, st)
        issue(far(j + 5), b0_ref)
        st = consume(far(j + 2), a1_ref, st)
        return consume(far(j + 3), b1_ref, st)

    nrest = jnp.maximum(nfar - 2, 0)
    n16 = nrest // 16

    def far_16(u, st):
        for q4 in range(4):
            st = far_quad(16 * u + 4 * q4 + 2, st)
        return st

    st = lax.fori_loop(0, n16, far_16, st)
    d16 = 16 * n16 + 2
    n8 = (nrest - 16 * n16) // 8
    st = lax.fori_loop(0, n8, lambda u, st: far_quad(8 * u + 4 + d16, far_quad(8 * u + d16, st)), st)
    n4 = (nrest - 16 * n16 - 8 * n8) // 4
    st = lax.fori_loop(0, n4, lambda u, st: far_quad(8 * n8 + 4 * u + d16, st), st)
    base = 8 * n8 + 4 * n4 + d16
    rem = nfar - base

    def rem_first(st):
        issue(far(base + 2), a1_ref)
        st = consume(far(base), a0_ref, st)
        return consume(far(base + 1), b0_ref, st)

    st = lax.cond(rem >= 1, rem_first, lambda st: st, st)
    lax.cond(rem >= 3, lambda st: consume(far(base + 2), a1_ref, st), lambda st: st, st)

    ot = jnp.concatenate([acc_ref[hh, 0:HD, :] / acc_ref[hh, HD:HD + 1, :] for hh in range(2)], axis=0)
    o_ref[0] = (ot.T * _silu(z_ref[0])).astype(o_ref.dtype)


def _moba(rel_bias, q, k, v, z):
    b, s, _ = q.shape
    nb = s // MOBA_BLOCK
    assert s % MOBA_BLOCK == 0 and nb < ATTN_HEAD_DIM
    pairs = ATTN_HEADS // 2
    blk = lambda hp, bi, i: (bi, i, hp)
    whole = lambda hp, bi, i: (bi, 0, hp)
    return pl.pallas_call(
        functools.partial(_moba_kernel, nb=nb),
        grid=(pairs, b, nb),
        in_specs=[
            pl.BlockSpec(memory_space=pltpu.SMEM),
            pl.BlockSpec((1, MOBA_BLOCK, 128), blk),
            pl.BlockSpec((1, MOBA_BLOCK, 128), lambda hp, bi, i: (bi, jnp.minimum(i + 1, nb - 1), hp)),
            pl.BlockSpec((1, s, 128), whole),
            pl.BlockSpec((1, s, 128), whole),
            pl.BlockSpec((1, MOBA_BLOCK, 128), blk),
        ],
        out_specs=pl.BlockSpec((1, MOBA_BLOCK, 128), blk),
        out_shape=jax.ShapeDtypeStruct((b, s, ATTN_WIDTH), BF16),
        scratch_shapes=[
            pltpu.VMEM((2, nb + 1, MOBA_BLOCK, 128), BF16),
            pltpu.VMEM((nb + 1, 2, VT_ROWS, MOBA_BLOCK), BF16),
            pltpu.VMEM((2, ATTN_HEAD_DIM, 128), F32),
            pltpu.VMEM((2, 2, MOBA_BLOCK, MOBA_BLOCK), F32),
            pltpu.VMEM((2, VT_ROWS, MOBA_BLOCK), F32),
            pltpu.VMEM((2, 2, 128, MOBA_BLOCK), BF16),
            pltpu.VMEM((4, 2, MOBA_BLOCK, MOBA_BLOCK), F32),
        ],
        compiler_params=pltpu.CompilerParams(
            dimension_semantics=("arbitrary", "arbitrary", "arbitrary"), vmem_limit_bytes=VMEM_LIMIT),
        name="moba",
    )(rel_bias, q, q, k, v, z)


DN_ROWS = DN_HEADS * DN_CHUNK


def _split2(a):
    a0 = a.astype(BF16)
    return a0, (a - a0.astype(F32)).astype(BF16)


def _split3(a):
    a0 = a.astype(BF16)
    r1 = a - a0.astype(F32)
    a1 = r1.astype(BF16)
    return a0, a1, (r1 - a1.astype(F32)).astype(BF16)


def _dot(a, b, dims=(((1,), (0,)), ((), ()))):
    return lax.dot_general(a.astype(BF16), b.astype(BF16), dims, preferred_element_type=F32)


_NT = (((1,), (1,)), ((), ()))
_TN = (((0,), (0,)), ((), ()))


def _dot_sel(sel, b):
    return sum(_nn(sel, p) for p in _split2(b))


def _deltanet_kernel(alog_ref, dtb_ref, x_ref, z_ref, g_ref, cw_ref, nw_ref, res_ref, ya_ref, wa_ref, wd_ref,
                     o_ref, state_ref, pad_ref, act_ref, rep_ref, yd_ref, *, tb):
    t = pl.program_id(1)
    C, R, W, H, D = DN_CHUNK, DN_ROWS, DN_WIDTH, DN_HEADS, DN_HEAD

    @pl.when(t == 0)
    def _reset():
        state_ref[...] = jnp.zeros_like(state_ref)
        pad_ref[0:8, :] = jnp.zeros((8, 3 * W), F32)

    pad_ref[8:8 + tb, :] = x_ref[0]
    for gcol in range(3 * H):
        cols = slice(gcol * D, (gcol + 1) * D)
        xp = pad_ref[:, cols]
        acc = xp * cw_ref[0:1, cols]
        for w in range(1, DN_CONV_WIDTH):
            acc = xp * cw_ref[w:w + 1, cols] + pltpu.roll(acc, 1, 0)
        a = _silu(acc[8:8 + tb])
        if gcol < 2 * H:
            a = a * lax.rsqrt(jnp.sum(a * a, axis=-1, keepdims=True) + EPS)
            if gcol < H:
                a = a * (D ** -0.5)
        act_ref[:, cols] = a
    pad_ref[0:8, :] = x_ref[0, tb - 8:tb, :]

    lane1 = lax.broadcasted_iota(jnp.int32, (1, GATE_PAD), 1)
    alog_row = jnp.zeros((1, GATE_PAD), F32)
    dtb_row = jnp.zeros((1, GATE_PAD), F32)
    for h in range(H):
        alog_row = jnp.where(lane1 == H + h, alog_ref[h], alog_row)
        dtb_row = jnp.where(lane1 == H + h, dtb_ref[h], dtb_row)
    gates = g_ref[0]
    xs = gates + dtb_row
    softplus = jnp.maximum(xs, 0.0) + jnp.log(1.0 + jnp.exp(-jnp.abs(xs)))
    gval = jnp.where(lane1 < H, 1.0 / (1.0 + jnp.exp(-gates)), -jnp.exp(alog_row) * softplus)
    tr = lax.broadcasted_iota(jnp.int32, (tb, tb), 0)
    tc = lax.broadcasted_iota(jnp.int32, (tb, tb), 1)
    gcum = _dot_sel(jnp.where((tr // C == tc // C) & (tr >= tc), 1.0, 0.0).astype(BF16), gval)
    er = lax.broadcasted_iota(jnp.int32, (GATE_PAD, 2 * H * D), 0)
    ec = lax.broadcasted_iota(jnp.int32, (GATE_PAD, 2 * H * D), 1)
    spread = jnp.where(er == ec // D, 1.0, 0.0).astype(BF16)
    rep_ref[...] = sum(_nn(p, spread) for p in _split2(jnp.where(lane1 < H, gval, gcum)))

    r = lax.broadcasted_iota(jnp.int32, (R, R), 0)
    c = lax.broadcasted_iota(jnp.int32, (R, R), 1)
    same = (r // C) == (c // C)
    incl = same & (r >= c)
    strict = same & (r > c)
    eye = jnp.where(r == c, 1.0, 0.0)
    lane_r = lax.broadcasted_iota(jnp.int32, (R, D), 1)

    chunks = range(tb // C)
    each = lambda f, *seqs: [f(*a) for a in zip(*seqs)]
    rows = [slice(ci * C, (ci + 1) * C) for ci in chunks]
    stack = lambda ref, c0: [jnp.concatenate([ref[rw, c0 + D * h:c0 + D * (h + 1)] for h in range(H)], axis=0)
                             for rw in rows]
    qc, kc, vc = stack(act_ref, 0), stack(act_ref, W), stack(act_ref, 2 * W)
    beta = stack(rep_ref, 0)
    gc = stack(rep_ref, H * D)

    pick = jnp.where((lax.broadcasted_iota(jnp.int32, (16, D), 0) == 0)
                     & (lax.broadcasted_iota(jnp.int32, (16, D), 1) < 3), 1.0, 0.0).astype(BF16)

    def pair_diff(g):
        p0, p1, p2 = (p.astype(F32) for p in _split3(g))
        parts = jnp.where(lane_r == 0, p0, jnp.where(lane_r == 1, p1, jnp.where(lane_r == 2, p2, 0.0)))
        g_row = _nt(pick, parts.astype(BF16))[0:1, :]
        return jnp.concatenate([g, g], axis=1) - g_row

    decay = each(lambda g: jnp.where(incl, jnp.exp(jnp.where(incl, pair_diff(g), 0.0)), 0.0), gc)
    kb = each(lambda k, b: k * b, kc, beta)
    x = each(lambda a, k, d: -jnp.where(strict, _dot(a, k, _NT) * d, 0.0), kb, kc, decay)
    sq = lambda m: each(lambda a: _dot(a, a), m)
    mul = lambda ma, mb: each(lambda a, b: _dot(a, b), ma, mb)
    add = lambda ma, mb: each(lambda a, b: a + b, ma, mb)
    plus_eye = lambda m: each(lambda a: eye + a, m)
    x2 = sq(x)
    x4 = sq(x2)
    f1 = plus_eye(x)
    m1 = add(f1, mul(f1, x2))
    x8 = sq(x4)
    f4 = plus_eye(x4)
    m2 = add(f4, mul(f4, x8))
    x16 = sq(x8)
    m12 = mul(m1, m2)
    x32 = sq(x16)
    f16 = plus_eye(x16)
    m3 = add(f16, mul(f16, x32))
    tinv = mul(m12, m3)
    eg = each(jnp.exp, gc)
    uw = each(lambda t, v, b, k, e: _dot(t, jnp.concatenate([v * b, k * e], axis=1)),
              tinv, vc, beta, kb, eg)
    a_intra = each(lambda q, k, d: jnp.where(incl, _dot(q, k, _NT) * d, 0.0), qc, kc, decay)
    glast = [[g[h * C + C - 1:h * C + C, :] for h in range(H)] for g in gc]
    q_dec = each(lambda q, e: q * e, qc, eg)
    k_dec = each(lambda k, g, gl: k * jnp.exp(
        jnp.concatenate([jnp.broadcast_to(t, (C, D)) for t in gl], axis=0) - g), kc, gc, glast)

    hs = [slice(h * C, (h + 1) * C) for h in range(H)]
    for ci in chunks:
        state = [state_ref[h] for h in range(H)]
        wq = [_dot(jnp.concatenate([uw[ci][hs[h], D:2 * D], q_dec[ci][hs[h]]], axis=0), state[h])
              for h in range(H)]
        vn = [uw[ci][hs[h], 0:D] - wq[h][0:C] for h in range(H)]
        for h in range(H):
            state_ref[h] = state[h] * jnp.exp(glast[ci][h]) + _dot(k_dec[ci][hs[h]], vn[h], _TN)
        o = jnp.concatenate([wq[h][C:2 * C] for h in range(H)], axis=0) + _dot(
            a_intra[ci], jnp.concatenate(vn, axis=0))
        on = o * lax.rsqrt(jnp.mean(o * o, axis=-1, keepdims=True) + EPS) * nw_ref[...]
        for h in range(H):
            cols = slice(h * D, (h + 1) * D)
            yd_ref[rows[ci], cols] = (on[hs[h]] * _silu(z_ref[0, rows[ci], cols])).astype(yd_ref.dtype)

    o_ref[0] = res_ref[0] + _nn(ya_ref[0], wa_ref[...]) + _nn(yd_ref[...], wd_ref[...])


def _deltanet(a_log, dt_bias, qkv, z, gates, conv_w, dn_norm_w, x, ya, wa, wd, tb):
    b, s, _ = qkv.shape
    assert s % tb == 0 and tb % DN_CHUNK == 0
    blk = lambda bi, t: (bi, t, 0)
    fixed = lambda bi, t: (0, 0)
    return pl.pallas_call(
        functools.partial(_deltanet_kernel, tb=tb),
        grid=(b, s // tb),
        in_specs=[
            pl.BlockSpec(memory_space=pltpu.SMEM),
            pl.BlockSpec(memory_space=pltpu.SMEM),
            pl.BlockSpec((1, tb, 3 * DN_WIDTH), blk),
            pl.BlockSpec((1, tb, DN_WIDTH), blk),
            pl.BlockSpec((1, tb, GATE_PAD), blk),
            pl.BlockSpec((DN_CONV_WIDTH, 3 * DN_WIDTH), fixed),
            pl.BlockSpec((1, DN_HEAD), fixed),
            pl.BlockSpec((1, tb, D_MODEL), blk),
            pl.BlockSpec((1, tb, ATTN_WIDTH), blk),
            pl.BlockSpec((ATTN_WIDTH, D_MODEL), fixed),
            pl.BlockSpec((DN_WIDTH, D_MODEL), fixed),
        ],
        out_specs=pl.BlockSpec((1, tb, D_MODEL), blk),
        out_shape=jax.ShapeDtypeStruct((b, s, D_MODEL), F32),
        scratch_shapes=[
            pltpu.VMEM((DN_HEADS, DN_HEAD, DN_HEAD), F32),
            pltpu.VMEM((tb + 8, 3 * DN_WIDTH), F32),
            pltpu.VMEM((tb, 3 * DN_WIDTH), F32),
            pltpu.VMEM((tb, 2 * DN_HEADS * DN_HEAD), F32),
            pltpu.VMEM((tb, DN_WIDTH), BF16),
        ],
        compiler_params=pltpu.CompilerParams(
            dimension_semantics=("arbitrary", "arbitrary"), vmem_limit_bytes=VMEM_LIMIT),
        name="deltanet",
    )(a_log, dt_bias, qkv, z, gates, conv_w, dn_norm_w, x, ya, wa, wd)


def _row_tile(n):
    for tm in (1024, 512, 256, 128, 64, 32, 16, 8):
        if n % tm == 0:
            return tm
    raise ValueError(f"row count {n} is not a multiple of 8")


def kernel(x, rel_bias, norm_w, w_in, q_norm_w, k_norm_w, conv_w, a_log, dt_bias, dn_norm_w, w_out):
    b, s, d = x.shape
    assert d == D_MODEL and norm_w.shape[0] == 1, "single-layer kernel"
    n = b * s
    x2 = x.reshape(n, d)
    tm = _row_tile(n)

    w_main = w_in[0][:, :MAIN_COLS].astype(BF16)
    w_gate = jnp.pad(w_in[0][:, MAIN_COLS:], ((0, 0), (0, GATE_PAD - 2 * DN_HEADS))).astype(BF16)
    qnw = jnp.tile(q_norm_w[0], ATTN_HEADS)[None, :]
    knw = jnp.tile(k_norm_w[0], ATTN_HEADS)[None, :]

    q, k, v, za, qkv, zd, gates = _inproj(x2, norm_w, w_main, w_gate, qnw, knw, tm)

    r3 = lambda t: t.reshape(b, s, t.shape[-1])
    ya = _moba(rel_bias, r3(q), r3(k), r3(v), r3(za))
    tb = 256 if s % 256 == 0 else DN_CHUNK
    w_o = w_out[0].astype(BF16)
    return _deltanet(a_log[0], dt_bias[0], r3(qkv), r3(zd), r3(gates), conv_w[0], dn_norm_w,
                     x, ya, w_o[:ATTN_WIDTH], w_o[ATTN_WIDTH:], tb)
```
